```python
import jax, jax.numpy as jnp
from jax import lax
import numpy as np

D_MODEL = 1024
BATCH = 8
SEQ = 4096
DEPTH = 1

GDN_HEADS = 4
GDN_DK = 128
GDN_DV = 128
ML_HEADS = 4
ML_DK = 128
ML_DV = 128
CONV_WIDTH = 4
CHUNK = 64
GATE_SOFTCAP = 15.0
N_EXPERTS = 32
TOP_K = 4
D_EXPERT = 1024
SWIGLU_LIMIT = 7.0
SWIGLU_ALPHA = 1.702
EXPERT_BLOCK = 256
NORM_EPS = 1e-6

GDN_QK = GDN_HEADS * GDN_DK
GDN_V = GDN_HEADS * GDN_DV
ML_QK = ML_HEADS * ML_DK
ML_V = ML_HEADS * ML_DV
IN_WIDTHS = (GDN_QK, GDN_QK, GDN_V, GDN_V, GDN_HEADS, GDN_HEADS,
             ML_QK, ML_QK, ML_V, ML_V, ML_HEADS, ML_HEADS,
             D_MODEL, D_MODEL)
D_IN = sum(IN_WIDTHS)
SPLIT_POINTS = [int(s) for s in np.cumsum(IN_WIDTHS)[:-1]]

kernel_name = "hybrid_gdn_mlstm_moe_block"


def rms_norm(x, g):
    xf = x.astype(jnp.float32)
    y = xf * lax.rsqrt(jnp.mean(xf * xf, axis=-1, keepdims=True) + NORM_EPS)
    return (y * g.astype(jnp.float32)).astype(x.dtype)


def head_rms_norm(x, g):
    return x * lax.rsqrt(jnp.mean(x * x, axis=-1, keepdims=True) + NORM_EPS) * g


def l2_normalize(x):
    return x * lax.rsqrt(jnp.sum(x * x, axis=-1, keepdims=True) + NORM_EPS)


def soft_cap(x):
    return GATE_SOFTCAP * jnp.tanh(x / GATE_SOFTCAP)


def causal_depthwise_conv(x, w):
    c = x.shape[-1]
    return lax.conv_general_dilated(
        x, w[:, None, :].astype(x.dtype), window_strides=(1,),
        padding=[(CONV_WIDTH - 1, 0)], dimension_numbers=("NWC", "WIO", "NWC"),
        feature_group_count=c)


def to_chunks(x, n_heads):
    b, s, hd = x.shape
    return x.reshape(b, s // CHUNK, CHUNK, n_heads, hd // n_heads).transpose(0, 3, 1, 2, 4)


def scalar_to_chunks(x):
    b, s, h = x.shape
    return x.reshape(b, s // CHUNK, CHUNK, h).transpose(0, 3, 1, 2)


def from_chunks(x):
    b, h, nc, c, d = x.shape
    return x.transpose(0, 2, 3, 1, 4).reshape(b, nc * c, h * d)


def causal_masks():
    idx = jnp.arange(CHUNK)
    return idx[:, None] > idx[None, :], idx[:, None] >= idx[None, :]


def gated_delta_rule(q, k, v, g, beta):
    strict, incl = causal_masks()
    gam = jnp.cumsum(g, axis=-1)
    decay = jnp.exp(jnp.where(incl, gam[..., :, None] - gam[..., None, :], -jnp.inf))
    kb = k * beta[..., None]
    lower = jnp.where(strict, jnp.einsum("bhnid,bhnjd->bhnij", kb, k) * decay, 0.0)
    eye = jnp.eye(CHUNK, dtype=q.dtype)
    dv = v.shape[-1]
    rhs = jnp.concatenate([v * beta[..., None], kb * jnp.exp(gam)[..., None]], axis=-1)
    sol = lax.linalg.triangular_solve(lower + eye, rhs, left_side=True, lower=True,
                                      unit_diagonal=True)
    u_local, w_state = sol[..., :dv], sol[..., dv:]
    attn = jnp.einsum("bhnid,bhnjd->bhnij", q, k) * decay
    q_dec = q * jnp.exp(gam)[..., None]
    gam_last = gam[..., -1]
    k_dec = k * jnp.exp(gam_last[..., None] - gam)[..., None]

    def step(state, inp):
        u0, w_c, qd, kd, a, gl = inp
        u = u0 - jnp.einsum("bhck,bhkv->bhcv", w_c, state)
        o = jnp.einsum("bhck,bhkv->bhcv", qd, state) + jnp.einsum("bhij,bhjv->bhiv", a, u)
        state = state * jnp.exp(gl)[..., None, None] + jnp.einsum("bhck,bhcv->bhkv", kd, u)
        return state, o

    b, h, _, _, dk = q.shape
    s0 = jnp.zeros((b, h, dk, dv), q.dtype)
    xs = tuple(jnp.moveaxis(t, 2, 0) for t in (u_local, w_state, q_dec, k_dec, attn, gam_last))
    _, o = lax.scan(step, s0, xs)
    return jnp.moveaxis(o, 0, 2)


def mlstm_chunkwise(q, k, v, i_pre, logf):
    _, incl = causal_masks()
    b_cum = jnp.cumsum(logf, axis=-1)
    d_log = jnp.where(incl, b_cum[..., :, None] - b_cum[..., None, :] + i_pre[..., None, :], -jnp.inf)
    m_loc = jnp.max(d_log, axis=-1)
    p = jnp.exp(d_log - m_loc[..., None]) * jnp.einsum("bhnid,bhnjd->bhnij", q, k)
    num_loc = jnp.einsum("bhnij,bhnjv->bhniv", p, v)
    den_loc = jnp.sum(p, axis=-1)
    b_last = b_cum[..., -1]
    a_log = b_last[..., None] - b_cum + i_pre
    m_chunk = jnp.max(a_log, axis=-1)
    wk = k * jnp.exp(a_log - m_chunk[..., None])[..., None]
    dc = jnp.einsum("bhnck,bhncv->bhnkv", wk, v)
    dn = jnp.sum(wk, axis=-2)

    def step(carry, inp):
        c_st, n_st, m_st = carry
        q_c, b_c, ml_c, num_c, den_c, dc_c, dn_c, bl_c, mc_c = inp
        m_inter = b_c + m_st[..., None]
        m_t = jnp.maximum(ml_c, m_inter)
        s_inter = jnp.exp(m_inter - m_t)
        s_loc = jnp.exp(ml_c - m_t)
        num = (s_inter[..., None] * jnp.einsum("bhck,bhkv->bhcv", q_c, c_st)
               + s_loc[..., None] * num_c)
        den = s_inter * jnp.einsum("bhck,bhk->bhc", q_c, n_st) + s_loc * den_c
        h = num / jnp.maximum(jnp.abs(den), jnp.exp(-m_t))[..., None]
        m_new = jnp.maximum(bl_c + m_st, mc_c)
        s_old = jnp.exp(bl_c + m_st - m_new)
        s_new = jnp.exp(mc_c - m_new)
        c_st = s_old[..., None, None] * c_st + s_new[..., None, None] * dc_c
        n_st = s_old[..., None] * n_st + s_new[..., None] * dn_c
        return (c_st, n_st, m_new), h

    b, h, _, _, dk = q.shape
    dv = v.shape[-1]
    carry0 = (jnp.zeros((b, h, dk, dv), q.dtype), jnp.zeros((b, h, dk), q.dtype),
              jnp.zeros((b, h), q.dtype))
    xs = tuple(jnp.moveaxis(t, 2, 0) for t in
               (q, b_cum, m_loc, num_loc, den_loc, dc, dn, b_last, m_chunk))
    _, hs = lax.scan(step, carry0, xs)
    return jnp.moveaxis(hs, 0, 2)


def token_mixer(h, w_in, gdn_conv, gdn_a_log, gdn_dt_bias, gdn_norm, ml_conv, ml_b_i, ml_b_f,
                ml_norm, w_up_gdn, w_up_ml, w_out):
    f32 = jnp.float32
    dt = h.dtype
    proj = (h @ w_in).astype(f32)
    (g_q, g_k, g_v, g_z, g_a, g_b, m_q, m_k, m_v, m_o, m_i, m_f,
     gate_gdn, gate_ml) = jnp.split(proj, SPLIT_POINTS, axis=-1)

    qkv = jax.nn.silu(causal_depthwise_conv(jnp.concatenate([g_q, g_k, g_v], axis=-1),
                                            gdn_conv.astype(f32)))
    a_q, a_k, a_v = jnp.split(qkv, [GDN_QK, 2 * GDN_QK], axis=-1)
    a_q = l2_normalize(to_chunks(a_q, GDN_HEADS)) * GDN_DK ** -0.5
    a_k = l2_normalize(to_chunks(a_k, GDN_HEADS))
    a_v = to_chunks(a_v, GDN_HEADS)
    log_decay = (-jnp.exp(gdn_a_log.astype(f32))[:, None, None]
                 * jax.nn.softplus(scalar_to_chunks(g_a) + gdn_dt_bias.astype(f32)[:, None, None]))
    beta = jax.nn.sigmoid(scalar_to_chunks(g_b))
    o_a = gated_delta_rule(a_q, a_k, a_v, log_decay, beta)
    o_a = from_chunks(head_rms_norm(o_a, gdn_norm.astype(f32))) * jax.nn.silu(g_z)

    qk = jax.nn.silu(causal_depthwise_conv(jnp.concatenate([m_q, m_k], axis=-1),
                                           ml_conv.astype(f32)))
    b_q, b_k = jnp.split(qk, [ML_QK], axis=-1)
    b_q = to_chunks(b_q, ML_HEADS)
    b_k = to_chunks(b_k, ML_HEADS) * ML_DK ** -0.5
    b_v = to_chunks(m_v, ML_HEADS)
    i_pre = soft_cap(scalar_to_chunks(m_i) + ml_b_i.astype(f32)[:, None, None])
    f_pre = soft_cap(scalar_to_chunks(m_f) + ml_b_f.astype(f32)[:, None, None])
    o_b = mlstm_chunkwise(b_q, b_k, b_v, i_pre, jax.nn.log_sigmoid(f_pre))
    o_b = from_chunks(head_rms_norm(o_b, ml_norm.astype(f32).reshape(ML_HEADS, 1, 1, ML_DV)))
    o_b = o_b * jax.nn.sigmoid(m_o)

    y_a = (o_a.astype(dt) @ w_up_gdn).astype(f32)
    y_b = (o_b.astype(dt) @ w_up_ml).astype(f32)
    y = jax.nn.sigmoid(gate_gdn) * y_a + jax.nn.sigmoid(gate_ml) * y_b
    return y.astype(dt) @ w_out


def moe_ffn(h, w_router, b_router, w_gate_up, b_gate_up, w_down, b_down):
    b, s, d = h.shape
    n_tok = b * s
    hf = h.reshape(n_tok, d)
    logits = (hf @ w_router + b_router).astype(jnp.float32)
    top_val, top_idx = lax.top_k(logits, TOP_K)
    gates = jax.nn.softmax(top_val, axis=-1)

    n_assign = n_tok * TOP_K
    n_blocks = -(-n_assign // EXPERT_BLOCK) + N_EXPERTS
    n_rows = n_blocks * EXPERT_BLOCK
    e_flat = top_idx.reshape(-1)
    tok_flat = jnp.repeat(jnp.arange(n_tok, dtype=jnp.int32), TOP_K)
    g_flat = gates.reshape(-1)
    order = jnp.argsort(e_flat)
    e_sorted = e_flat[order]
    counts = jnp.bincount(e_flat, length=N_EXPERTS)
    start = jnp.cumsum(counts) - counts
    padded = (counts + EXPERT_BLOCK - 1) // EXPERT_BLOCK * EXPERT_BLOCK
    pend = jnp.cumsum(padded)
    pstart = pend - padded
    rows = pstart[e_sorted] + (jnp.arange(n_assign) - start[e_sorted])
    row_tok = jnp.zeros((n_rows,), jnp.int32).at[rows].set(tok_flat[order])
    row_gate = jnp.zeros((n_rows,), h.dtype).at[rows].set(g_flat[order].astype(h.dtype))
    block_expert = jnp.minimum(
        jnp.searchsorted(pend, jnp.arange(n_blocks) * EXPERT_BLOCK, side="right"), N_EXPERTS - 1)

    def expert_block(args):
        toks, e = args
        xb = hf[toks]
        gu = xb @ w_gate_up[e] + b_gate_up[e]
        gate = jnp.minimum(gu[..., ::2], SWIGLU_LIMIT)
        up = jnp.clip(gu[..., 1::2], -SWIGLU_LIMIT, SWIGLU_LIMIT)
        act = gate * jax.nn.sigmoid(SWIGLU_ALPHA * gate) * (up + 1.0)
        return act @ w_down[e] + b_down[e]

    y_rows = lax.map(expert_block, (row_tok.reshape(n_blocks, EXPERT_BLOCK), block_expert))
    y_rows = y_rows.reshape(n_rows, d) * row_gate[:, None]
    out = jax.ops.segment_sum(y_rows, row_tok, num_segments=n_tok)
    return out.reshape(b, s, d).astype(h.dtype)


def setup_inputs(seed: int = 0) -> dict:
    key = jax.random.key(seed)
    ks = jax.random.split(key, 24)
    f32 = jnp.float32
    L = DEPTH

    def nrm(k, shape, scale):
        return jax.random.normal(k, shape, f32) * scale

    x = nrm(ks[0], (BATCH, SEQ, D_MODEL), 1.0)
    norm_mix = 1.0 + nrm(ks[1], (L, D_MODEL), 0.02)
    w_in = nrm(ks[2], (L, D_MODEL, D_IN), D_MODEL ** -0.5)
    gdn_conv = nrm(ks[3], (L, CONV_WIDTH, 2 * GDN_QK + GDN_V), CONV_WIDTH ** -0.5)
    gdn_a_log = jnp.log(jax.random.uniform(ks[4], (L, GDN_HEADS), f32, 1.0, 16.0))
    dt = jnp.exp(jax.random.uniform(ks[5], (L, GDN_HEADS), f32, float(np.log(1e-3)),
                                    float(np.log(1e-1))))
    gdn_dt_bias = dt + jnp.log(-jnp.expm1(-dt))
    gdn_norm = 1.0 + nrm(ks[6], (L, GDN_DV), 0.02)
    ml_conv = nrm(ks[7], (L, CONV_WIDTH, 2 * ML_QK), CONV_WIDTH ** -0.5)
    ml_b_i = nrm(ks[8], (L, ML_HEADS), 0.1)
    ml_b_f = jnp.linspace(3.0, 6.0, ML_HEADS, dtype=f32)[None, :] + nrm(ks[9], (L, ML_HEADS), 0.1)
    ml_norm = 1.0 + nrm(ks[10], (L, ML_V), 0.02)
    w_up_gdn = nrm(ks[11], (L, GDN_V, D_MODEL), GDN_V ** -0.5)
    w_up_ml = nrm(ks[12], (L, ML_V, D_MODEL), ML_V ** -0.5)
    w_out = nrm(ks[13], (L, D_MODEL, D_MODEL), D_MODEL ** -0.5)
    norm_ffn = 1.0 + nrm(ks[14], (L, D_MODEL), 0.02)
    w_router = nrm(ks[15], (L, D_MODEL, N_EXPERTS), D_MODEL ** -0.5)
    b_router = nrm(ks[16], (L, N_EXPERTS), 0.01)
    w_gate_up = nrm(ks[17], (L, N_EXPERTS, D_MODEL, 2 * D_EXPERT), D_MODEL ** -0.5)
    b_gate_up = nrm(ks[18], (L, N_EXPERTS, 2 * D_EXPERT), 0.01)
    w_down = nrm(ks[19], (L, N_EXPERTS, D_EXPERT, D_MODEL), D_EXPERT ** -0.5)
    b_down = nrm(ks[20], (L, N_EXPERTS, D_MODEL), 0.01)
    norm_final = 1.0 + nrm(ks[21], (D_MODEL,), 0.02)
    return {"x": x, "norm_mix": norm_mix, "w_in": w_in, "gdn_conv": gdn_conv,
            "gdn_a_log": gdn_a_log, "gdn_dt_bias": gdn_dt_bias, "gdn_norm": gdn_norm,
            "ml_conv": ml_conv, "ml_b_i": ml_b_i, "ml_b_f": ml_b_f, "ml_norm": ml_norm,
            "w_up_gdn": w_up_gdn, "w_up_ml": w_up_ml, "w_out": w_out, "norm_ffn": norm_ffn,
            "w_router": w_router, "b_router": b_router, "w_gate_up": w_gate_up,
            "b_gate_up": b_gate_up, "w_down": w_down, "b_down": b_down,
            "norm_final": norm_final}


def reference(x, norm_mix, w_in, gdn_conv, gdn_a_log, gdn_dt_bias, gdn_norm, ml_conv, ml_b_i,
              ml_b_f, ml_norm, w_up_gdn, w_up_ml, w_out, norm_ffn, w_router, b_router,
              w_gate_up, b_gate_up, w_down, b_down, norm_final):
    for l in range(DEPTH):
        h = rms_norm(x, norm_mix[l])
        x = x + token_mixer(h, w_in[l], gdn_conv[l], gdn_a_log[l], gdn_dt_bias[l], gdn_norm[l],
                            ml_conv[l], ml_b_i[l], ml_b_f[l], ml_norm[l], w_up_gdn[l],
                            w_up_ml[l], w_out[l])
        h = rms_norm(x, norm_ffn[l])
        x = x + moe_ffn(h, w_router[l], b_router[l], w_gate_up[l], b_gate_up[l], w_down[l],
                        b_down[l])
    return rms_norm(x, norm_final)
```

```python
import functools

import jax
import jax.numpy as jnp
from jax import lax
from jax.experimental import pallas as pl
from jax.experimental.pallas import tpu as pltpu

F32 = jnp.float32
BF16 = jnp.bfloat16
I32 = jnp.int32
HIGHEST = lax.Precision.HIGHEST

D_MODEL = 1024
N_HEADS = 4
D_HEAD = 128
D_QK = N_HEADS * D_HEAD
CONV_WIDTH = 4
CHUNK = 64
GATE_SOFTCAP = 15.0
N_EXPERTS = 32
TOP_K = 4
D_EXPERT = 1024
SWIGLU_LIMIT = 7.0
SWIGLU_ALPHA = 1.702
NORM_EPS = 1e-6

LANES = 128
N_SMALL = 16
N_BIG = 6 * D_MODEL
VMEM_LIMIT = 56 * 1024 * 1024

_OFF = {}
_o = 0
for _name, _w in (("g_q", D_QK), ("g_k", D_QK), ("g_v", D_QK), ("g_z", D_QK), ("g_a", N_HEADS),
                  ("g_b", N_HEADS), ("m_q", D_QK), ("m_k", D_QK), ("m_v", D_QK), ("m_o", D_QK),
                  ("m_i", N_HEADS), ("m_f", N_HEADS), ("gate_gdn", D_MODEL), ("gate_ml", D_MODEL)):
    _OFF[_name] = (_o, _w)
    _o += _w


def _cols(w, name):
    o, n = _OFF[name]
    return w[:, o:o + n]


def _bdot(a, b):
    return jnp.dot(a.astype(BF16), b.astype(BF16), preferred_element_type=F32)


def _bdot_nt(a, b):
    return lax.dot_general(a.astype(BF16), b.astype(BF16), (((1,), (1,)), ((), ())),
                           preferred_element_type=F32)


def _bdot_tn(a, b):
    return lax.dot_general(a.astype(BF16), b.astype(BF16), (((0,), (0,)), ((), ())),
                           preferred_element_type=F32)


def _fdot(a, b):
    return jnp.dot(a, b, preferred_element_type=F32, precision=HIGHEST)


def _fdot_nt(a, b):
    return lax.dot_general(a, b, (((1,), (1,)), ((), ())), preferred_element_type=F32,
                           precision=HIGHEST)


def _sigmoid(x):
    return 1.0 / (1.0 + jnp.exp(-x))


def _silu(x):
    return x * _sigmoid(x)


def _softplus(x):
    return jnp.maximum(x, 0.0) + jnp.log(1.0 + jnp.exp(-jnp.abs(x)))


def _rms(x, g):
    return x * lax.rsqrt(jnp.mean(x * x, axis=-1, keepdims=True) + NORM_EPS) * g


def _in_proj_kernel(x_ref, g_ref, wbig_ref, wsmall_ref, big_ref, small_ref, *, n_chunk):
    h = _rms(x_ref[...], g_ref[...]).astype(BF16)
    for c in range(N_BIG // n_chunk):
        sl = slice(c * n_chunk, (c + 1) * n_chunk)
        big_ref[:, sl] = jnp.dot(h, wbig_ref[:, sl], preferred_element_type=F32)
    small_ref[...] = jnp.dot(h, wsmall_ref[...], preferred_element_type=F32)


def _in_proj(x2, g, w_big, w_small, tm):
    t = x2.shape[0]
    return pl.pallas_call(
        functools.partial(_in_proj_kernel, n_chunk=1024),
        grid=(t // tm,),
        in_specs=[pl.BlockSpec((tm, D_MODEL), lambda i: (i, 0)),
                  pl.BlockSpec((1, D_MODEL), lambda i: (0, 0)),
                  pl.BlockSpec((D_MODEL, N_BIG), lambda i: (0, 0)),
                  pl.BlockSpec((D_MODEL, LANES), lambda i: (0, 0))],
        out_specs=[pl.BlockSpec((tm, N_BIG), lambda i: (i, 0)),
                   pl.BlockSpec((tm, LANES), lambda i: (i, 0))],
        out_shape=[jax.ShapeDtypeStruct((t, N_BIG), F32),
                   jax.ShapeDtypeStruct((t, LANES), F32)],
        compiler_params=pltpu.CompilerParams(dimension_semantics=("arbitrary",),
                                             vmem_limit_bytes=VMEM_LIMIT),
        name="in_proj",
    )(x2, g, w_big, w_small)


def _chunk_masks():
    r = lax.broadcasted_iota(I32, (CHUNK, CHUNK), 0)
    c = lax.broadcasted_iota(I32, (CHUNK, CHUNK), 1)
    return r > c, r >= c


def _block_cumsum_matrix(cs):
    r = lax.broadcasted_iota(I32, (cs, cs), 0)
    c = lax.broadcasted_iota(I32, (cs, cs), 1)
    same = (r // CHUNK) == (c // CHUNK)
    return jnp.where(same & (r >= c), 1.0, 0.0).astype(F32)


def _lane_onehot(lane):
    return jnp.where(lax.broadcasted_iota(I32, (CHUNK, LANES), 1) == lane, 1.0, 0.0).astype(F32)


def _conv_silu(x_ref, w_ref, buf_ref, cs):
    @pl.when(pl.program_id(1) == 0)
    def _():
        buf_ref[0:8, :] = jnp.zeros((8, buf_ref.shape[1]), F32)

    buf_ref[8:8 + cs, :] = x_ref[0]
    acc = w_ref[CONV_WIDTH - 1:CONV_WIDTH, :] * buf_ref[8:8 + cs, :]
    for j in range(CONV_WIDTH - 1):
        s = 8 - (CONV_WIDTH - 1) + j
        acc = acc + w_ref[j:j + 1, :] * buf_ref[s:s + cs, :]
    buf_ref[0:8, :] = buf_ref[cs:cs + 8, :]
    return _silu(acc)


def _gdn_kernel(qkv_ref, z_ref, sm_ref, conv_ref, prm_ref, gn_ref, o_ref,
                buf_ref, q_s, k_s, v_s, gam_s, beta_s, state_s, *, cs):
    n_chunks = cs // CHUNK

    @pl.when(pl.program_id(1) == 0)
    def _():
        state_s[...] = jnp.zeros(state_s.shape, F32)

    qkv = _conv_silu(qkv_ref, conv_ref, buf_ref, cs)
    for h in range(N_HEADS):
        sl = slice(h * D_HEAD, (h + 1) * D_HEAD)
        q = qkv[:, h * D_HEAD:(h + 1) * D_HEAD]
        k = qkv[:, D_QK + h * D_HEAD:D_QK + (h + 1) * D_HEAD]
        q_s[:, sl] = q * lax.rsqrt(jnp.sum(q * q, axis=-1, keepdims=True) + NORM_EPS) * (D_HEAD ** -0.5)
        k_s[:, sl] = k * lax.rsqrt(jnp.sum(k * k, axis=-1, keepdims=True) + NORM_EPS)
    v_s[...] = qkv[:, 2 * D_QK:]

    sm = sm_ref[0]
    logdec = -jnp.exp(prm_ref[0:1, :]) * _softplus(sm + prm_ref[1:2, :])
    gam_s[...] = _fdot(_block_cumsum_matrix(cs), logdec)
    beta_s[...] = _sigmoid(sm)

    strict, incl = _chunk_masks()
    eye = jnp.where(lax.broadcasted_iota(I32, (CHUNK, CHUNK), 0)
                    == lax.broadcasted_iota(I32, (CHUNK, CHUNK), 1), 1.0, 0.0).astype(F32)
    gnorm = gn_ref[...]

    def chunk_body(c, carry):
        rows = pl.ds(pl.multiple_of(c * CHUNK, CHUNK), CHUNK)
        gam_all = gam_s[rows, :]
        beta_all = beta_s[rows, :]
        for h in range(N_HEADS):
            sl = slice(h * D_HEAD, (h + 1) * D_HEAD)
            q = q_s[rows, sl]
            k = k_s[rows, sl]
            v = v_s[rows, sl]
            gc = gam_all[:, h:h + 1]
            beta = beta_all[:, N_HEADS + h:N_HEADS + h + 1]
            g_row = _fdot_nt(_lane_onehot(h), gam_all)
            decay = jnp.where(incl, jnp.exp(jnp.minimum(gc - g_row, 0.0)), 0.0)
            kb = k * beta
            lower = jnp.where(strict, _bdot_nt(kb, k) * decay, 0.0)
            inv = eye - lower
            pw = _fdot(lower, lower)
            for it in range(5):
                inv = inv + _fdot(inv, pw)
                if it < 4:
                    pw = _fdot(pw, pw)
            e_gc = jnp.exp(gc)
            rhs = jnp.concatenate([v * beta, kb * e_gc], axis=1)
            sol = _fdot(inv, rhs)
            u0 = sol[:, :D_HEAD]
            w = sol[:, D_HEAD:]
            attn = _bdot_nt(q, k) * decay
            g_last = gc[CHUNK - 1:CHUNK, :]
            k_dec = k * jnp.exp(g_last - gc)
            st = state_s[h]
            u = u0 - _bdot(w, st)
            o = _bdot(q * e_gc, st) + _bdot(attn, u)
            state_s[h] = st * jnp.exp(g_last) + _bdot_tn(k_dec, u)
            o = o * lax.rsqrt(jnp.mean(o * o, axis=-1, keepdims=True) + NORM_EPS) * gnorm
            o_ref[0, rows, sl] = (o * _silu(z_ref[0, rows, sl])).astype(o_ref.dtype)
        return carry

    lax.fori_loop(0, n_chunks, chunk_body, 0)


def _gdn(big3, small3, conv_w, prm, gnorm, cs):
    b, s, _ = big3.shape
    return pl.pallas_call(
        functools.partial(_gdn_kernel, cs=cs),
        grid=(b, s // cs),
        in_specs=[pl.BlockSpec((1, cs, 3 * D_QK), lambda i, j: (i, j, 0)),
                  pl.BlockSpec((1, cs, D_QK), lambda i, j: (i, j, 3)),
                  pl.BlockSpec((1, cs, LANES), lambda i, j: (i, j, 0)),
                  pl.BlockSpec((CONV_WIDTH, 3 * D_QK), lambda i, j: (0, 0)),
                  pl.BlockSpec((8, LANES), lambda i, j: (0, 0)),
                  pl.BlockSpec((1, D_HEAD), lambda i, j: (0, 0))],
        out_specs=pl.BlockSpec((1, cs, D_QK), lambda i, j: (i, j, 0)),
        out_shape=jax.ShapeDtypeStruct((b, s, D_QK), BF16),
        scratch_shapes=[pltpu.VMEM((cs + 8, 3 * D_QK), F32),
                        pltpu.VMEM((cs, D_QK), F32), pltpu.VMEM((cs, D_QK), F32),
                        pltpu.VMEM((cs, D_QK), F32),
                        pltpu.VMEM((cs, LANES), F32), pltpu.VMEM((cs, LANES), F32),
                        pltpu.VMEM((N_HEADS, D_HEAD, D_HEAD), F32)],
        compiler_params=pltpu.CompilerParams(dimension_semantics=("arbitrary", "arbitrary"),
                                             vmem_limit_bytes=VMEM_LIMIT),
        name="gdn",
    )(big3, big3, small3, conv_w, prm, gnorm)


def _mlstm_kernel(qk_ref, v_ref, og_ref, sm_ref, conv_ref, prm_ref, gn_ref, o_ref,
                  buf_ref, q_s, k_s, bcum_s, ipre_s, state_s, m_s, *, cs):
    n_chunks = cs // CHUNK

    @pl.when(pl.program_id(1) == 0)
    def _():
        state_s[...] = jnp.zeros(state_s.shape, F32)
        m_s[...] = jnp.zeros(m_s.shape, F32)

    qk = _conv_silu(qk_ref, conv_ref, buf_ref, cs)
    q_s[...] = qk[:, :D_QK]
    k_s[...] = qk[:, D_QK:] * (D_HEAD ** -0.5)

    pre = sm_ref[0] + prm_ref[0:1, :]
    capped = GATE_SOFTCAP * jnp.tanh(pre / GATE_SOFTCAP)
    logf = -_softplus(-capped)
    bcum_s[...] = _fdot(_block_cumsum_matrix(cs), logf)
    ipre_s[...] = pltpu.roll(capped, N_HEADS, axis=1)

    _, incl = _chunk_masks()
    ones_aug = jnp.ones((CHUNK, D_HEAD), F32)

    def chunk_body(c, carry):
        rows = pl.ds(pl.multiple_of(c * CHUNK, CHUNK), CHUNK)
        b_all = bcum_s[rows, :]
        i_all = ipre_s[rows, :]
        comb_all = i_all - b_all
        for h in range(N_HEADS):
            lane = 3 * N_HEADS + h
            sl = slice(h * D_HEAD, (h + 1) * D_HEAD)
            q = q_s[rows, sl]
            k = k_s[rows, sl]
            v_aug = jnp.concatenate([v_ref[0, rows, sl], ones_aug], axis=1)
            bc = b_all[:, lane:lane + 1]
            comb_row = _fdot_nt(_lane_onehot(lane), comb_all)
            d_log = jnp.where(incl, bc + comb_row, -jnp.inf)
            m_loc = jnp.max(d_log, axis=-1, keepdims=True)
            p = jnp.exp(d_log - m_loc) * _bdot_nt(q, k)
            loc = _bdot(p, v_aug)
            b_last = bc[CHUNK - 1:CHUNK, :]
            a_log = b_last + comb_all[:, lane:lane + 1]
            m_chunk = jnp.max(a_log, axis=0, keepdims=True)
            wk = k * jnp.exp(a_log - m_chunk)
            d_state = _bdot_tn(wk, v_aug)
            st = state_s[h]
            m_st = m_s[h]
            m_inter = bc + m_st
            m_t = jnp.maximum(m_loc, m_inter)
            s_inter = jnp.exp(m_inter - m_t)
            s_loc = jnp.exp(m_loc - m_t)
            tot = s_inter * _bdot(q, st) + s_loc * loc
            hh = tot[:, :D_HEAD] / jnp.maximum(jnp.abs(tot[:, D_HEAD:]), jnp.exp(-m_t))
            m_new = jnp.maximum(b_last + m_st, m_chunk)
            state_s[h] = jnp.exp(b_last + m_st - m_new) * st + jnp.exp(m_chunk - m_new) * d_state
            m_s[h] = m_new
            hh = hh * lax.rsqrt(jnp.mean(hh * hh, axis=-1, keepdims=True) + NORM_EPS) * gn_ref[:, sl]
            o_ref[0, rows, sl] = (hh * _sigmoid(og_ref[0, rows, sl])).astype(o_ref.dtype)
        return carry

    lax.fori_loop(0, n_chunks, chunk_body, 0)


def _mlstm(big3, small3, conv_w, prm, gnorm, cs):
    b, s, _ = big3.shape
    return pl.pallas_call(
        functools.partial(_mlstm_kernel, cs=cs),
        grid=(b, s // cs),
        in_specs=[pl.BlockSpec((1, cs, 2 * D_QK), lambda i, j: (i, j, 2)),
                  pl.BlockSpec((1, cs, D_QK), lambda i, j: (i, j, 6)),
                  pl.BlockSpec((1, cs, D_QK), lambda i, j: (i, j, 7)),
                  pl.BlockSpec((1, cs, LANES), lambda i, j: (i, j, 0)),
                  pl.BlockSpec((CONV_WIDTH, 2 * D_QK), lambda i, j: (0, 0)),
                  pl.BlockSpec((8, LANES), lambda i, j: (0, 0)),
                  pl.BlockSpec((1, D_QK), lambda i, j: (0, 0))],
        out_specs=pl.BlockSpec((1, cs, D_QK), lambda i, j: (i, j, 0)),
        out_shape=jax.ShapeDtypeStruct((b, s, D_QK), BF16),
        scratch_shapes=[pltpu.VMEM((cs + 8, 2 * D_QK), F32),
                        pltpu.VMEM((cs, D_QK), F32), pltpu.VMEM((cs, D_QK), F32),
                        pltpu.VMEM((cs, LANES), F32), pltpu.VMEM((cs, LANES), F32),
                        pltpu.VMEM((N_HEADS, D_HEAD, 2 * D_HEAD), F32),
                        pltpu.VMEM((N_HEADS, 1, 1), F32)],
        compiler_params=pltpu.CompilerParams(dimension_semantics=("arbitrary", "arbitrary"),
                                             vmem_limit_bytes=VMEM_LIMIT),
        name="mlstm",
    )(big3, big3, big3, small3, conv_w, prm, gnorm)


def _merge_kernel(oa_ref, ob_ref, gates_ref, x_ref, wa_ref, wb_ref, wo_ref, g_ref, wr_ref, br_ref,
                  x1_ref, h2_ref, idx_ref, gate_ref, rank_ref, cnt_ref, carry_s, *, tm):
    @pl.when(pl.program_id(0) == 0)
    def _():
        carry_s[...] = jnp.zeros(carry_s.shape, F32)

    y_a = jnp.dot(oa_ref[...], wa_ref[...], preferred_element_type=F32)
    y_b = jnp.dot(ob_ref[...], wb_ref[...], preferred_element_type=F32)
    y = _sigmoid(gates_ref[:, :D_MODEL]) * y_a + _sigmoid(gates_ref[:, D_MODEL:]) * y_b
    x1 = x_ref[...] + jnp.dot(y.astype(BF16), wo_ref[...], preferred_element_type=F32)
    x1_ref[...] = x1
    h2 = _rms(x1, g_ref[...])
    h2_ref[...] = h2
    logits = jnp.dot(h2.astype(BF16), wr_ref[...], preferred_element_type=F32) + br_ref[...]

    lane = lax.broadcasted_iota(I32, (tm, LANES), 1).astype(F32)
    vals, sels = [], []
    idx_t = jnp.zeros((tm, LANES), F32)
    work = logits
    for k in range(TOP_K):
        m = jnp.max(work, axis=-1, keepdims=True)
        i = jnp.min(jnp.where(work == m, lane, float(LANES)), axis=-1, keepdims=True)
        sel = lane == i
        work = jnp.where(sel, -jnp.inf, work)
        idx_t = jnp.where(lane == float(k), i, idx_t)
        vals.append(m)
        sels.append(sel)
    es = [jnp.exp(v - vals[0]) for v in vals]
    denom = es[0] + es[1] + es[2] + es[3]
    gate_t = jnp.zeros((tm, LANES), F32)
    for k in range(TOP_K):
        gate_t = jnp.where(lane == float(k), es[k] / denom, gate_t)
    idx_ref[...] = idx_t[:, :TOP_K].astype(I32)
    gate_ref[...] = gate_t[:, :TOP_K]

    onehot = jnp.zeros((tm, LANES), F32)
    for sel in sels:
        onehot = onehot + jnp.where(sel, 1.0, 0.0)
    r = lax.broadcasted_iota(I32, (tm, tm), 0)
    c = lax.broadcasted_iota(I32, (tm, tm), 1)
    before = jnp.where(r > c, 1.0, 0.0).astype(BF16)
    cum = jnp.dot(before, onehot.astype(BF16), preferred_element_type=F32) + carry_s[...]
    rank_t = jnp.zeros((tm, LANES), F32)
    for k, sel in enumerate(sels):
        rk = jnp.sum(jnp.where(sel, cum, 0.0), axis=-1, keepdims=True)
        rank_t = jnp.where(lane == float(k), rk, rank_t)
    rank_ref[...] = rank_t[:, :TOP_K].astype(I32)
    total = carry_s[...] + jnp.sum(onehot, axis=0, keepdims=True)
    carry_s[...] = total
    cnt_ref[...] = total.astype(I32)


def _merge(oa, ob, big, x2, wa, wb, wo, g, wr, br, tm):
    t = x2.shape[0]
    const = lambda i: (0, 0)
    return pl.pallas_call(
        functools.partial(_merge_kernel, tm=tm),
        grid=(t // tm,),
        in_specs=[pl.BlockSpec((tm, D_QK), lambda i: (i, 0)),
                  pl.BlockSpec((tm, D_QK), lambda i: (i, 0)),
                  pl.BlockSpec((tm, 2 * D_MODEL), lambda i: (i, 2)),
                  pl.BlockSpec((tm, D_MODEL), lambda i: (i, 0)),
                  pl.BlockSpec((D_QK, D_MODEL), const),
                  pl.BlockSpec((D_QK, D_MODEL), const),
                  pl.BlockSpec((D_MODEL, D_MODEL), const),
                  pl.BlockSpec((1, D_MODEL), const),
                  pl.BlockSpec((D_MODEL, LANES), const),
                  pl.BlockSpec((1, LANES), const)],
        out_specs=[pl.BlockSpec((tm, D_MODEL), lambda i: (i, 0)),
                   pl.BlockSpec((tm, D_MODEL), lambda i: (i, 0)),
                   pl.BlockSpec((tm, TOP_K), lambda i: (i, 0)),
                   pl.BlockSpec((tm, TOP_K), lambda i: (i, 0)),
                   pl.BlockSpec((tm, TOP_K), lambda i: (i, 0)),
                   pl.BlockSpec((1, LANES), const)],
        out_shape=[jax.ShapeDtypeStruct((t, D_MODEL), F32),
                   jax.ShapeDtypeStruct((t, D_MODEL), F32),
                   jax.ShapeDtypeStruct((t, TOP_K), I32),
                   jax.ShapeDtypeStruct((t, TOP_K), F32),
                   jax.ShapeDtypeStruct((t, TOP_K), I32),
                   jax.ShapeDtypeStruct((1, LANES), I32)],
        scratch_shapes=[pltpu.VMEM((1, LANES), F32)],
        compiler_params=pltpu.CompilerParams(dimension_semantics=("arbitrary",),
                                             vmem_limit_bytes=VMEM_LIMIT),
        name="merge_router",
    )(oa, ob, big, x2, wa, wb, wo, g, wr, br)


def _row_copy(src_hbm, src_row, dst_ref, dst_row, sem):
    return pltpu.make_async_copy(src_hbm.at[pl.ds(src_row, 1)], dst_ref.at[pl.ds(dst_row, 1)], sem)


def _scatter_kernel(dest_ref, h2_hbm, xs_in_hbm, xs_hbm, sem, *, tm):
    del xs_in_hbm
    base = pl.program_id(0) * tm

    def issue(t, carry):
        for k in range(TOP_K):
            _row_copy(h2_hbm, base + t, xs_hbm, dest_ref[t * TOP_K + k], sem).start()
        return carry

    lax.fori_loop(0, tm, issue, 0)

    def drain(t, carry):
        _row_copy(h2_hbm, 0, xs_hbm, 0, sem).wait()
        return carry

    lax.fori_loop(0, tm * TOP_K, drain, 0)


def _scatter_rows(dest_flat, h2, n_rows, tm):
    t = h2.shape[0]
    xs0 = jnp.zeros((n_rows, D_MODEL), F32)
    return pl.pallas_call(
        functools.partial(_scatter_kernel, tm=tm),
        grid=(t // tm,),
        in_specs=[pl.BlockSpec((tm * TOP_K,), lambda i: (i,), memory_space=pltpu.SMEM),
                  pl.BlockSpec(memory_space=pl.ANY),
                  pl.BlockSpec(memory_space=pl.ANY)],
        out_specs=pl.BlockSpec(memory_space=pl.ANY),
        out_shape=jax.ShapeDtypeStruct((n_rows, D_MODEL), F32),
        scratch_shapes=[pltpu.SemaphoreType.DMA(())],
        input_output_aliases={2: 0},
        compiler_params=pltpu.CompilerParams(dimension_semantics=("arbitrary",),
                                             has_side_effects=True),
        name="scatter_rows",
    )(dest_flat, h2, xs0)


def _expert_kernel(te_ref, nu_ref, x_ref, wg_ref, wu_ref, wd_ref, bg_ref, bu_ref, bd_ref, y_ref):
    del te_ref

    @pl.when(pl.program_id(0) < nu_ref[0])
    def _():
        x = x_ref[...].astype(BF16)
        g = jnp.dot(x, wg_ref[0], preferred_element_type=F32) + bg_ref[0]
        u = jnp.dot(x, wu_ref[0], preferred_element_type=F32) + bu_ref[0]
        gate = jnp.minimum(g, SWIGLU_LIMIT)
        up = jnp.clip(u, -SWIGLU_LIMIT, SWIGLU_LIMIT)
        act = gate * _sigmoid(SWIGLU_ALPHA * gate) * (up + 1.0)
        y_ref[...] = jnp.dot(act.astype(BF16), wd_ref[0], preferred_element_type=F32) + bd_ref[0]

    @pl.when(pl.program_id(0) >= nu_ref[0])
    def _():
        y_ref[...] = jnp.zeros(y_ref.shape, F32)


def _experts(tile_expert, n_used, xs, wg, wu, wd, bg, bu, bd, bm):
    n_rows = xs.shape[0]
    n_tiles = n_rows // bm
    row_map = lambda i, te, nu: (jnp.minimum(i, nu[0] - 1), 0)
    w_map = lambda i, te, nu: (te[i], 0, 0)
    grid_spec = pltpu.PrefetchScalarGridSpec(
        num_scalar_prefetch=2,
        grid=(n_tiles,),
        in_specs=[pl.BlockSpec((bm, D_MODEL), row_map),
                  pl.BlockSpec((1, D_MODEL, D_EXPERT), w_map),
                  pl.BlockSpec((1, D_MODEL, D_EXPERT), w_map),
                  pl.BlockSpec((1, D_EXPERT, D_MODEL), w_map),
                  pl.BlockSpec((1, 1, D_EXPERT), w_map),
                  pl.BlockSpec((1, 1, D_EXPERT), w_map),
                  pl.BlockSpec((1, 1, D_MODEL), w_map)],
        out_specs=pl.BlockSpec((bm, D_MODEL), lambda i, te, nu: (i, 0)),
    )
    return pl.pallas_call(
        _expert_kernel,
        grid_spec=grid_spec,
        out_shape=jax.ShapeDtypeStruct((n_rows, D_MODEL), F32),
        compiler_params=pltpu.CompilerParams(dimension_semantics=("arbitrary",),
                                             vmem_limit_bytes=VMEM_LIMIT),
        name="experts",
    )(tile_expert, n_used, xs, wg, wu, wd, bg, bu, bd)


def _combine_kernel(dest_ref, y_hbm, gate_ref, x1_ref, g_ref, o_ref, buf_ref, sem, *, tm):
    def issue(t, carry):
        for k in range(TOP_K):
            _row_copy(y_hbm, dest_ref[t * TOP_K + k], buf_ref.at[k], t, sem).start()
        return carry

    lax.fori_loop(0, tm, issue, 0)

    def drain(t, carry):
        _row_copy(y_hbm, 0, buf_ref.at[0], 0, sem).wait()
        return carry

    lax.fori_loop(0, tm * TOP_K, drain, 0)

    acc = x1_ref[...]
    for k in range(TOP_K):
        acc = acc + gate_ref[:, k:k + 1] * buf_ref[k]
    o_ref[...] = _rms(acc, g_ref[...])


def _combine(dest_flat, y_rows, gates, x1, g, tm):
    t = x1.shape[0]
    return pl.pallas_call(
        functools.partial(_combine_kernel, tm=tm),
        grid=(t // tm,),
        in_specs=[pl.BlockSpec((tm * TOP_K,), lambda i: (i,), memory_space=pltpu.SMEM),
                  pl.BlockSpec(memory_space=pl.ANY),
                  pl.BlockSpec((tm, TOP_K), lambda i: (i, 0)),
                  pl.BlockSpec((tm, D_MODEL), lambda i: (i, 0)),
                  pl.BlockSpec((1, D_MODEL), lambda i: (0, 0))],
        out_specs=pl.BlockSpec((tm, D_MODEL), lambda i: (i, 0)),
        out_shape=jax.ShapeDtypeStruct((t, D_MODEL), F32),
        scratch_shapes=[pltpu.VMEM((TOP_K, tm, D_MODEL), F32), pltpu.SemaphoreType.DMA(())],
        compiler_params=pltpu.CompilerParams(dimension_semantics=("arbitrary",),
                                             vmem_limit_bytes=VMEM_LIMIT),
        name="combine",
    )(dest_flat, y_rows, gates, x1, g)


def _lane_row(vec, lane0):
    n = vec.shape[0]
    return jnp.zeros((8, LANES), F32).at[0, lane0:lane0 + n].set(vec.astype(F32))


def _tile_size(n, pref):
    return pref if n % pref == 0 else n


def kernel(x, norm_mix, w_in, gdn_conv, gdn_a_log, gdn_dt_bias, gdn_norm, ml_conv, ml_b_i, ml_b_f,
           ml_norm, w_up_gdn, w_up_ml, w_out, norm_ffn, w_router, b_router, w_gate_up, b_gate_up,
           w_down, b_down, norm_final):
    assert norm_mix.shape[0] == 1, "single-layer stack"
    b, s, d = x.shape
    assert d == D_MODEL and s % CHUNK == 0
    t = b * s
    x2 = x.reshape(t, d)

    w = w_in[0]
    w_big = jnp.concatenate([_cols(w, n) for n in ("g_q", "g_k", "g_v", "g_z", "m_q", "m_k", "m_v",
                                                   "m_o", "gate_gdn", "gate_ml")], axis=1).astype(BF16)
    w_small = jnp.concatenate([_cols(w, n) for n in ("g_a", "g_b", "m_i", "m_f")], axis=1)
    w_small = jnp.pad(w_small, ((0, 0), (0, LANES - N_SMALL))).astype(BF16)

    tm = _tile_size(t, 512)
    big, small = _in_proj(x2, norm_mix[0][None, :], w_big, w_small, tm)
    big3 = big.reshape(b, s, N_BIG)
    small3 = small.reshape(b, s, LANES)

    cs = _tile_size(s, 512)
    gdn_prm = _lane_row(gdn_a_log[0], 0).at[1, 0:N_HEADS].set(gdn_dt_bias[0].astype(F32))
    oa = _gdn(big3, small3, gdn_conv[0].astype(F32), gdn_prm, gdn_norm[0][None, :].astype(F32), cs)
    ml_prm = _lane_row(ml_b_i[0], 2 * N_HEADS).at[0, 3 * N_HEADS:4 * N_HEADS].set(ml_b_f[0].astype(F32))
    ob = _mlstm(big3, small3, ml_conv[0].astype(F32), ml_prm, ml_norm[0][None, :].astype(F32), cs)

    w_r = jnp.pad(w_router[0], ((0, 0), (0, LANES - N_EXPERTS))).astype(BF16)
    b_r = jnp.full((1, LANES), -1e30, F32).at[0, :N_EXPERTS].set(b_router[0].astype(F32))
    x1, h2, idx, gates, rank, counts = _merge(
        oa.reshape(t, D_QK), ob.reshape(t, D_QK), big, x2, w_up_gdn[0].astype(BF16),
        w_up_ml[0].astype(BF16), w_out[0].astype(BF16), norm_ffn[0][None, :], w_r, b_r, tm)

    bm = 512
    n_assign = t * TOP_K
    n_tiles = -(-n_assign // bm) + N_EXPERTS
    counts = counts[0, :N_EXPERTS]
    padded = (counts + bm - 1) // bm * bm
    pend = jnp.cumsum(padded)
    pstart = pend - padded
    dest = (pstart[idx] + rank).reshape(-1).astype(I32)
    n_used = (pend[-1] // bm).astype(I32)
    tile_ids = jnp.minimum(jnp.arange(n_tiles, dtype=I32), n_used - 1)
    tile_expert = jnp.minimum(jnp.searchsorted(pend, tile_ids * bm, side="right"),
                              N_EXPERTS - 1).astype(I32)

    tm_row = _tile_size(t, 256)
    xs = _scatter_rows(dest, h2, n_tiles * bm, tm_row)
    wgu = w_gate_up[0]
    bgu = b_gate_up[0]
    y_rows = _experts(tile_expert, n_used.reshape(1), xs,
                      wgu[:, :, 0::2].astype(BF16), wgu[:, :, 1::2].astype(BF16),
                      w_down[0].astype(BF16),
                      bgu[:, None, 0::2].astype(F32), bgu[:, None, 1::2].astype(F32),
                      b_down[0][:, None, :].astype(F32), bm)
    out = _combine(dest, y_rows, gates, x1, norm_final[None, :], tm_row)
    return out.reshape(b, s, d)
```

```python
import functools

import jax
import jax.numpy as jnp
from jax import lax
from jax.experimental import pallas as pl
from jax.experimental.pallas import tpu as pltpu

F32 = jnp.float32
BF16 = jnp.bfloat16
I32 = jnp.int32

D_MODEL = 1024
N_HEADS = 4
D_HEAD = 128
D_QK = N_HEADS * D_HEAD
CONV_WIDTH = 4
CHUNK = 64
GATE_SOFTCAP = 15.0
N_EXPERTS = 32
TOP_K = 4
D_EXPERT = 1024
SWIGLU_LIMIT = 7.0
SWIGLU_ALPHA = 1.702
NORM_EPS = 1e-6

LANES = 128
SUBLANES = 8
N_SMALL = 16
N_BIG = 6 * D_MODEL
VMEM_LIMIT = 56 * 1024 * 1024
EXPERT_TILE = 512
DMA_UNROLL = 8

_OFF = {}
_o = 0
for _name, _w in (("g_q", D_QK), ("g_k", D_QK), ("g_v", D_QK), ("g_z", D_QK), ("g_a", N_HEADS),
                  ("g_b", N_HEADS), ("m_q", D_QK), ("m_k", D_QK), ("m_v", D_QK), ("m_o", D_QK),
                  ("m_i", N_HEADS), ("m_f", N_HEADS), ("gate_gdn", D_MODEL), ("gate_ml", D_MODEL)):
    _OFF[_name] = (_o, _w)
    _o += _w


def _cols(w, name):
    o, n = _OFF[name]
    return w[:, o:o + n]


_NN = (((1,), (0,)), ((), ()))
_NT = (((1,), (1,)), ((), ()))
_TN = (((0,), (0,)), ((), ()))


def _bdot(a, b, dims=_NN):
    return lax.dot_general(a.astype(BF16), b.astype(BF16), dims, preferred_element_type=F32)


def _split2(a):
    hi = a.astype(BF16)
    return hi, (a - hi.astype(F32)).astype(BF16)


def _split3(a):
    hi = a.astype(BF16)
    r = a - hi.astype(F32)
    mid = r.astype(BF16)
    return hi, mid, (r - mid.astype(F32)).astype(BF16)


def _dot3(a, b, dims=_NN):
    ah, al = _split2(a)
    bh, bl = _split2(b)
    d = functools.partial(lax.dot_general, dimension_numbers=dims, preferred_element_type=F32)
    return d(ah, bh) + (d(ah, bl) + d(al, bh))


def _select_dot(sel01, x, dims=_NN):
    s = sel01.astype(BF16)
    h, m, l = _split3(x)
    d = functools.partial(lax.dot_general, dimension_numbers=dims, preferred_element_type=F32)
    return d(s, h) + (d(s, m) + d(s, l))


def _sigmoid(x):
    return 1.0 / (1.0 + jnp.exp(-x))


def _silu(x):
    return x * _sigmoid(x)


def _softplus(x):
    return jnp.maximum(x, 0.0) + jnp.log(1.0 + jnp.exp(-jnp.abs(x)))


def _rms(x, g):
    return x * lax.rsqrt(jnp.mean(x * x, axis=-1, keepdims=True) + NORM_EPS) * g


def _in_proj_kernel(x_ref, g_ref, wbig_ref, wsmall_ref, big_ref, small_ref, *, n_chunk):
    h = _rms(x_ref[...], g_ref[...]).astype(BF16)
    for c in range(N_BIG // n_chunk):
        sl = slice(c * n_chunk, (c + 1) * n_chunk)
        big_ref[:, sl] = jnp.dot(h, wbig_ref[:, sl], preferred_element_type=F32)
    small_ref[...] = jnp.dot(h, wsmall_ref[...], preferred_element_type=F32)


def _in_proj(x2, g, w_big, w_small, tm):
    t = x2.shape[0]
    return pl.pallas_call(
        functools.partial(_in_proj_kernel, n_chunk=1024),
        grid=(t // tm,),
        in_specs=[pl.BlockSpec((tm, D_MODEL), lambda i: (i, 0)),
                  pl.BlockSpec((1, D_MODEL), lambda i: (0, 0)),
                  pl.BlockSpec((D_MODEL, N_BIG), lambda i: (0, 0)),
                  pl.BlockSpec((D_MODEL, LANES), lambda i: (0, 0))],
        out_specs=[pl.BlockSpec((tm, N_BIG), lambda i: (i, 0)),
                   pl.BlockSpec((tm, LANES), lambda i: (i, 0))],
        out_shape=[jax.ShapeDtypeStruct((t, N_BIG), F32),
                   jax.ShapeDtypeStruct((t, LANES), F32)],
        compiler_params=pltpu.CompilerParams(dimension_semantics=("arbitrary",),
                                             vmem_limit_bytes=VMEM_LIMIT),
        name="in_proj",
    )(x2, g, w_big, w_small)


def _chunk_masks():
    r = lax.broadcasted_iota(I32, (CHUNK, CHUNK), 0)
    c = lax.broadcasted_iota(I32, (CHUNK, CHUNK), 1)
    return r > c, r >= c


def _block_cumsum_matrix(cs):
    r = lax.broadcasted_iota(I32, (cs, cs), 0)
    c = lax.broadcasted_iota(I32, (cs, cs), 1)
    same = (r // CHUNK) == (c // CHUNK)
    return jnp.where(same & (r >= c), 1.0, 0.0).astype(F32)


def _lane_onehot(lane):
    return jnp.where(lax.broadcasted_iota(I32, (CHUNK, LANES), 1) == lane, 1.0, 0.0).astype(F32)


def _conv_silu(x_ref, w_ref, buf_ref, cs):
    @pl.when(pl.program_id(1) == 0)
    def _():
        buf_ref[0:8, :] = jnp.zeros((8, buf_ref.shape[1]), F32)

    buf_ref[8:8 + cs, :] = x_ref[0]
    acc = w_ref[CONV_WIDTH - 1:CONV_WIDTH, :] * buf_ref[8:8 + cs, :]
    for j in range(CONV_WIDTH - 1):
        s = 8 - (CONV_WIDTH - 1) + j
        acc = acc + w_ref[j:j + 1, :] * buf_ref[s:s + cs, :]
    buf_ref[0:8, :] = buf_ref[cs:cs + 8, :]
    return _silu(acc)


def _gdn_kernel(qkv_ref, z_ref, sm_ref, conv_ref, prm_ref, gn_ref, o_ref,
                buf_ref, q_s, k_s, v_s, w_s, attn_s, gam_s, beta_s, state_s, *, cs):
    n_chunks = cs // CHUNK

    @pl.when(pl.program_id(1) == 0)
    def _():
        state_s[...] = jnp.zeros(state_s.shape, F32)

    qkv = _conv_silu(qkv_ref, conv_ref, buf_ref, cs)
    for h in range(N_HEADS):
        sl = slice(h * D_HEAD, (h + 1) * D_HEAD)
        q = qkv[:, h * D_HEAD:(h + 1) * D_HEAD]
        k = qkv[:, D_QK + h * D_HEAD:D_QK + (h + 1) * D_HEAD]
        q_s[:, sl] = q * lax.rsqrt(jnp.sum(q * q, axis=-1, keepdims=True) + NORM_EPS) * (D_HEAD ** -0.5)
        k_s[:, sl] = k * lax.rsqrt(jnp.sum(k * k, axis=-1, keepdims=True) + NORM_EPS)
    v_s[...] = qkv[:, 2 * D_QK:]

    sm = sm_ref[0]
    logdec = -jnp.exp(prm_ref[0:1, :]) * _softplus(sm + prm_ref[1:2, :])
    gam_s[...] = _select_dot(_block_cumsum_matrix(cs), logdec)
    beta_s[...] = _sigmoid(sm)

    strict, incl = _chunk_masks()
    eye = jnp.where(lax.broadcasted_iota(I32, (CHUNK, CHUNK), 0)
                    == lax.broadcasted_iota(I32, (CHUNK, CHUNK), 1), 1.0, 0.0).astype(F32)
    gnorm = gn_ref[...]

    def local_body(c, carry):
        rows = pl.ds(pl.multiple_of(c * CHUNK, CHUNK), CHUNK)
        gam_all = gam_s[rows, :]
        beta_all = beta_s[rows, :]
        for h in range(N_HEADS):
            sl = slice(h * D_HEAD, (h + 1) * D_HEAD)
            q = q_s[rows, sl]
            k = k_s[rows, sl]
            v = v_s[rows, sl]
            gc = gam_all[:, h:h + 1]
            beta = beta_all[:, N_HEADS + h:N_HEADS + h + 1]
            g_row = _select_dot(_lane_onehot(h), gam_all, _NT)
            decay = jnp.where(incl, jnp.exp(jnp.minimum(gc - g_row, 0.0)), 0.0)
            kb = k * beta
            lower = jnp.where(strict, _bdot(kb, k, _NT) * decay, 0.0)
            inv = eye - lower
            pw = _dot3(lower, lower)
            for it in range(5):
                inv = inv + _dot3(inv, pw)
                if it < 4:
                    pw = _dot3(pw, pw)
            e_gc = jnp.exp(gc)
            rhs = jnp.concatenate([v * beta, kb * e_gc], axis=1)
            sol = _dot3(inv, rhs)
            v_s[rows, sl] = sol[:, :D_HEAD]
            w_s[rows, sl] = sol[:, D_HEAD:]
            attn_s[rows, h * CHUNK:(h + 1) * CHUNK] = _bdot(q, k, _NT) * decay
            q_s[rows, sl] = q * e_gc
            k_s[rows, sl] = k * jnp.exp(gc[CHUNK - 1:CHUNK, :] - gc)
        return carry

    lax.fori_loop(0, n_chunks, local_body, 0)

    def state_body(c, carry):
        rows = pl.ds(pl.multiple_of(c * CHUNK, CHUNK), CHUNK)
        g_last = gam_s[pl.ds(c * CHUNK + CHUNK - 1, 1), :]
        for h in range(N_HEADS):
            sl = slice(h * D_HEAD, (h + 1) * D_HEAD)
            st = state_s[h]
            u = v_s[rows, sl] - _bdot(w_s[rows, sl], st)
            o = _bdot(q_s[rows, sl], st) + _bdot(attn_s[rows, h * CHUNK:(h + 1) * CHUNK], u)
            state_s[h] = st * jnp.exp(g_last[:, h:h + 1]) + _bdot(k_s[rows, sl], u, _TN)
            o = o * lax.rsqrt(jnp.mean(o * o, axis=-1, keepdims=True) + NORM_EPS) * gnorm
            o_ref[0, rows, sl] = (o * _silu(z_ref[0, rows, sl])).astype(o_ref.dtype)
        return carry

    lax.fori_loop(0, n_chunks, state_body, 0)


def _gdn(big3, small3, conv_w, prm, gnorm, cs):
    b, s, _ = big3.shape
    return pl.pallas_call(
        functools.partial(_gdn_kernel, cs=cs),
        grid=(b, s // cs),
        in_specs=[pl.BlockSpec((1, cs, 3 * D_QK), lambda i, j: (i, j, 0)),
                  pl.BlockSpec((1, cs, D_QK), lambda i, j: (i, j, 3)),
                  pl.BlockSpec((1, cs, LANES), lambda i, j: (i, j, 0)),
                  pl.BlockSpec((CONV_WIDTH, 3 * D_QK), lambda i, j: (0, 0)),
                  pl.BlockSpec((8, LANES), lambda i, j: (0, 0)),
                  pl.BlockSpec((1, D_HEAD), lambda i, j: (0, 0))],
        out_specs=pl.BlockSpec((1, cs, D_QK), lambda i, j: (i, j, 0)),
        out_shape=jax.ShapeDtypeStruct((b, s, D_QK), BF16),
        scratch_shapes=[pltpu.VMEM((cs + 8, 3 * D_QK), F32),
                        pltpu.VMEM((cs, D_QK), F32), pltpu.VMEM((cs, D_QK), F32),
                        pltpu.VMEM((cs, D_QK), F32), pltpu.VMEM((cs, D_QK), F32),
                        pltpu.VMEM((cs, N_HEADS * CHUNK), F32),
                        pltpu.VMEM((cs, LANES), F32), pltpu.VMEM((cs, LANES), F32),
                        pltpu.VMEM((N_HEADS, D_HEAD, D_HEAD), F32)],
        compiler_params=pltpu.CompilerParams(dimension_semantics=("arbitrary", "arbitrary"),
                                             vmem_limit_bytes=VMEM_LIMIT),
        name="gdn",
    )(big3, big3, small3, conv_w, prm, gnorm)


def _mlstm_kernel(qk_ref, v_ref, og_ref, sm_ref, conv_ref, prm_ref, gn_ref, o_ref,
                  buf_ref, q_s, k_s, bcum_s, ipre_s, state_s, m_s, *, cs):
    n_chunks = cs // CHUNK

    @pl.when(pl.program_id(1) == 0)
    def _():
        state_s[...] = jnp.zeros(state_s.shape, F32)
        m_s[...] = jnp.zeros(m_s.shape, F32)

    qk = _conv_silu(qk_ref, conv_ref, buf_ref, cs)
    q_s[...] = qk[:, :D_QK]
    k_s[...] = qk[:, D_QK:] * (D_HEAD ** -0.5)

    pre = sm_ref[0] + prm_ref[0:1, :]
    capped = GATE_SOFTCAP * jnp.tanh(pre / GATE_SOFTCAP)
    logf = -_softplus(-capped)
    bcum_s[...] = _select_dot(_block_cumsum_matrix(cs), logf)
    ipre_s[...] = pltpu.roll(capped, N_HEADS, axis=1)

    _, incl = _chunk_masks()
    ones_aug = jnp.ones((CHUNK, D_HEAD), F32)

    def chunk_body(c, carry):
        rows = pl.ds(pl.multiple_of(c * CHUNK, CHUNK), CHUNK)
        b_all = bcum_s[rows, :]
        i_all = ipre_s[rows, :]
        comb_all = i_all - b_all
        for h in range(N_HEADS):
            lane = 3 * N_HEADS + h
            sl = slice(h * D_HEAD, (h + 1) * D_HEAD)
            q = q_s[rows, sl]
            k = k_s[rows, sl]
            v_aug = jnp.concatenate([v_ref[0, rows, sl], ones_aug], axis=1)
            bc = b_all[:, lane:lane + 1]
            comb_row = _select_dot(_lane_onehot(lane), comb_all, _NT)
            d_log = jnp.where(incl, bc + comb_row, -jnp.inf)
            m_loc = jnp.max(d_log, axis=-1, keepdims=True)
            p = jnp.exp(d_log - m_loc) * _bdot(q, k, _NT)
            loc = _bdot(p, v_aug)
            b_last = bc[CHUNK - 1:CHUNK, :]
            a_log = b_last + comb_all[:, lane:lane + 1]
            m_chunk = jnp.max(a_log, axis=0, keepdims=True)
            wk = k * jnp.exp(a_log - m_chunk)
            d_state = _bdot(wk, v_aug, _TN)
            st = state_s[h]
            m_st = m_s[h]
            m_inter = bc + m_st
            m_t = jnp.maximum(m_loc, m_inter)
            s_inter = jnp.exp(m_inter - m_t)
            s_loc = jnp.exp(m_loc - m_t)
            tot = s_inter * _bdot(q, st) + s_loc * loc
            hh = tot[:, :D_HEAD] / jnp.maximum(jnp.abs(tot[:, D_HEAD:]), jnp.exp(-m_t))
            m_new = jnp.maximum(b_last + m_st, m_chunk)
            state_s[h] = jnp.exp(b_last + m_st - m_new) * st + jnp.exp(m_chunk - m_new) * d_state
            m_s[h] = m_new
            hh = hh * lax.rsqrt(jnp.mean(hh * hh, axis=-1, keepdims=True) + NORM_EPS) * gn_ref[:, sl]
            o_ref[0, rows, sl] = (hh * _sigmoid(og_ref[0, rows, sl])).astype(o_ref.dtype)
        return carry

    lax.fori_loop(0, n_chunks, chunk_body, 0)


def _mlstm(big3, small3, conv_w, prm, gnorm, cs):
    b, s, _ = big3.shape
    return pl.pallas_call(
        functools.partial(_mlstm_kernel, cs=cs),
        grid=(b, s // cs),
        in_specs=[pl.BlockSpec((1, cs, 2 * D_QK), lambda i, j: (i, j, 2)),
                  pl.BlockSpec((1, cs, D_QK), lambda i, j: (i, j, 6)),
                  pl.BlockSpec((1, cs, D_QK), lambda i, j: (i, j, 7)),
                  pl.BlockSpec((1, cs, LANES), lambda i, j: (i, j, 0)),
                  pl.BlockSpec((CONV_WIDTH, 2 * D_QK), lambda i, j: (0, 0)),
                  pl.BlockSpec((8, LANES), lambda i, j: (0, 0)),
                  pl.BlockSpec((1, D_QK), lambda i, j: (0, 0))],
        out_specs=pl.BlockSpec((1, cs, D_QK), lambda i, j: (i, j, 0)),
        out_shape=jax.ShapeDtypeStruct((b, s, D_QK), BF16),
        scratch_shapes=[pltpu.VMEM((cs + 8, 2 * D_QK), F32),
                        pltpu.VMEM((cs, D_QK), F32), pltpu.VMEM((cs, D_QK), F32),
                        pltpu.VMEM((cs, LANES), F32), pltpu.VMEM((cs, LANES), F32),
                        pltpu.VMEM((N_HEADS, D_HEAD, 2 * D_HEAD), F32),
                        pltpu.VMEM((N_HEADS, 1, 1), F32)],
        compiler_params=pltpu.CompilerParams(dimension_semantics=("arbitrary", "arbitrary"),
                                             vmem_limit_bytes=VMEM_LIMIT),
        name="mlstm",
    )(big3, big3, big3, small3, conv_w, prm, gnorm)


N_SEG = D_MODEL // LANES
assert N_SEG == SUBLANES


def _store_token_tiles(ref, val, n):
    for s in range(N_SEG):
        ref[pl.ds(s, n, stride=N_SEG), :] = val[:, s * LANES:(s + 1) * LANES]


def _load_token_tiles(ref, n):
    return jnp.concatenate([ref[pl.ds(s, n, stride=N_SEG), :] for s in range(N_SEG)], axis=1)


def _merge_kernel(oa_ref, ob_ref, gates_ref, x_ref, wa_ref, wb_ref, wo_ref, g_ref, wr_ref, br_ref,
                  x1_ref, h2_ref, idx_ref, gate_ref, rank_ref, cnt_ref, carry_s, *, tm):
    @pl.when(pl.program_id(0) == 0)
    def _():
        carry_s[...] = jnp.zeros(carry_s.shape, F32)

    y_a = jnp.dot(oa_ref[...], wa_ref[...], preferred_element_type=F32)
    y_b = jnp.dot(ob_ref[...], wb_ref[...], preferred_element_type=F32)
    y = _sigmoid(gates_ref[:, :D_MODEL]) * y_a + _sigmoid(gates_ref[:, D_MODEL:]) * y_b
    x1 = x_ref[...] + jnp.dot(y.astype(BF16), wo_ref[...], preferred_element_type=F32)
    x1_ref[...] = x1
    h2 = _rms(x1, g_ref[...])
    _store_token_tiles(h2_ref, h2, tm)
    logits = jnp.dot(h2.astype(BF16), wr_ref[...], preferred_element_type=F32) + br_ref[...]

    lane = lax.broadcasted_iota(I32, (tm, LANES), 1).astype(F32)
    vals, sels = [], []
    idx_t = jnp.zeros((tm, LANES), F32)
    work = logits
    for k in range(TOP_K):
        m = jnp.max(work, axis=-1, keepdims=True)
        i = jnp.min(jnp.where(work == m, lane, float(LANES)), axis=-1, keepdims=True)
        sel = lane == i
        work = jnp.where(sel, -jnp.inf, work)
        idx_t = jnp.where(lane == float(k), i, idx_t)
        vals.append(m)
        sels.append(sel)
    es = [jnp.exp(v - vals[0]) for v in vals]
    denom = es[0] + es[1] + es[2] + es[3]
    gate_t = jnp.zeros((tm, LANES), F32)
    for k in range(TOP_K):
        gate_t = jnp.where(lane == float(k), es[k] / denom, gate_t)
    idx_ref[...] = idx_t[:, :TOP_K].astype(I32)
    gate_ref[...] = gate_t[:, :TOP_K]

    onehot = jnp.zeros((tm, LANES), F32)
    for sel in sels:
        onehot = onehot + jnp.where(sel, 1.0, 0.0)
    r = lax.broadcasted_iota(I32, (tm, tm), 0)
    c = lax.broadcasted_iota(I32, (tm, tm), 1)
    before = jnp.where(r > c, 1.0, 0.0).astype(BF16)
    cum = jnp.dot(before, onehot.astype(BF16), preferred_element_type=F32) + carry_s[...]
    rank_t = jnp.zeros((tm, LANES), F32)
    for k, sel in enumerate(sels):
        rk = jnp.sum(jnp.where(sel, cum, 0.0), axis=-1, keepdims=True)
        rank_t = jnp.where(lane == float(k), rk, rank_t)
    rank_ref[...] = rank_t[:, :TOP_K].astype(I32)
    total = carry_s[...] + jnp.sum(onehot, axis=0, keepdims=True)
    carry_s[...] = total
    cnt_ref[...] = total.astype(I32)


def _merge(oa, ob, big, x2, wa, wb, wo, g, wr, br, tm):
    t = x2.shape[0]
    const = lambda i: (0, 0)
    return pl.pallas_call(
        functools.partial(_merge_kernel, tm=tm),
        grid=(t // tm,),
        in_specs=[pl.BlockSpec((tm, D_QK), lambda i: (i, 0)),
                  pl.BlockSpec((tm, D_QK), lambda i: (i, 0)),
                  pl.BlockSpec((tm, 2 * D_MODEL), lambda i: (i, 2)),
                  pl.BlockSpec((tm, D_MODEL), lambda i: (i, 0)),
                  pl.BlockSpec((D_QK, D_MODEL), const),
                  pl.BlockSpec((D_QK, D_MODEL), const),
                  pl.BlockSpec((D_MODEL, D_MODEL), const),
                  pl.BlockSpec((1, D_MODEL), const),
                  pl.BlockSpec((D_MODEL, LANES), const),
                  pl.BlockSpec((1, LANES), const)],
        out_specs=[pl.BlockSpec((tm, D_MODEL), lambda i: (i, 0)),
                   pl.BlockSpec((tm * N_SEG, LANES), lambda i: (i, 0)),
                   pl.BlockSpec((tm, TOP_K), lambda i: (i, 0)),
                   pl.BlockSpec((tm, TOP_K), lambda i: (i, 0)),
                   pl.BlockSpec((tm, TOP_K), lambda i: (i, 0)),
                   pl.BlockSpec((1, LANES), const)],
        out_shape=[jax.ShapeDtypeStruct((t, D_MODEL), F32),
                   jax.ShapeDtypeStruct((t * N_SEG, LANES), F32),
                   jax.ShapeDtypeStruct((t, TOP_K), I32),
                   jax.ShapeDtypeStruct((t, TOP_K), F32),
                   jax.ShapeDtypeStruct((t, TOP_K), I32),
                   jax.ShapeDtypeStruct((1, LANES), I32)],
        scratch_shapes=[pltpu.VMEM((1, LANES), F32)],
        compiler_params=pltpu.CompilerParams(dimension_semantics=("arbitrary",),
                                             vmem_limit_bytes=VMEM_LIMIT),
        name="merge_router",
    )(oa, ob, big, x2, wa, wb, wo, g, wr, br)


def _expert_kernel(te_ref, nu_ref, nv_ref, tok_ref, tok_next_ref, slot_ref, h2_hbm,
                   wg_ref, wu_ref, wd_ref, bg_ref, bu_ref, bd_ref, y_hbm,
                   xbuf, ybuf, in_sem, out_sem, *, bm):
    del te_ref
    i = pl.program_id(0)
    n_used = nu_ref[0]
    slot = lax.rem(i, 2)

    def token_tile(ref, r):
        return ref.at[pl.ds(pl.multiple_of(r * N_SEG, N_SEG), N_SEG)]

    def gather(ids_ref, s):
        def body(r, carry):
            pltpu.make_async_copy(token_tile(h2_hbm, ids_ref[r]), token_tile(xbuf.at[s], r),
                                  in_sem.at[s]).start()
            return carry
        lax.fori_loop(0, bm, body, 0, unroll=DMA_UNROLL)

    def wait_scatter(s, n_rows):
        n = pl.multiple_of(n_rows * N_SEG, N_SEG)
        pltpu.make_async_copy(ybuf.at[s, pl.ds(0, n)], y_hbm.at[pl.ds(0, n)], out_sem.at[s]).wait()

    @pl.when(i == 0)
    def _():
        gather(tok_ref, 0)

    @pl.when(i < n_used)
    def _():
        pltpu.make_async_copy(h2_hbm.at[pl.ds(0, bm * N_SEG)], xbuf.at[slot], in_sem.at[slot]).wait()

        @pl.when(i + 1 < n_used)
        def _():
            gather(tok_next_ref, 1 - slot)

        @pl.when(i >= 2)
        def _():
            wait_scatter(slot, nv_ref[i - 2])

        x = _load_token_tiles(xbuf.at[slot], bm).astype(BF16)
        g = lax.dot_general(x, wg_ref[0], _NT, preferred_element_type=F32) + bg_ref[0]
        u = lax.dot_general(x, wu_ref[0], _NT, preferred_element_type=F32) + bu_ref[0]
        gate = jnp.minimum(g, SWIGLU_LIMIT)
        up = jnp.clip(u, -SWIGLU_LIMIT, SWIGLU_LIMIT)
        act = gate * _sigmoid(SWIGLU_ALPHA * gate) * (up + 1.0)
        y = jnp.dot(act.astype(BF16), wd_ref[0], preferred_element_type=F32) + bd_ref[0]
        _store_token_tiles(ybuf.at[slot], y, bm)

        n_valid = nv_ref[i]

        def scatter(r, carry):
            pltpu.make_async_copy(token_tile(ybuf.at[slot], r), token_tile(y_hbm, slot_ref[r]),
                                  out_sem.at[slot]).start()
            return carry

        def scatter_group(j, carry):
            for q in range(DMA_UNROLL):
                scatter(j * DMA_UNROLL + q, carry)
            return carry
        n_groups = n_valid // DMA_UNROLL
        lax.fori_loop(0, n_groups, scatter_group, 0)
        lax.fori_loop(n_groups * DMA_UNROLL, n_valid, scatter, 0)

        @pl.when(i == n_used - 1)
        def _():
            wait_scatter(slot, n_valid)

            @pl.when(i >= 1)
            def _():
                wait_scatter(1 - slot, nv_ref[i - 1])


def _experts(tile_expert, n_used, tile_valid, row_tok, row_slot, h2, wg_t, wu_t, wd, bg, bu, bd, bm):
    n_rows = row_tok.shape[0]
    n_tiles = n_rows // bm
    n_slots = h2.shape[0] // N_SEG * TOP_K
    cur =lambda i, te, nu, nv: (jnp.minimum(i, nu[0] - 1),)
    nxt = lambda i, te, nu, nv: (jnp.minimum(i + 1, nu[0] - 1),)
    w_map = lambda i, te, nu, nv: (te[i], 0, 0)
    grid_spec = pltpu.PrefetchScalarGridSpec(
        num_scalar_prefetch=3,
        grid=(n_tiles,),
        in_specs=[pl.BlockSpec((bm,), cur, memory_space=pltpu.SMEM),
                  pl.BlockSpec((bm,), nxt, memory_space=pltpu.SMEM),
                  pl.BlockSpec((bm,), cur, memory_space=pltpu.SMEM),
                  pl.BlockSpec(memory_space=pl.ANY),
                  pl.BlockSpec((1, D_EXPERT, D_MODEL), w_map),
                  pl.BlockSpec((1, D_EXPERT, D_MODEL), w_map),
                  pl.BlockSpec((1, D_EXPERT, D_MODEL), w_map),
                  pl.BlockSpec((1, 1, D_EXPERT), w_map),
                  pl.BlockSpec((1, 1, D_EXPERT), w_map),
                  pl.BlockSpec((1, 1, D_MODEL), w_map)],
        out_specs=pl.BlockSpec(memory_space=pl.ANY),
        scratch_shapes=[pltpu.VMEM((2, bm * N_SEG, LANES), F32), pltpu.VMEM((2, bm * N_SEG, LANES), F32),
                        pltpu.SemaphoreType.DMA((2,)), pltpu.SemaphoreType.DMA((2,))],
    )
    return pl.pallas_call(
        functools.partial(_expert_kernel, bm=bm),
        grid_spec=grid_spec,
        out_shape=jax.ShapeDtypeStruct((n_slots * N_SEG, LANES), F32),
        compiler_params=pltpu.CompilerParams(dimension_semantics=("arbitrary",),
                                             vmem_limit_bytes=VMEM_LIMIT,
                                             has_side_effects=True),
        name="experts",
    )(tile_expert, n_used, tile_valid, row_tok, row_tok, row_slot, h2, wg_t, wu_t, wd, bg, bu, bd)


def _combine_kernel(y0_ref, y1_ref, y2_ref, y3_ref, gate_ref, x1_ref, g_ref, o_ref, *, tm):
    acc = x1_ref[...]
    for k, y_ref in enumerate((y0_ref, y1_ref, y2_ref, y3_ref)):
        acc = acc + gate_ref[:, k:k + 1] * _load_token_tiles(y_ref, tm)
    o_ref[...] = _rms(acc, g_ref[...])


def _combine(y_slots, gates, x1, g, tm):
    t = x1.shape[0]
    nb = t // tm

    def slot_spec(k):
        return pl.BlockSpec((tm * N_SEG, LANES), lambda i: (k * nb + i, 0))

    return pl.pallas_call(
        functools.partial(_combine_kernel, tm=tm),
        grid=(nb,),
        in_specs=[slot_spec(0), slot_spec(1), slot_spec(2), slot_spec(3),
                  pl.BlockSpec((tm, TOP_K), lambda i: (i, 0)),
                  pl.BlockSpec((tm, D_MODEL), lambda i: (i, 0)),
                  pl.BlockSpec((1, D_MODEL), lambda i: (0, 0))],
        out_specs=pl.BlockSpec((tm, D_MODEL), lambda i: (i, 0)),
        out_shape=jax.ShapeDtypeStruct((t, D_MODEL), F32),
        compiler_params=pltpu.CompilerParams(dimension_semantics=("arbitrary",),
                                             vmem_limit_bytes=VMEM_LIMIT),
        name="combine",
    )(y_slots, y_slots, y_slots, y_slots, gates, x1, g)


def _lane_row(vec, lane0):
    n = vec.shape[0]
    return jnp.zeros((8, LANES), F32).at[0, lane0:lane0 + n].set(vec.astype(F32))


def _tile_size(n, pref):
    return pref if n % pref == 0 else n


def kernel(x, norm_mix, w_in, gdn_conv, gdn_a_log, gdn_dt_bias, gdn_norm, ml_conv, ml_b_i, ml_b_f,
           ml_norm, w_up_gdn, w_up_ml, w_out, norm_ffn, w_router, b_router, w_gate_up, b_gate_up,
           w_down, b_down, norm_final):
    assert norm_mix.shape[0] == 1, "single-layer stack"
    b, s, d = x.shape
    assert d == D_MODEL and s % CHUNK == 0
    t = b * s
    x2 = x.reshape(t, d)

    w = w_in[0]
    w_big = jnp.concatenate([_cols(w, n) for n in ("g_q", "g_k", "g_v", "g_z", "m_q", "m_k", "m_v",
                                                   "m_o", "gate_gdn", "gate_ml")], axis=1).astype(BF16)
    w_small = jnp.concatenate([_cols(w, n) for n in ("g_a", "g_b", "m_i", "m_f")], axis=1)
    w_small = jnp.pad(w_small, ((0, 0), (0, LANES - N_SMALL))).astype(BF16)

    tm = _tile_size(t, 512)
    big, small = _in_proj(x2, norm_mix[0][None, :], w_big, w_small, tm)
    big3 = big.reshape(b, s, N_BIG)
    small3 = small.reshape(b, s, LANES)

    cs = _tile_size(s, 512)
    gdn_prm = _lane_row(gdn_a_log[0], 0).at[1, 0:N_HEADS].set(gdn_dt_bias[0].astype(F32))
    oa = _gdn(big3, small3, gdn_conv[0].astype(F32), gdn_prm, gdn_norm[0][None, :].astype(F32), cs)
    ml_prm = _lane_row(ml_b_i[0], 2 * N_HEADS).at[0, 3 * N_HEADS:4 * N_HEADS].set(ml_b_f[0].astype(F32))
    ob = _mlstm(big3, small3, ml_conv[0].astype(F32), ml_prm, ml_norm[0][None, :].astype(F32), cs)

    w_r = jnp.pad(w_router[0], ((0, 0), (0, LANES - N_EXPERTS))).astype(BF16)
    b_r = jnp.full((1, LANES), -1e30, F32).at[0, :N_EXPERTS].set(b_router[0].astype(F32))
    x1, h2, idx, gates, rank, counts = _merge(
        oa.reshape(t, D_QK), ob.reshape(t, D_QK), big, x2, w_up_gdn[0].astype(BF16),
        w_up_ml[0].astype(BF16), w_out[0].astype(BF16), norm_ffn[0][None, :], w_r, b_r, tm)

    bm = EXPERT_TILE
    n_assign = t * TOP_K
    n_tiles = -(-n_assign // bm) + N_EXPERTS
    counts = counts[0, :N_EXPERTS]
    padded = (counts + bm - 1) // bm * bm
    pend = jnp.cumsum(padded)
    pstart = pend - padded
    dest = (pstart[idx] + rank).reshape(-1)
    n_used = (pend[-1] // bm).astype(I32)
    tile_ids = jnp.minimum(jnp.arange(n_tiles, dtype=I32), n_used - 1)
    tile_expert = jnp.minimum(jnp.sum((pend[None, :] <= (tile_ids * bm)[:, None]).astype(I32), axis=1),
                              N_EXPERTS - 1)
    tile_valid = jnp.clip(counts[tile_expert] - (tile_ids * bm - pstart[tile_expert]), 0, bm).astype(I32)
    assign = jnp.arange(n_assign, dtype=I32)
    row_slot = jnp.zeros((n_tiles * bm,), I32).at[dest].set((assign % TOP_K) * t + assign // TOP_K)
    row_tok = row_slot % t

    wgu_t = jnp.swapaxes(w_gate_up[0], 1, 2)
    bgu = b_gate_up[0]
    y_slots = _experts(tile_expert.astype(I32), n_used.reshape(1), tile_valid, row_tok, row_slot, h2,
                       wgu_t[:, 0::2, :].astype(BF16), wgu_t[:, 1::2, :].astype(BF16),
                       w_down[0].astype(BF16),
                       bgu[:, None, 0::2].astype(F32), bgu[:, None, 1::2].astype(F32),
                       b_down[0][:, None, :].astype(F32), bm)
    out = _combine(y_slots, gates, x1, norm_final[None, :], _tile_size(t, 256))
    return out.reshape(b, s, d)
```

```python
import functools

import jax
import jax.numpy as jnp
from jax import lax
from jax.experimental import pallas as pl
from jax.experimental.pallas import tpu as pltpu

F32 = jnp.float32
BF16 = jnp.bfloat16
I32 = jnp.int32

D_MODEL = 1024
N_HEADS = 4
D_HEAD = 128
D_QK = N_HEADS * D_HEAD
CONV_WIDTH = 4
CHUNK = 64
GATE_SOFTCAP = 15.0
N_EXPERTS = 32
TOP_K = 4
D_EXPERT = 1024
SWIGLU_LIMIT = 7.0
SWIGLU_ALPHA = 1.702
NORM_EPS = 1e-6

LANES = 128
SUBLANES = 8
N_SMALL = 16
N_BIG = 6 * D_MODEL
VMEM_LIMIT = 56 * 1024 * 1024
EXPERT_TILE = 512
DMA_UNROLL = 8
LOCAL_CHUNKS = 2

_OFF = {}
_o = 0
for _name, _w in (("g_q", D_QK), ("g_k", D_QK), ("g_v", D_QK), ("g_z", D_QK), ("g_a", N_HEADS),
                  ("g_b", N_HEADS), ("m_q", D_QK), ("m_k", D_QK), ("m_v", D_QK), ("m_o", D_QK),
                  ("m_i", N_HEADS), ("m_f", N_HEADS), ("gate_gdn", D_MODEL), ("gate_ml", D_MODEL)):
    _OFF[_name] = (_o, _w)
    _o += _w


def _cols(w, name):
    o, n = _OFF[name]
    return w[:, o:o + n]


_NN = (((1,), (0,)), ((), ()))
_NT = (((1,), (1,)), ((), ()))
_TN = (((0,), (0,)), ((), ()))


def _bdot(a, b, dims=_NN):
    return lax.dot_general(a.astype(BF16), b.astype(BF16), dims, preferred_element_type=F32)


def _split2(a):
    hi = a.astype(BF16)
    return hi, (a - hi.astype(F32)).astype(BF16)


def _split3(a):
    hi = a.astype(BF16)
    r = a - hi.astype(F32)
    mid = r.astype(BF16)
    return hi, mid, (r - mid.astype(F32)).astype(BF16)


def _dot3(a, b, dims=_NN):
    ah, al = _split2(a)
    bh, bl = _split2(b)
    d = functools.partial(lax.dot_general, dimension_numbers=dims, preferred_element_type=F32)
    return d(ah, bh) + (d(ah, bl) + d(al, bh))


def _select_dot(sel01, x, dims=_NN):
    s = sel01.astype(BF16)
    h, m, l = _split3(x)
    d = functools.partial(lax.dot_general, dimension_numbers=dims, preferred_element_type=F32)
    return d(s, h) + (d(s, m) + d(s, l))


def _sigmoid(x):
    return 1.0 / (1.0 + jnp.exp(-x))


def _silu(x):
    return x * _sigmoid(x)


def _softplus(x):
    return jnp.maximum(x, 0.0) + jnp.log(1.0 + jnp.exp(-jnp.abs(x)))


def _rms(x, g):
    return x * lax.rsqrt(jnp.mean(x * x, axis=-1, keepdims=True) + NORM_EPS) * g


def _in_proj_kernel(x_ref, g_ref, wbig_ref, wsmall_ref, big_ref, small_ref, *, n_chunk):
    h = _rms(x_ref[...], g_ref[...]).astype(BF16)
    for c in range(N_BIG // n_chunk):
        sl = slice(c * n_chunk, (c + 1) * n_chunk)
        big_ref[:, sl] = jnp.dot(h, wbig_ref[:, sl], preferred_element_type=F32)
    small_ref[...] = jnp.dot(h, wsmall_ref[...], preferred_element_type=F32)


def _in_proj(x2, g, w_big, w_small, tm):
    t = x2.shape[0]
    return pl.pallas_call(
        functools.partial(_in_proj_kernel, n_chunk=1024),
        grid=(t // tm,),
        in_specs=[pl.BlockSpec((tm, D_MODEL), lambda i: (i, 0)),
                  pl.BlockSpec((1, D_MODEL), lambda i: (0, 0)),
                  pl.BlockSpec((D_MODEL, N_BIG), lambda i: (0, 0)),
                  pl.BlockSpec((D_MODEL, LANES), lambda i: (0, 0))],
        out_specs=[pl.BlockSpec((tm, N_BIG), lambda i: (i, 0)),
                   pl.BlockSpec((tm, LANES), lambda i: (i, 0))],
        out_shape=[jax.ShapeDtypeStruct((t, N_BIG), F32),
                   jax.ShapeDtypeStruct((t, LANES), F32)],
        compiler_params=pltpu.CompilerParams(dimension_semantics=("arbitrary",),
                                             vmem_limit_bytes=VMEM_LIMIT),
        name="in_proj",
    )(x2, g, w_big, w_small)


def _chunk_masks():
    r = lax.broadcasted_iota(I32, (CHUNK, CHUNK), 0)
    c = lax.broadcasted_iota(I32, (CHUNK, CHUNK), 1)
    return r > c, r >= c


def _block_cumsum_matrix(cs):
    r = lax.broadcasted_iota(I32, (cs, cs), 0)
    c = lax.broadcasted_iota(I32, (cs, cs), 1)
    same = (r // CHUNK) == (c // CHUNK)
    return jnp.where(same & (r >= c), 1.0, 0.0).astype(F32)


def _lane_onehot(lane):
    return jnp.where(lax.broadcasted_iota(I32, (CHUNK, LANES), 1) == lane, 1.0, 0.0).astype(F32)


def _conv_silu(x_ref, w_ref, buf_ref, cs):
    @pl.when(pl.program_id(1) == 0)
    def _():
        buf_ref[0:8, :] = jnp.zeros((8, buf_ref.shape[1]), F32)

    buf_ref[8:8 + cs, :] = x_ref[0]
    acc = w_ref[CONV_WIDTH - 1:CONV_WIDTH, :] * buf_ref[8:8 + cs, :]
    for j in range(CONV_WIDTH - 1):
        s = 8 - (CONV_WIDTH - 1) + j
        acc = acc + w_ref[j:j + 1, :] * buf_ref[s:s + cs, :]
    buf_ref[0:8, :] = buf_ref[cs:cs + 8, :]
    return _silu(acc)


def _gdn_kernel(qkv_ref, z_ref, sm_ref, conv_ref, prm_ref, gn_ref, o_ref,
                buf_ref, q_s, k_s, v_s, w_s, attn_s, gam_s, beta_s, state_s, *, cs):
    n_chunks = cs // CHUNK

    @pl.when(pl.program_id(1) == 0)
    def _():
        state_s[...] = jnp.zeros(state_s.shape, F32)

    qkv = _conv_silu(qkv_ref, conv_ref, buf_ref, cs)
    for h in range(N_HEADS):
        sl = slice(h * D_HEAD, (h + 1) * D_HEAD)
        q = qkv[:, h * D_HEAD:(h + 1) * D_HEAD]
        k = qkv[:, D_QK + h * D_HEAD:D_QK + (h + 1) * D_HEAD]
        q_s[:, sl] = q * lax.rsqrt(jnp.sum(q * q, axis=-1, keepdims=True) + NORM_EPS) * (D_HEAD ** -0.5)
        k_s[:, sl] = k * lax.rsqrt(jnp.sum(k * k, axis=-1, keepdims=True) + NORM_EPS)
    v_s[...] = qkv[:, 2 * D_QK:]

    sm = sm_ref[0]
    logdec = -jnp.exp(prm_ref[0:1, :]) * _softplus(sm + prm_ref[1:2, :])
    gam_s[...] = _select_dot(_block_cumsum_matrix(cs), logdec)
    beta_s[...] = _sigmoid(sm)

    strict, incl = _chunk_masks()
    eye = jnp.where(lax.broadcasted_iota(I32, (CHUNK, CHUNK), 0)
                    == lax.broadcasted_iota(I32, (CHUNK, CHUNK), 1), 1.0, 0.0).astype(F32)
    gnorm = gn_ref[...]

    heads = range(N_HEADS)
    hsl = [slice(h * D_HEAD, (h + 1) * D_HEAD) for h in heads]
    asl = [slice(h * CHUNK, (h + 1) * CHUNK) for h in heads]
    onehots = [_lane_onehot(h) for h in heads]

    def local_body(it, carry):
        rows = [pl.ds(pl.multiple_of((it * LOCAL_CHUNKS + ci) * CHUNK, CHUNK), CHUNK)
                for ci in range(LOCAL_CHUNKS)]
        probs = [(ci, h) for ci in range(LOCAL_CHUNKS) for h in heads]
        gam_all = [gam_s[r, :] for r in rows]
        beta_all = [beta_s[r, :] for r in rows]
        q = [q_s[rows[ci], hsl[h]] for ci, h in probs]
        k = [k_s[rows[ci], hsl[h]] for ci, h in probs]
        v = [v_s[rows[ci], hsl[h]] for ci, h in probs]
        gc = [gam_all[ci][:, h:h + 1] for ci, h in probs]
        beta = [beta_all[ci][:, N_HEADS + h:N_HEADS + h + 1] for ci, h in probs]
        g_row = [_select_dot(onehots[h], gam_all[ci], _NT) for ci, h in probs]
        kb = [a * b for a, b in zip(k, beta)]
        kk = [_bdot(a, b, _NT) for a, b in zip(kb, k)]
        qk = [_bdot(a, b, _NT) for a, b in zip(q, k)]
        decay = [jnp.where(incl, jnp.exp(jnp.minimum(a - b, 0.0)), 0.0) for a, b in zip(gc, g_row)]
        lower = [jnp.where(strict, a * b, 0.0) for a, b in zip(kk, decay)]
        inv = [eye - a for a in lower]
        pw = [_dot3(a, a) for a in lower]
        for lvl in range(5):
            upd = [_dot3(a, b) for a, b in zip(inv, pw)]
            if lvl < 4:
                pw = [_dot3(a, a) for a in pw]
            inv = [a + b for a, b in zip(inv, upd)]
        e_gc = [jnp.exp(a) for a in gc]
        rhs = [jnp.concatenate([a * b, c * d], axis=1) for a, b, c, d in zip(v, beta, kb, e_gc)]
        sol = [_dot3(a, b) for a, b in zip(inv, rhs)]
        for p, (ci, h) in enumerate(probs):
            v_s[rows[ci], hsl[h]] = sol[p][:, :D_HEAD]
            w_s[rows[ci], hsl[h]] = sol[p][:, D_HEAD:]
            attn_s[rows[ci], asl[h]] = qk[p] * decay[p]
            q_s[rows[ci], hsl[h]] = q[p] * e_gc[p]
            k_s[rows[ci], hsl[h]] = k[p] * jnp.exp(gc[p][CHUNK - 1:CHUNK, :] - gc[p])
        return carry

    lax.fori_loop(0, n_chunks // LOCAL_CHUNKS, local_body, 0)

    def state_body(c, carry):
        rows = pl.ds(pl.multiple_of(c * CHUNK, CHUNK), CHUNK)
        g_last = gam_s[pl.ds(c * CHUNK + CHUNK - 1, 1), :]
        st = [state_s[h] for h in heads]
        ws = [_bdot(w_s[rows, hsl[h]], st[h]) for h in heads]
        qs = [_bdot(q_s[rows, hsl[h]], st[h]) for h in heads]
        u = [v_s[rows, hsl[h]] - ws[h] for h in heads]
        au = [_bdot(attn_s[rows, asl[h]], u[h]) for h in heads]
        ku = [_bdot(k_s[rows, hsl[h]], u[h], _TN) for h in heads]
        for h in heads:
            state_s[h] = st[h] * jnp.exp(g_last[:, h:h + 1]) + ku[h]
            o = qs[h] + au[h]
            o = o * lax.rsqrt(jnp.mean(o * o, axis=-1, keepdims=True) + NORM_EPS) * gnorm
            o_ref[0, rows, hsl[h]] = (o * _silu(z_ref[0, rows, hsl[h]])).astype(o_ref.dtype)
        return carry

    lax.fori_loop(0, n_chunks, state_body, 0)


def _gdn(big3, small3, conv_w, prm, gnorm, cs):
    b, s, _ = big3.shape
    return pl.pallas_call(
        functools.partial(_gdn_kernel, cs=cs),
        grid=(b, s // cs),
        in_specs=[pl.BlockSpec((1, cs, 3 * D_QK), lambda i, j: (i, j, 0)),
                  pl.BlockSpec((1, cs, D_QK), lambda i, j: (i, j, 3)),
                  pl.BlockSpec((1, cs, LANES), lambda i, j: (i, j, 0)),
                  pl.BlockSpec((CONV_WIDTH, 3 * D_QK), lambda i, j: (0, 0)),
                  pl.BlockSpec((8, LANES), lambda i, j: (0, 0)),
                  pl.BlockSpec((1, D_HEAD), lambda i, j: (0, 0))],
        out_specs=pl.BlockSpec((1, cs, D_QK), lambda i, j: (i, j, 0)),
        out_shape=jax.ShapeDtypeStruct((b, s, D_QK), BF16),
        scratch_shapes=[pltpu.VMEM((cs + 8, 3 * D_QK), F32),
                        pltpu.VMEM((cs, D_QK), F32), pltpu.VMEM((cs, D_QK), F32),
                        pltpu.VMEM((cs, D_QK), F32), pltpu.VMEM((cs, D_QK), F32),
                        pltpu.VMEM((cs, N_HEADS * CHUNK), F32),
                        pltpu.VMEM((cs, LANES), F32), pltpu.VMEM((cs, LANES), F32),
                        pltpu.VMEM((N_HEADS, D_HEAD, D_HEAD), F32)],
        compiler_params=pltpu.CompilerParams(dimension_semantics=("arbitrary", "arbitrary"),
                                             vmem_limit_bytes=VMEM_LIMIT),
        name="gdn",
    )(big3, big3, small3, conv_w, prm, gnorm)


def _mlstm_kernel(qk_ref, v_ref, og_ref, sm_ref, conv_ref, prm_ref, gn_ref, o_ref,
                  buf_ref, q_s, k_s, bcum_s, ipre_s, state_s, m_s, *, cs):
    n_chunks = cs // CHUNK

    @pl.when(pl.program_id(1) == 0)
    def _():
        state_s[...] = jnp.zeros(state_s.shape, F32)
        m_s[...] = jnp.zeros(m_s.shape, F32)

    qk = _conv_silu(qk_ref, conv_ref, buf_ref, cs)
    q_s[...] = qk[:, :D_QK]
    k_s[...] = qk[:, D_QK:] * (D_HEAD ** -0.5)

    pre = sm_ref[0] + prm_ref[0:1, :]
    capped = GATE_SOFTCAP * jnp.tanh(pre / GATE_SOFTCAP)
    logf = -_softplus(-capped)
    bcum_s[...] = _select_dot(_block_cumsum_matrix(cs), logf)
    ipre_s[...] = pltpu.roll(capped, N_HEADS, axis=1)

    _, incl = _chunk_masks()
    ones_aug = jnp.ones((CHUNK, D_HEAD), F32)

    heads = range(N_HEADS)
    hsl = [slice(h * D_HEAD, (h + 1) * D_HEAD) for h in heads]
    lanes = [3 * N_HEADS + h for h in heads]
    onehots = [_lane_onehot(lane) for lane in lanes]

    def chunk_body(c, carry):
        rows = pl.ds(pl.multiple_of(c * CHUNK, CHUNK), CHUNK)
        b_all = bcum_s[rows, :]
        comb_all = ipre_s[rows, :] - b_all
        q = [q_s[rows, hsl[h]] for h in heads]
        k = [k_s[rows, hsl[h]] for h in heads]
        v_aug = [jnp.concatenate([v_ref[0, rows, hsl[h]], ones_aug], axis=1) for h in heads]
        st = [state_s[h] for h in heads]
        m_st = [m_s[h] for h in heads]
        bc = [b_all[:, lane:lane + 1] for lane in lanes]
        comb_row = [_select_dot(onehots[h], comb_all, _NT) for h in heads]
        qk = [_bdot(q[h], k[h], _NT) for h in heads]
        inter = [_bdot(q[h], st[h]) for h in heads]
        b_last = [bc[h][CHUNK - 1:CHUNK, :] for h in heads]
        a_log = [b_last[h] + comb_all[:, lanes[h]:lanes[h] + 1] for h in heads]
        m_chunk = [jnp.max(a_log[h], axis=0, keepdims=True) for h in heads]
        wk = [k[h] * jnp.exp(a_log[h] - m_chunk[h]) for h in heads]
        d_state = [_bdot(wk[h], v_aug[h], _TN) for h in heads]
        d_log = [jnp.where(incl, bc[h] + comb_row[h], -jnp.inf) for h in heads]
        m_loc = [jnp.max(d_log[h], axis=-1, keepdims=True) for h in heads]
        p = [jnp.exp(d_log[h] - m_loc[h]) * qk[h] for h in heads]
        loc = [_bdot(p[h], v_aug[h]) for h in heads]
        for h in heads:
            m_inter = bc[h] + m_st[h]
            m_t = jnp.maximum(m_loc[h], m_inter)
            tot = jnp.exp(m_inter - m_t) * inter[h] + jnp.exp(m_loc[h] - m_t) * loc[h]
            hh = tot[:, :D_HEAD] / jnp.maximum(jnp.abs(tot[:, D_HEAD:]), jnp.exp(-m_t))
            m_new = jnp.maximum(b_last[h] + m_st[h], m_chunk[h])
            state_s[h] = (jnp.exp(b_last[h] + m_st[h] - m_new) * st[h]
                          + jnp.exp(m_chunk[h] - m_new) * d_state[h])
            m_s[h] = m_new
            hh = hh * lax.rsqrt(jnp.mean(hh * hh, axis=-1, keepdims=True) + NORM_EPS) * gn_ref[:, hsl[h]]
            o_ref[0, rows, hsl[h]] = (hh * _sigmoid(og_ref[0, rows, hsl[h]])).astype(o_ref.dtype)
        return carry

    lax.fori_loop(0, n_chunks, chunk_body, 0)


def _mlstm(big3, small3, conv_w, prm, gnorm, cs):
    b, s, _ = big3.shape
    return pl.pallas_call(
        functools.partial(_mlstm_kernel, cs=cs),
        grid=(b, s // cs),
        in_specs=[pl.BlockSpec((1, cs, 2 * D_QK), lambda i, j: (i, j, 2)),
                  pl.BlockSpec((1, cs, D_QK), lambda i, j: (i, j, 6)),
                  pl.BlockSpec((1, cs, D_QK), lambda i, j: (i, j, 7)),
                  pl.BlockSpec((1, cs, LANES), lambda i, j: (i, j, 0)),
                  pl.BlockSpec((CONV_WIDTH, 2 * D_QK), lambda i, j: (0, 0)),
                  pl.BlockSpec((8, LANES), lambda i, j: (0, 0)),
                  pl.BlockSpec((1, D_QK), lambda i, j: (0, 0))],
        out_specs=pl.BlockSpec((1, cs, D_QK), lambda i, j: (i, j, 0)),
        out_shape=jax.ShapeDtypeStruct((b, s, D_QK), BF16),
        scratch_shapes=[pltpu.VMEM((cs + 8, 2 * D_QK), F32),
                        pltpu.VMEM((cs, D_QK), F32), pltpu.VMEM((cs, D_QK), F32),
                        pltpu.VMEM((cs, LANES), F32), pltpu.VMEM((cs, LANES), F32),
                        pltpu.VMEM((N_HEADS, D_HEAD, 2 * D_HEAD), F32),
                        pltpu.VMEM((N_HEADS, 1, 1), F32)],
        compiler_params=pltpu.CompilerParams(dimension_semantics=("arbitrary", "arbitrary"),
                                             vmem_limit_bytes=VMEM_LIMIT),
        name="mlstm",
    )(big3, big3, big3, small3, conv_w, prm, gnorm)


N_SEG = D_MODEL // LANES
assert N_SEG == SUBLANES


def _store_token_tiles(ref, val, n):
    for s in range(N_SEG):
        ref[pl.ds(s, n, stride=N_SEG), :] = val[:, s * LANES:(s + 1) * LANES]


def _load_token_tiles(ref, n):
    return jnp.concatenate([ref[pl.ds(s, n, stride=N_SEG), :] for s in range(N_SEG)], axis=1)


def _merge_kernel(oa_ref, ob_ref, gates_ref, x_ref, wa_ref, wb_ref, wo_ref, g_ref, wr_ref, br_ref,
                  x1_ref, h2_ref, idx_ref, gate_ref, rank_ref, cnt_ref, carry_s, *, tm):
    @pl.when(pl.program_id(0) == 0)
    def _():
        carry_s[...] = jnp.zeros(carry_s.shape, F32)

    y_a = jnp.dot(oa_ref[...], wa_ref[...], preferred_element_type=F32)
    y_b = jnp.dot(ob_ref[...], wb_ref[...], preferred_element_type=F32)
    y = _sigmoid(gates_ref[:, :D_MODEL]) * y_a + _sigmoid(gates_ref[:, D_MODEL:]) * y_b
    x1 = x_ref[...] + jnp.dot(y.astype(BF16), wo_ref[...], preferred_element_type=F32)
    x1_ref[...] = x1
    h2 = _rms(x1, g_ref[...])
    _store_token_tiles(h2_ref, h2, tm)
    logits = jnp.dot(h2.astype(BF16), wr_ref[...], preferred_element_type=F32) + br_ref[...]

    lane = lax.broadcasted_iota(I32, (tm, LANES), 1).astype(F32)
    vals, sels = [], []
    idx_t = jnp.zeros((tm, LANES), F32)
    work = logits
    for k in range(TOP_K):
        m = jnp.max(work, axis=-1, keepdims=True)
        i = jnp.min(jnp.where(work == m, lane, float(LANES)), axis=-1, keepdims=True)
        sel = lane == i
        work = jnp.where(sel, -jnp.inf, work)
        idx_t = jnp.where(lane == float(k), i, idx_t)
        vals.append(m)
        sels.append(sel)
    es = [jnp.exp(v - vals[0]) for v in vals]
    denom = es[0] + es[1] + es[2] + es[3]
    gate_t = jnp.zeros((tm, LANES), F32)
    for k in range(TOP_K):
        gate_t = jnp.where(lane == float(k), es[k] / denom, gate_t)
    idx_ref[...] = idx_t[:, :TOP_K].astype(I32)
    gate_ref[...] = gate_t[:, :TOP_K]

    onehot = jnp.zeros((tm, LANES), F32)
    for sel in sels:
        onehot = onehot + jnp.where(sel, 1.0, 0.0)
    r = lax.broadcasted_iota(I32, (tm, tm), 0)
    c = lax.broadcasted_iota(I32, (tm, tm), 1)
    before = jnp.where(r > c, 1.0, 0.0).astype(BF16)
    cum = jnp.dot(before, onehot.astype(BF16), preferred_element_type=F32) + carry_s[...]
    rank_t = jnp.zeros((tm, LANES), F32)
    for k, sel in enumerate(sels):
        rk = jnp.sum(jnp.where(sel, cum, 0.0), axis=-1, keepdims=True)
        rank_t = jnp.where(lane == float(k), rk, rank_t)
    rank_ref[...] = rank_t[:, :TOP_K].astype(I32)
    total = carry_s[...] + jnp.sum(onehot, axis=0, keepdims=True)
    carry_s[...] = total
    cnt_ref[...] = total.astype(I32)


def _merge(oa, ob, big, x2, wa, wb, wo, g, wr, br, tm):
    t = x2.shape[0]
    const = lambda i: (0, 0)
    return pl.pallas_call(
        functools.partial(_merge_kernel, tm=tm),
        grid=(t // tm,),
        in_specs=[pl.BlockSpec((tm, D_QK), lambda i: (i, 0)),
                  pl.BlockSpec((tm, D_QK), lambda i: (i, 0)),
                  pl.BlockSpec((tm, 2 * D_MODEL), lambda i: (i, 2)),
                  pl.BlockSpec((tm, D_MODEL), lambda i: (i, 0)),
                  pl.BlockSpec((D_QK, D_MODEL), const),
                  pl.BlockSpec((D_QK, D_MODEL), const),
                  pl.BlockSpec((D_MODEL, D_MODEL), const),
                  pl.BlockSpec((1, D_MODEL), const),
                  pl.BlockSpec((D_MODEL, LANES), const),
                  pl.BlockSpec((1, LANES), const)],
        out_specs=[pl.BlockSpec((tm, D_MODEL), lambda i: (i, 0)),
                   pl.BlockSpec((tm * N_SEG, LANES), lambda i: (i, 0)),
                   pl.BlockSpec((tm, TOP_K), lambda i: (i, 0)),
                   pl.BlockSpec((tm, TOP_K), lambda i: (i, 0)),
                   pl.BlockSpec((tm, TOP_K), lambda i: (i, 0)),
                   pl.BlockSpec((1, LANES), const)],
        out_shape=[jax.ShapeDtypeStruct((t, D_MODEL), F32),
                   jax.ShapeDtypeStruct((t * N_SEG, LANES), F32),
                   jax.ShapeDtypeStruct((t, TOP_K), I32),
                   jax.ShapeDtypeStruct((t, TOP_K), F32),
                   jax.ShapeDtypeStruct((t, TOP_K), I32),
                   jax.ShapeDtypeStruct((1, LANES), I32)],
        scratch_shapes=[pltpu.VMEM((1, LANES), F32)],
        compiler_params=pltpu.CompilerParams(dimension_semantics=("arbitrary",),
                                             vmem_limit_bytes=VMEM_LIMIT),
        name="merge_router",
    )(oa, ob, big, x2, wa, wb, wo, g, wr, br)


def _expert_kernel(te_ref, nu_ref, nv_ref, tok_ref, tok_next_ref, slot_ref, h2_hbm,
                   wg_ref, wu_ref, wd_ref, bg_ref, bu_ref, bd_ref, y_hbm,
                   xbuf, ybuf, in_sem, out_sem, *, bm):
    del te_ref
    i = pl.program_id(0)
    n_used = nu_ref[0]
    slot = lax.rem(i, 2)

    def token_tile(ref, r):
        return ref.at[pl.ds(pl.multiple_of(r * N_SEG, N_SEG), N_SEG)]

    def gather(ids_ref, s):
        def body(r, carry):
            pltpu.make_async_copy(token_tile(h2_hbm, ids_ref[r]), token_tile(xbuf.at[s], r),
                                  in_sem.at[s]).start()
            return carry
        lax.fori_loop(0, bm, body, 0, unroll=DMA_UNROLL)

    def wait_scatter(s, n_rows):
        n = pl.multiple_of(n_rows * N_SEG, N_SEG)
        pltpu.make_async_copy(ybuf.at[s, pl.ds(0, n)], y_hbm.at[pl.ds(0, n)], out_sem.at[s]).wait()

    @pl.when(i == 0)
    def _():
        gather(tok_ref, 0)

    @pl.when(i < n_used)
    def _():
        pltpu.make_async_copy(h2_hbm.at[pl.ds(0, bm * N_SEG)], xbuf.at[slot], in_sem.at[slot]).wait()

        @pl.when(i + 1 < n_used)
        def _():
            gather(tok_next_ref, 1 - slot)

        @pl.when(i >= 2)
        def _():
            wait_scatter(slot, nv_ref[i - 2])

        x = _load_token_tiles(xbuf.at[slot], bm).astype(BF16)
        g = lax.dot_general(x, wg_ref[0], _NT, preferred_element_type=F32) + bg_ref[0]
        u = lax.dot_general(x, wu_ref[0], _NT, preferred_element_type=F32) + bu_ref[0]
        gate = jnp.minimum(g, SWIGLU_LIMIT)
        up = jnp.clip(u, -SWIGLU_LIMIT, SWIGLU_LIMIT)
        act = gate * _sigmoid(SWIGLU_ALPHA * gate) * (up + 1.0)
        y = jnp.dot(act.astype(BF16), wd_ref[0], preferred_element_type=F32) + bd_ref[0]
        _store_token_tiles(ybuf.at[slot], y, bm)

        n_valid = nv_ref[i]

        def scatter(r, carry):
            pltpu.make_async_copy(token_tile(ybuf.at[slot], r), token_tile(y_hbm, slot_ref[r]),
                                  out_sem.at[slot]).start()
            return carry

        def scatter_group(j, carry):
            for q in range(DMA_UNROLL):
                scatter(j * DMA_UNROLL + q, carry)
            return carry
        n_groups = n_valid // DMA_UNROLL
        lax.fori_loop(0, n_groups, scatter_group, 0)
        lax.fori_loop(n_groups * DMA_UNROLL, n_valid, scatter, 0)

        @pl.when(i == n_used - 1)
        def _():
            wait_scatter(slot, n_valid)

            @pl.when(i >= 1)
            def _():
                wait_scatter(1 - slot, nv_ref[i - 1])


def _experts(tile_expert, n_used, tile_valid, row_tok, row_slot, h2, wg_t, wu_t, wd, bg, bu, bd, bm):
    n_rows = row_tok.shape[0]
    n_tiles = n_rows // bm
    n_slots = h2.shape[0] // N_SEG * TOP_K
    cur =lambda i, te, nu, nv: (jnp.minimum(i, nu[0] - 1),)
    nxt = lambda i, te, nu, nv: (jnp.minimum(i + 1, nu[0] - 1),)
    w_map = lambda i, te, nu, nv: (te[i], 0, 0)
    grid_spec = pltpu.PrefetchScalarGridSpec(
        num_scalar_prefetch=3,
        grid=(n_tiles,),
        in_specs=[pl.BlockSpec((bm,), cur, memory_space=pltpu.SMEM),
                  pl.BlockSpec((bm,), nxt, memory_space=pltpu.SMEM),
                  pl.BlockSpec((bm,), cur, memory_space=pltpu.SMEM),
                  pl.BlockSpec(memory_space=pl.ANY),
                  pl.BlockSpec((1, D_EXPERT, D_MODEL), w_map),
                  pl.BlockSpec((1, D_EXPERT, D_MODEL), w_map),
                  pl.BlockSpec((1, D_EXPERT, D_MODEL), w_map),
                  pl.BlockSpec((1, 1, D_EXPERT), w_map),
                  pl.BlockSpec((1, 1, D_EXPERT), w_map),
                  pl.BlockSpec((1, 1, D_MODEL), w_map)],
        out_specs=pl.BlockSpec(memory_space=pl.ANY),
        scratch_shapes=[pltpu.VMEM((2, bm * N_SEG, LANES), F32), pltpu.VMEM((2, bm * N_SEG, LANES), F32),
                        pltpu.SemaphoreType.DMA((2,)), pltpu.SemaphoreType.DMA((2,))],
    )
    return pl.pallas_call(
        functools.partial(_expert_kernel, bm=bm),
        grid_spec=grid_spec,
        out_shape=jax.ShapeDtypeStruct((n_slots * N_SEG, LANES), F32),
        compiler_params=pltpu.CompilerParams(dimension_semantics=("arbitrary",),
                                             vmem_limit_bytes=VMEM_LIMIT,
                                             has_side_effects=True),
        name="experts",
    )(tile_expert, n_used, tile_valid, row_tok, row_tok, row_slot, h2, wg_t, wu_t, wd, bg, bu, bd)


def _combine_kernel(y0_ref, y1_ref, y2_ref, y3_ref, gate_ref, x1_ref, g_ref, o_ref, *, tm):
    acc = x1_ref[...]
    for k, y_ref in enumerate((y0_ref, y1_ref, y2_ref, y3_ref)):
        acc = acc + gate_ref[:, k:k + 1] * _load_token_tiles(y_ref, tm)
    o_ref[...] = _rms(acc, g_ref[...])


def _combine(y_slots, gates, x1, g, tm):
    t = x1.shape[0]
    nb = t // tm

    def slot_spec(k):
        return pl.BlockSpec((tm * N_SEG, LANES), lambda i: (k * nb + i, 0))

    return pl.pallas_call(
        functools.partial(_combine_kernel, tm=tm),
        grid=(nb,),
        in_specs=[slot_spec(0), slot_spec(1), slot_spec(2), slot_spec(3),
                  pl.BlockSpec((tm, TOP_K), lambda i: (i, 0)),
                  pl.BlockSpec((tm, D_MODEL), lambda i: (i, 0)),
                  pl.BlockSpec((1, D_MODEL), lambda i: (0, 0))],
        out_specs=pl.BlockSpec((tm, D_MODEL), lambda i: (i, 0)),
        out_shape=jax.ShapeDtypeStruct((t, D_MODEL), F32),
        compiler_params=pltpu.CompilerParams(dimension_semantics=("arbitrary",),
                                             vmem_limit_bytes=VMEM_LIMIT),
        name="combine",
    )(y_slots, y_slots, y_slots, y_slots, gates, x1, g)


def _lane_row(vec, lane0):
    n = vec.shape[0]
    return jnp.zeros((8, LANES), F32).at[0, lane0:lane0 + n].set(vec.astype(F32))


def _tile_size(n, pref):
    return pref if n % pref == 0 else n


def kernel(x, norm_mix, w_in, gdn_conv, gdn_a_log, gdn_dt_bias, gdn_norm, ml_conv, ml_b_i, ml_b_f,
           ml_norm, w_up_gdn, w_up_ml, w_out, norm_ffn, w_router, b_router, w_gate_up, b_gate_up,
           w_down, b_down, norm_final):
    assert norm_mix.shape[0] == 1, "single-layer stack"
    b, s, d = x.shape
    assert d == D_MODEL and s % CHUNK == 0
    t = b * s
    x2 = x.reshape(t, d)

    w = w_in[0]
    w_big = jnp.concatenate([_cols(w, n) for n in ("g_q", "g_k", "g_v", "g_z", "m_q", "m_k", "m_v",
                                                   "m_o", "gate_gdn", "gate_ml")], axis=1).astype(BF16)
    w_small = jnp.concatenate([_cols(w, n) for n in ("g_a", "g_b", "m_i", "m_f")], axis=1)
    w_small = jnp.pad(w_small, ((0, 0), (0, LANES - N_SMALL))).astype(BF16)

    tm = _tile_size(t, 512)
    big, small = _in_proj(x2, norm_mix[0][None, :], w_big, w_small, tm)
    big3 = big.reshape(b, s, N_BIG)
    small3 = small.reshape(b, s, LANES)

    cs = _tile_size(s, 512)
    gdn_prm = _lane_row(gdn_a_log[0], 0).at[1, 0:N_HEADS].set(gdn_dt_bias[0].astype(F32))
    oa = _gdn(big3, small3, gdn_conv[0].astype(F32), gdn_prm, gdn_norm[0][None, :].astype(F32), cs)
    ml_prm = _lane_row(ml_b_i[0], 2 * N_HEADS).at[0, 3 * N_HEADS:4 * N_HEADS].set(ml_b_f[0].astype(F32))
    ob = _mlstm(big3, small3, ml_conv[0].astype(F32), ml_prm, ml_norm[0][None, :].astype(F32), cs)

    w_r = jnp.pad(w_router[0], ((0, 0), (0, LANES - N_EXPERTS))).astype(BF16)
    b_r = jnp.full((1, LANES), -1e30, F32).at[0, :N_EXPERTS].set(b_router[0].astype(F32))
    x1, h2, idx, gates, rank, counts = _merge(
        oa.reshape(t, D_QK), ob.reshape(t, D_QK), big, x2, w_up_gdn[0].astype(BF16),
        w_up_ml[0].astype(BF16), w_out[0].astype(BF16), norm_ffn[0][None, :], w_r, b_r, tm)

    bm = EXPERT_TILE
    n_assign = t * TOP_K
    n_tiles = -(-n_assign // bm) + N_EXPERTS
    counts = counts[0, :N_EXPERTS]
    padded = (counts + bm - 1) // bm * bm
    pend = jnp.cumsum(padded)
    pstart = pend - padded
    dest = (pstart[idx] + rank).reshape(-1)
    n_used = (pend[-1] // bm).astype(I32)
    tile_ids = jnp.minimum(jnp.arange(n_tiles, dtype=I32), n_used - 1)
    tile_expert = jnp.minimum(jnp.sum((pend[None, :] <= (tile_ids * bm)[:, None]).astype(I32), axis=1),
                              N_EXPERTS - 1)
    tile_valid = jnp.clip(counts[tile_expert] - (tile_ids * bm - pstart[tile_expert]), 0, bm).astype(I32)
    assign = jnp.arange(n_assign, dtype=I32)
    row_slot = jnp.zeros((n_tiles * bm,), I32).at[dest].set((assign % TOP_K) * t + assign // TOP_K)
    row_tok = row_slot % t

    wgu_t = jnp.swapaxes(w_gate_up[0], 1, 2)
    bgu = b_gate_up[0]
    y_slots = _experts(tile_expert.astype(I32), n_used.reshape(1), tile_valid, row_tok, row_slot, h2,
                       wgu_t[:, 0::2, :].astype(BF16), wgu_t[:, 1::2, :].astype(BF16),
                       w_down[0].astype(BF16),
                       bgu[:, None, 0::2].astype(F32), bgu[:, None, 1::2].astype(F32),
                       b_down[0][:, None, :].astype(F32), bm)
    out = _combine(y_slots, gates, x1, norm_final[None, :], _tile_size(t, 256))
    return out.reshape(b, s, d)
```

```python
import functools

import jax
import jax.numpy as jnp
from jax import lax
from jax.experimental import pallas as pl
from jax.experimental.pallas import tpu as pltpu

F32 = jnp.float32
BF16 = jnp.bfloat16
I32 = jnp.int32

D_MODEL = 1024
N_HEADS = 4
D_HEAD = 128
D_QK = N_HEADS * D_HEAD
CONV_WIDTH = 4
CHUNK = 64
GATE_SOFTCAP = 15.0
N_EXPERTS = 32
TOP_K = 4
D_EXPERT = 1024
SWIGLU_LIMIT = 7.0
SWIGLU_ALPHA = 1.702
NORM_EPS = 1e-6

LANES = 128
SUBLANES = 8
N_SMALL = 16
N_BIG = 6 * D_MODEL
VMEM_LIMIT = 56 * 1024 * 1024
EXPERT_TILE = 512
DMA_UNROLL = 8
PERM_BLOCK = 256
LOCAL_CHUNKS = 2

_OFF = {}
_o = 0
for _name, _w in (("g_q", D_QK), ("g_k", D_QK), ("g_v", D_QK), ("g_z", D_QK), ("g_a", N_HEADS),
                  ("g_b", N_HEADS), ("m_q", D_QK), ("m_k", D_QK), ("m_v", D_QK), ("m_o", D_QK),
                  ("m_i", N_HEADS), ("m_f", N_HEADS), ("gate_gdn", D_MODEL), ("gate_ml", D_MODEL)):
    _OFF[_name] = (_o, _w)
    _o += _w


def _cols(w, name):
    o, n = _OFF[name]
    return w[:, o:o + n]


_NN = (((1,), (0,)), ((), ()))
_NT = (((1,), (1,)), ((), ()))
_TN = (((0,), (0,)), ((), ()))


def _bdot(a, b, dims=_NN):
    return lax.dot_general(a.astype(BF16), b.astype(BF16), dims, preferred_element_type=F32)


def _split2(a):
    hi = a.astype(BF16)
    return hi, (a - hi.astype(F32)).astype(BF16)


def _split3(a):
    hi = a.astype(BF16)
    r = a - hi.astype(F32)
    mid = r.astype(BF16)
    return hi, mid, (r - mid.astype(F32)).astype(BF16)


def _dot3(a, b, dims=_NN):
    ah, al = _split2(a)
    bh, bl = _split2(b)
    d = functools.partial(lax.dot_general, dimension_numbers=dims, preferred_element_type=F32)
    return d(ah, bh) + (d(ah, bl) + d(al, bh))


def _select_dot(sel01, x, dims=_NN):
    s = sel01.astype(BF16)
    h, m, l = _split3(x)
    d = functools.partial(lax.dot_general, dimension_numbers=dims, preferred_element_type=F32)
    return d(s, h) + (d(s, m) + d(s, l))


def _sigmoid(x):
    return 1.0 / (1.0 + jnp.exp(-x))


def _silu(x):
    return x * _sigmoid(x)


def _softplus(x):
    return jnp.maximum(x, 0.0) + jnp.log(1.0 + jnp.exp(-jnp.abs(x)))


def _rms(x, g):
    return x * lax.rsqrt(jnp.mean(x * x, axis=-1, keepdims=True) + NORM_EPS) * g


def _in_proj_kernel(x_ref, g_ref, wbig_ref, wsmall_ref, big_ref, small_ref, *, n_chunk):
    h = _rms(x_ref[...], g_ref[...]).astype(BF16)
    for c in range(N_BIG // n_chunk):
        sl = slice(c * n_chunk, (c + 1) * n_chunk)
        big_ref[:, sl] = jnp.dot(h, wbig_ref[:, sl], preferred_element_type=F32)
    small_ref[...] = jnp.dot(h, wsmall_ref[...], preferred_element_type=F32)


def _in_proj(x2, g, w_big, w_small, tm):
    t = x2.shape[0]
    return pl.pallas_call(
        functools.partial(_in_proj_kernel, n_chunk=1024),
        grid=(t // tm,),
        in_specs=[pl.BlockSpec((tm, D_MODEL), lambda i: (i, 0)),
                  pl.BlockSpec((1, D_MODEL), lambda i: (0, 0)),
                  pl.BlockSpec((D_MODEL, N_BIG), lambda i: (0, 0)),
                  pl.BlockSpec((D_MODEL, LANES), lambda i: (0, 0))],
        out_specs=[pl.BlockSpec((tm, N_BIG), lambda i: (i, 0)),
                   pl.BlockSpec((tm, LANES), lambda i: (i, 0))],
        out_shape=[jax.ShapeDtypeStruct((t, N_BIG), F32),
                   jax.ShapeDtypeStruct((t, LANES), F32)],
        compiler_params=pltpu.CompilerParams(dimension_semantics=("arbitrary",),
                                             vmem_limit_bytes=VMEM_LIMIT),
        name="in_proj",
    )(x2, g, w_big, w_small)


def _chunk_masks():
    r = lax.broadcasted_iota(I32, (CHUNK, CHUNK), 0)
    c = lax.broadcasted_iota(I32, (CHUNK, CHUNK), 1)
    return r > c, r >= c


def _block_cumsum_matrix(cs):
    r = lax.broadcasted_iota(I32, (cs, cs), 0)
    c = lax.broadcasted_iota(I32, (cs, cs), 1)
    same = (r // CHUNK) == (c // CHUNK)
    return jnp.where(same & (r >= c), 1.0, 0.0).astype(F32)


def _lane_onehot(lane):
    return jnp.where(lax.broadcasted_iota(I32, (CHUNK, LANES), 1) == lane, 1.0, 0.0).astype(F32)


def _conv_silu(x_ref, w_ref, buf_ref, cs):
    @pl.when(pl.program_id(1) == 0)
    def _():
        buf_ref[0:8, :] = jnp.zeros((8, buf_ref.shape[1]), F32)

    buf_ref[8:8 + cs, :] = x_ref[0]
    acc = w_ref[CONV_WIDTH - 1:CONV_WIDTH, :] * buf_ref[8:8 + cs, :]
    for j in range(CONV_WIDTH - 1):
        s = 8 - (CONV_WIDTH - 1) + j
        acc = acc + w_ref[j:j + 1, :] * buf_ref[s:s + cs, :]
    buf_ref[0:8, :] = buf_ref[cs:cs + 8, :]
    return _silu(acc)


def _gdn_kernel(qkv_ref, z_ref, sm_ref, conv_ref, prm_ref, gn_ref, o_ref,
                buf_ref, q_s, k_s, v_s, w_s, attn_s, gam_s, beta_s, state_s, *, cs):
    n_chunks = cs // CHUNK

    @pl.when(pl.program_id(1) == 0)
    def _():
        state_s[...] = jnp.zeros(state_s.shape, F32)

    qkv = _conv_silu(qkv_ref, conv_ref, buf_ref, cs)
    for h in range(N_HEADS):
        sl = slice(h * D_HEAD, (h + 1) * D_HEAD)
        q = qkv[:, h * D_HEAD:(h + 1) * D_HEAD]
        k = qkv[:, D_QK + h * D_HEAD:D_QK + (h + 1) * D_HEAD]
        q_s[:, sl] = q * lax.rsqrt(jnp.sum(q * q, axis=-1, keepdims=True) + NORM_EPS) * (D_HEAD ** -0.5)
        k_s[:, sl] = k * lax.rsqrt(jnp.sum(k * k, axis=-1, keepdims=True) + NORM_EPS)
    v_s[...] = qkv[:, 2 * D_QK:]

    sm = sm_ref[0]
    logdec = -jnp.exp(prm_ref[0:1, :]) * _softplus(sm + prm_ref[1:2, :])
    gam_s[...] = _select_dot(_block_cumsum_matrix(cs), logdec)
    beta_s[...] = _sigmoid(sm)

    strict, incl = _chunk_masks()
    eye = jnp.where(lax.broadcasted_iota(I32, (CHUNK, CHUNK), 0)
                    == lax.broadcasted_iota(I32, (CHUNK, CHUNK), 1), 1.0, 0.0).astype(F32)
    gnorm = gn_ref[...]

    heads = range(N_HEADS)
    hsl = [slice(h * D_HEAD, (h + 1) * D_HEAD) for h in heads]
    asl = [slice(h * CHUNK, (h + 1) * CHUNK) for h in heads]
    onehots = [_lane_onehot(h) for h in heads]

    def local_body(it, carry):
        rows = [pl.ds(pl.multiple_of((it * LOCAL_CHUNKS + ci) * CHUNK, CHUNK), CHUNK)
                for ci in range(LOCAL_CHUNKS)]
        probs = [(ci, h) for ci in range(LOCAL_CHUNKS) for h in heads]
        gam_all = [gam_s[r, :] for r in rows]
        beta_all = [beta_s[r, :] for r in rows]
        q = [q_s[rows[ci], hsl[h]] for ci, h in probs]
        k = [k_s[rows[ci], hsl[h]] for ci, h in probs]
        v = [v_s[rows[ci], hsl[h]] for ci, h in probs]
        gc = [gam_all[ci][:, h:h + 1] for ci, h in probs]
        beta = [beta_all[ci][:, N_HEADS + h:N_HEADS + h + 1] for ci, h in probs]
        g_row = [_select_dot(onehots[h], gam_all[ci], _NT) for ci, h in probs]
        kb = [a * b for a, b in zip(k, beta)]
        kk = [_bdot(a, b, _NT) for a, b in zip(kb, k)]
        qk = [_bdot(a, b, _NT) for a, b in zip(q, k)]
        decay = [jnp.where(incl, jnp.exp(jnp.minimum(a - b, 0.0)), 0.0) for a, b in zip(gc, g_row)]
        lower = [jnp.where(strict, a * b, 0.0) for a, b in zip(kk, decay)]
        inv = [eye - a for a in lower]
        pw = [_dot3(a, a) for a in lower]
        for lvl in range(5):
            upd = [_dot3(a, b) for a, b in zip(inv, pw)]
            if lvl < 4:
                pw = [_dot3(a, a) for a in pw]
            inv = [a + b for a, b in zip(inv, upd)]
        e_gc = [jnp.exp(a) for a in gc]
        rhs = [jnp.concatenate([a * b, c * d], axis=1) for a, b, c, d in zip(v, beta, kb, e_gc)]
        sol = [_dot3(a, b) for a, b in zip(inv, rhs)]
        for p, (ci, h) in enumerate(probs):
            v_s[rows[ci], hsl[h]] = sol[p][:, :D_HEAD]
            w_s[rows[ci], hsl[h]] = sol[p][:, D_HEAD:]
            attn_s[rows[ci], asl[h]] = qk[p] * decay[p]
            q_s[rows[ci], hsl[h]] = q[p] * e_gc[p]
            k_s[rows[ci], hsl[h]] = k[p] * jnp.exp(gc[p][CHUNK - 1:CHUNK, :] - gc[p])
        return carry

    lax.fori_loop(0, n_chunks // LOCAL_CHUNKS, local_body, 0)

    def state_body(c, carry):
        rows = pl.ds(pl.multiple_of(c * CHUNK, CHUNK), CHUNK)
        g_last = gam_s[pl.ds(c * CHUNK + CHUNK - 1, 1), :]
        st = [state_s[h] for h in heads]
        ws = [_bdot(w_s[rows, hsl[h]], st[h]) for h in heads]
        qs = [_bdot(q_s[rows, hsl[h]], st[h]) for h in heads]
        u = [v_s[rows, hsl[h]] - ws[h] for h in heads]
        au = [_bdot(attn_s[rows, asl[h]], u[h]) for h in heads]
        ku = [_bdot(k_s[rows, hsl[h]], u[h], _TN) for h in heads]
        for h in heads:
            state_s[h] = st[h] * jnp.exp(g_last[:, h:h + 1]) + ku[h]
            o = qs[h] + au[h]
            o = o * lax.rsqrt(jnp.mean(o * o, axis=-1, keepdims=True) + NORM_EPS) * gnorm
            o_ref[0, rows, hsl[h]] = (o * _silu(z_ref[0, rows, hsl[h]])).astype(o_ref.dtype)
        return carry

    lax.fori_loop(0, n_chunks, state_body, 0)


def _gdn(big3, small3, conv_w, prm, gnorm, cs):
    b, s, _ = big3.shape
    return pl.pallas_call(
        functools.partial(_gdn_kernel, cs=cs),
        grid=(b, s // cs),
        in_specs=[pl.BlockSpec((1, cs, 3 * D_QK), lambda i, j: (i, j, 0)),
                  pl.BlockSpec((1, cs, D_QK), lambda i, j: (i, j, 3)),
                  pl.BlockSpec((1, cs, LANES), lambda i, j: (i, j, 0)),
                  pl.BlockSpec((CONV_WIDTH, 3 * D_QK), lambda i, j: (0, 0)),
                  pl.BlockSpec((8, LANES), lambda i, j: (0, 0)),
                  pl.BlockSpec((1, D_HEAD), lambda i, j: (0, 0))],
        out_specs=pl.BlockSpec((1, cs, D_QK), lambda i, j: (i, j, 0)),
        out_shape=jax.ShapeDtypeStruct((b, s, D_QK), BF16),
        scratch_shapes=[pltpu.VMEM((cs + 8, 3 * D_QK), F32),
                        pltpu.VMEM((cs, D_QK), F32), pltpu.VMEM((cs, D_QK), F32),
                        pltpu.VMEM((cs, D_QK), F32), pltpu.VMEM((cs, D_QK), F32),
                        pltpu.VMEM((cs, N_HEADS * CHUNK), F32),
                        pltpu.VMEM((cs, LANES), F32), pltpu.VMEM((cs, LANES), F32),
                        pltpu.VMEM((N_HEADS, D_HEAD, D_HEAD), F32)],
        compiler_params=pltpu.CompilerParams(dimension_semantics=("arbitrary", "arbitrary"),
                                             vmem_limit_bytes=VMEM_LIMIT),
        name="gdn",
    )(big3, big3, small3, conv_w, prm, gnorm)


def _mlstm_kernel(qk_ref, v_ref, og_ref, sm_ref, conv_ref, prm_ref, gn_ref, o_ref,
                  buf_ref, q_s, k_s, bcum_s, ipre_s, state_s, m_s, *, cs):
    n_chunks = cs // CHUNK

    @pl.when(pl.program_id(1) == 0)
    def _():
        state_s[...] = jnp.zeros(state_s.shape, F32)
        m_s[...] = jnp.zeros(m_s.shape, F32)

    qk = _conv_silu(qk_ref, conv_ref, buf_ref, cs)
    q_s[...] = qk[:, :D_QK]
    k_s[...] = qk[:, D_QK:] * (D_HEAD ** -0.5)

    pre = sm_ref[0] + prm_ref[0:1, :]
    capped = GATE_SOFTCAP * jnp.tanh(pre / GATE_SOFTCAP)
    logf = -_softplus(-capped)
    bcum_s[...] = _select_dot(_block_cumsum_matrix(cs), logf)
    ipre_s[...] = pltpu.roll(capped, N_HEADS, axis=1)

    _, incl = _chunk_masks()
    ones_aug = jnp.ones((CHUNK, D_HEAD), F32)

    heads = range(N_HEADS)
    hsl = [slice(h * D_HEAD, (h + 1) * D_HEAD) for h in heads]
    lanes = [3 * N_HEADS + h for h in heads]
    onehots = [_lane_onehot(lane) for lane in lanes]

    def chunk_body(c, carry):
        rows = pl.ds(pl.multiple_of(c * CHUNK, CHUNK), CHUNK)
        b_all = bcum_s[rows, :]
        comb_all = ipre_s[rows, :] - b_all
        q = [q_s[rows, hsl[h]] for h in heads]
        k = [k_s[rows, hsl[h]] for h in heads]
        v_aug = [jnp.concatenate([v_ref[0, rows, hsl[h]], ones_aug], axis=1) for h in heads]
        st = [state_s[h] for h in heads]
        m_st = [m_s[h] for h in heads]
        bc = [b_all[:, lane:lane + 1] for lane in lanes]
        comb_row = [_select_dot(onehots[h], comb_all, _NT) for h in heads]
        qk = [_bdot(q[h], k[h], _NT) for h in heads]
        inter = [_bdot(q[h], st[h]) for h in heads]
        b_last = [bc[h][CHUNK - 1:CHUNK, :] for h in heads]
        a_log = [b_last[h] + comb_all[:, lanes[h]:lanes[h] + 1] for h in heads]
        m_chunk = [jnp.max(a_log[h], axis=0, keepdims=True) for h in heads]
        wk = [k[h] * jnp.exp(a_log[h] - m_chunk[h]) for h in heads]
        d_state = [_bdot(wk[h], v_aug[h], _TN) for h in heads]
        d_log = [jnp.where(incl, bc[h] + comb_row[h], -jnp.inf) for h in heads]
        m_loc = [jnp.max(d_log[h], axis=-1, keepdims=True) for h in heads]
        p = [jnp.exp(d_log[h] - m_loc[h]) * qk[h] for h in heads]
        loc = [_bdot(p[h], v_aug[h]) for h in heads]
        for h in heads:
            m_inter = bc[h] + m_st[h]
            m_t = jnp.maximum(m_loc[h], m_inter)
            tot = jnp.exp(m_inter - m_t) * inter[h] + jnp.exp(m_loc[h] - m_t) * loc[h]
            hh = tot[:, :D_HEAD] / jnp.maximum(jnp.abs(tot[:, D_HEAD:]), jnp.exp(-m_t))
            m_new = jnp.maximum(b_last[h] + m_st[h], m_chunk[h])
            state_s[h] = (jnp.exp(b_last[h] + m_st[h] - m_new) * st[h]
                          + jnp.exp(m_chunk[h] - m_new) * d_state[h])
            m_s[h] = m_new
            hh = hh * lax.rsqrt(jnp.mean(hh * hh, axis=-1, keepdims=True) + NORM_EPS) * gn_ref[:, hsl[h]]
            o_ref[0, rows, hsl[h]] = (hh * _sigmoid(og_ref[0, rows, hsl[h]])).astype(o_ref.dtype)
        return carry

    lax.fori_loop(0, n_chunks, chunk_body, 0)


def _mlstm(big3, small3, conv_w, prm, gnorm, cs):
    b, s, _ = big3.shape
    return pl.pallas_call(
        functools.partial(_mlstm_kernel, cs=cs),
        grid=(b, s // cs),
        in_specs=[pl.BlockSpec((1, cs, 2 * D_QK), lambda i, j: (i, j, 2)),
                  pl.BlockSpec((1, cs, D_QK), lambda i, j: (i, j, 6)),
                  pl.BlockSpec((1, cs, D_QK), lambda i, j: (i, j, 7)),
                  pl.BlockSpec((1, cs, LANES), lambda i, j: (i, j, 0)),
                  pl.BlockSpec((CONV_WIDTH, 2 * D_QK), lambda i, j: (0, 0)),
                  pl.BlockSpec((8, LANES), lambda i, j: (0, 0)),
                  pl.BlockSpec((1, D_QK), lambda i, j: (0, 0))],
        out_specs=pl.BlockSpec((1, cs, D_QK), lambda i, j: (i, j, 0)),
        out_shape=jax.ShapeDtypeStruct((b, s, D_QK), BF16),
        scratch_shapes=[pltpu.VMEM((cs + 8, 2 * D_QK), F32),
                        pltpu.VMEM((cs, D_QK), F32), pltpu.VMEM((cs, D_QK), F32),
                        pltpu.VMEM((cs, LANES), F32), pltpu.VMEM((cs, LANES), F32),
                        pltpu.VMEM((N_HEADS, D_HEAD, 2 * D_HEAD), F32),
                        pltpu.VMEM((N_HEADS, 1, 1), F32)],
        compiler_params=pltpu.CompilerParams(dimension_semantics=("arbitrary", "arbitrary"),
                                             vmem_limit_bytes=VMEM_LIMIT),
        name="mlstm",
    )(big3, big3, big3, small3, conv_w, prm, gnorm)


N_SEG = D_MODEL // LANES
assert N_SEG == SUBLANES


def _store_token_tiles(ref, val, n):
    for s in range(N_SEG):
        ref[pl.ds(s, n, stride=N_SEG), :] = val[:, s * LANES:(s + 1) * LANES]


def _load_token_tiles(ref, n):
    return jnp.concatenate([ref[pl.ds(s, n, stride=N_SEG), :] for s in range(N_SEG)], axis=1)


def _merge_kernel(oa_ref, ob_ref, gates_ref, x_ref, wa_ref, wb_ref, wo_ref, g_ref, wr_ref, br_ref,
                  x1_ref, h2_ref, idx_ref, gate_ref, rank_ref, cnt_ref, carry_s, *, tm):
    @pl.when(pl.program_id(0) == 0)
    def _():
        carry_s[...] = jnp.zeros(carry_s.shape, F32)

    y_a = jnp.dot(oa_ref[...], wa_ref[...], preferred_element_type=F32)
    y_b = jnp.dot(ob_ref[...], wb_ref[...], preferred_element_type=F32)
    y = _sigmoid(gates_ref[:, :D_MODEL]) * y_a + _sigmoid(gates_ref[:, D_MODEL:]) * y_b
    x1 = x_ref[...] + jnp.dot(y.astype(BF16), wo_ref[...], preferred_element_type=F32)
    x1_ref[...] = x1
    h2 = _rms(x1, g_ref[...])
    _store_token_tiles(h2_ref, h2, tm)
    logits = jnp.dot(h2.astype(BF16), wr_ref[...], preferred_element_type=F32) + br_ref[...]

    lane = lax.broadcasted_iota(I32, (tm, LANES), 1).astype(F32)
    vals, sels = [], []
    idx_t = jnp.zeros((tm, LANES), F32)
    work = logits
    for k in range(TOP_K):
        m = jnp.max(work, axis=-1, keepdims=True)
        i = jnp.min(jnp.where(work == m, lane, float(LANES)), axis=-1, keepdims=True)
        sel = lane == i
        work = jnp.where(sel, -jnp.inf, work)
        idx_t = jnp.where(lane == float(k), i, idx_t)
        vals.append(m)
        sels.append(sel)
    es = [jnp.exp(v - vals[0]) for v in vals]
    denom = es[0] + es[1] + es[2] + es[3]
    gate_t = jnp.zeros((tm, LANES), F32)
    for k in range(TOP_K):
        gate_t = jnp.where(lane == float(k), es[k] / denom, gate_t)
    idx_ref[...] = idx_t[:, :TOP_K].astype(I32)
    gate_ref[...] = gate_t[:, :TOP_K]

    onehot = jnp.zeros((tm, LANES), F32)
    for sel in sels:
        onehot = onehot + jnp.where(sel, 1.0, 0.0)
    r = lax.broadcasted_iota(I32, (tm, tm), 0)
    c = lax.broadcasted_iota(I32, (tm, tm), 1)
    before = jnp.where(r > c, 1.0, 0.0).astype(BF16)
    cum = jnp.dot(before, onehot.astype(BF16), preferred_element_type=F32) + carry_s[...]
    rank_t = jnp.zeros((tm, LANES), F32)
    for k, sel in enumerate(sels):
        rk = jnp.sum(jnp.where(sel, cum, 0.0), axis=-1, keepdims=True)
        rank_t = jnp.where(lane == float(k), rk, rank_t)
    rank_ref[...] = rank_t[:, :TOP_K].astype(I32)
    total = carry_s[...] + jnp.sum(onehot, axis=0, keepdims=True)
    carry_s[...] = total
    cnt_ref[...] = total.astype(I32)


def _merge(oa, ob, big, x2, wa, wb, wo, g, wr, br, tm):
    t = x2.shape[0]
    const = lambda i: (0, 0)
    return pl.pallas_call(
        functools.partial(_merge_kernel, tm=tm),
        grid=(t // tm,),
        in_specs=[pl.BlockSpec((tm, D_QK), lambda i: (i, 0)),
                  pl.BlockSpec((tm, D_QK), lambda i: (i, 0)),
                  pl.BlockSpec((tm, 2 * D_MODEL), lambda i: (i, 2)),
                  pl.BlockSpec((tm, D_MODEL), lambda i: (i, 0)),
                  pl.BlockSpec((D_QK, D_MODEL), const),
                  pl.BlockSpec((D_QK, D_MODEL), const),
                  pl.BlockSpec((D_MODEL, D_MODEL), const),
                  pl.BlockSpec((1, D_MODEL), const),
                  pl.BlockSpec((D_MODEL, LANES), const),
                  pl.BlockSpec((1, LANES), const)],
        out_specs=[pl.BlockSpec((tm, D_MODEL), lambda i: (i, 0)),
                   pl.BlockSpec((tm * N_SEG, LANES), lambda i: (i, 0)),
                   pl.BlockSpec((tm, TOP_K), lambda i: (i, 0)),
                   pl.BlockSpec((tm, TOP_K), lambda i: (i, 0)),
                   pl.BlockSpec((tm, TOP_K), lambda i: (i, 0)),
                   pl.BlockSpec((1, LANES), const)],
        out_shape=[jax.ShapeDtypeStruct((t, D_MODEL), F32),
                   jax.ShapeDtypeStruct((t * N_SEG, LANES), F32),
                   jax.ShapeDtypeStruct((t, TOP_K), I32),
                   jax.ShapeDtypeStruct((t, TOP_K), F32),
                   jax.ShapeDtypeStruct((t, TOP_K), I32),
                   jax.ShapeDtypeStruct((1, LANES), I32)],
        scratch_shapes=[pltpu.VMEM((1, LANES), F32)],
        compiler_params=pltpu.CompilerParams(dimension_semantics=("arbitrary",),
                                             vmem_limit_bytes=VMEM_LIMIT),
        name="merge_router",
    )(oa, ob, big, x2, wa, wb, wo, g, wr, br)


def _expert_kernel(te_ref, nu_ref, nv_ref, tok_ref, tok_next_ref, slot_ref, h2_hbm,
                   wgu_ref, wd_ref, perm_ref, bg_ref, bu_ref, bd_ref, y_hbm,
                   xbuf, ybuf, wg_s, wu_s, wd_s, in_sem, out_sem, *, bm):
    i = pl.program_id(0)
    n_used = nu_ref[0]
    slot = lax.rem(i, 2)
    new_expert = (i == 0) | (te_ref[i] != te_ref[jnp.maximum(i - 1, 0)])

    def token_tile(ref, r):
        return ref.at[pl.ds(pl.multiple_of(r * N_SEG, N_SEG), N_SEG)]

    def gather_row(ids_ref, s, r):
        pltpu.make_async_copy(token_tile(h2_hbm, ids_ref[r]), token_tile(xbuf.at[s], r),
                              in_sem.at[s]).start()

    def wait_gather(s):
        pltpu.make_async_copy(h2_hbm.at[pl.ds(0, bm * N_SEG)], xbuf.at[s], in_sem.at[s]).wait()

    def wait_scatter(s, n_rows):
        n = pl.multiple_of(n_rows * N_SEG, N_SEG)
        pltpu.make_async_copy(ybuf.at[s, pl.ds(0, n)], y_hbm.at[pl.ds(0, n)], out_sem.at[s]).wait()

    @pl.when(i == 0)
    def _():
        def body(r, carry):
            gather_row(tok_ref, 0, r)
            return carry
        lax.fori_loop(0, bm, body, 0, unroll=DMA_UNROLL)

    @pl.when(i < n_used)
    def _():
        wait_gather(slot)

        @pl.when(i >= 2)
        def _():
            wait_scatter(slot, nv_ref[i - 2])

        @pl.when(new_expert)
        def _():
            half = PERM_BLOCK // 2
            for j in range(2 * D_EXPERT // PERM_BLOCK):
                blk = wgu_ref[0, :, j * PERM_BLOCK:(j + 1) * PERM_BLOCK].astype(BF16)
                split = jnp.dot(blk, perm_ref[...], preferred_element_type=F32)
                wg_s[:, j * half:(j + 1) * half] = split[:, :half].astype(BF16)
                wu_s[:, j * half:(j + 1) * half] = split[:, half:].astype(BF16)
            wd_s[...] = wd_ref[0].astype(BF16)

        for r in range(bm):
            gather_row(tok_next_ref, 1 - slot, r)

        x = _load_token_tiles(xbuf.at[slot], bm).astype(BF16)
        g = jnp.dot(x, wg_s[...], preferred_element_type=F32) + bg_ref[0]
        u = jnp.dot(x, wu_s[...], preferred_element_type=F32) + bu_ref[0]
        gate = jnp.minimum(g, SWIGLU_LIMIT)
        up = jnp.clip(u, -SWIGLU_LIMIT, SWIGLU_LIMIT)
        act = gate * _sigmoid(SWIGLU_ALPHA * gate) * (up + 1.0)
        y = jnp.dot(act.astype(BF16), wd_s[...], preferred_element_type=F32) + bd_ref[0]
        _store_token_tiles(ybuf.at[slot], y, bm)

        n_valid = nv_ref[i]

        def scatter(r, carry):
            pltpu.make_async_copy(token_tile(ybuf.at[slot], r), token_tile(y_hbm, slot_ref[r]),
                                  out_sem.at[slot]).start()
            return carry

        def scatter_group(j, carry):
            for q in range(DMA_UNROLL):
                scatter(j * DMA_UNROLL + q, carry)
            return carry
        n_groups = n_valid // DMA_UNROLL
        lax.fori_loop(0, n_groups, scatter_group, 0)
        lax.fori_loop(n_groups * DMA_UNROLL, n_valid, scatter, 0)

        @pl.when(i == n_used - 1)
        def _():
            wait_gather(1 - slot)
            wait_scatter(slot, n_valid)

            @pl.when(i >= 1)
            def _():
                wait_scatter(1 - slot, nv_ref[i - 1])


def _split_permutation():
    half = PERM_BLOCK // 2
    src = jnp.arange(PERM_BLOCK)[:, None]
    dst = jnp.arange(PERM_BLOCK)[None, :]
    return (src == jnp.where(dst < half, 2 * dst, 2 * (dst - half) + 1)).astype(BF16)


def _experts(tile_expert, n_used, tile_valid, row_tok, row_slot, h2, wgu, wd, bg, bu, bd, bm):
    n_rows = row_tok.shape[0]
    n_tiles = n_rows // bm
    n_slots = h2.shape[0] // N_SEG * TOP_K
    cur = lambda i, te, nu, nv: (jnp.minimum(i, nu[0] - 1),)
    nxt = lambda i, te, nu, nv: (jnp.minimum(i + 1, nu[0] - 1),)
    w_map = lambda i, te, nu, nv: (te[i], 0, 0)
    grid_spec = pltpu.PrefetchScalarGridSpec(
        num_scalar_prefetch=3,
        grid=(n_tiles,),
        in_specs=[pl.BlockSpec((bm,), cur, memory_space=pltpu.SMEM),
                  pl.BlockSpec((bm,), nxt, memory_space=pltpu.SMEM),
                  pl.BlockSpec((bm,), cur, memory_space=pltpu.SMEM),
                  pl.BlockSpec(memory_space=pl.ANY),
                  pl.BlockSpec((1, D_MODEL, 2 * D_EXPERT), w_map),
                  pl.BlockSpec((1, D_EXPERT, D_MODEL), w_map),
                  pl.BlockSpec((PERM_BLOCK, PERM_BLOCK), lambda i, te, nu, nv: (0, 0)),
                  pl.BlockSpec((1, 1, D_EXPERT), w_map),
                  pl.BlockSpec((1, 1, D_EXPERT), w_map),
                  pl.BlockSpec((1, 1, D_MODEL), w_map)],
        out_specs=pl.BlockSpec(memory_space=pl.ANY),
        scratch_shapes=[pltpu.VMEM((2, bm * N_SEG, LANES), F32), pltpu.VMEM((2, bm * N_SEG, LANES), F32),
                        pltpu.VMEM((D_MODEL, D_EXPERT), BF16), pltpu.VMEM((D_MODEL, D_EXPERT), BF16),
                        pltpu.VMEM((D_EXPERT, D_MODEL), BF16),
                        pltpu.SemaphoreType.DMA((2,)), pltpu.SemaphoreType.DMA((2,))],
    )
    return pl.pallas_call(
        functools.partial(_expert_kernel, bm=bm),
        grid_spec=grid_spec,
        out_shape=jax.ShapeDtypeStruct((n_slots * N_SEG, LANES), F32),
        compiler_params=pltpu.CompilerParams(dimension_semantics=("arbitrary",),
                                             vmem_limit_bytes=VMEM_LIMIT,
                                             has_side_effects=True),
        name="experts",
    )(tile_expert, n_used, tile_valid, row_tok, row_tok, row_slot, h2, wgu, wd, _split_permutation(),
      bg, bu, bd)


def _combine_kernel(y0_ref, y1_ref, y2_ref, y3_ref, gate_ref, x1_ref, g_ref, o_ref, *, tm):
    acc = x1_ref[...]
    for k, y_ref in enumerate((y0_ref, y1_ref, y2_ref, y3_ref)):
        acc = acc + gate_ref[:, k:k + 1] * _load_token_tiles(y_ref, tm)
    o_ref[...] = _rms(acc, g_ref[...])


def _combine(y_slots, gates, x1, g, tm):
    t = x1.shape[0]
    nb = t // tm

    def slot_spec(k):
        return pl.BlockSpec((tm * N_SEG, LANES), lambda i: (k * nb + i, 0))

    return pl.pallas_call(
        functools.partial(_combine_kernel, tm=tm),
        grid=(nb,),
        in_specs=[slot_spec(0), slot_spec(1), slot_spec(2), slot_spec(3),
                  pl.BlockSpec((tm, TOP_K), lambda i: (i, 0)),
                  pl.BlockSpec((tm, D_MODEL), lambda i: (i, 0)),
                  pl.BlockSpec((1, D_MODEL), lambda i: (0, 0))],
        out_specs=pl.BlockSpec((tm, D_MODEL), lambda i: (i, 0)),
        out_shape=jax.ShapeDtypeStruct((t, D_MODEL), F32),
        compiler_params=pltpu.CompilerParams(dimension_semantics=("arbitrary",),
                                             vmem_limit_bytes=VMEM_LIMIT),
        name="combine",
    )(y_slots, y_slots, y_slots, y_slots, gates, x1, g)


def _lane_row(vec, lane0):
    n = vec.shape[0]
    return jnp.zeros((8, LANES), F32).at[0, lane0:lane0 + n].set(vec.astype(F32))


def _tile_size(n, pref):
    return pref if n % pref == 0 else n


def kernel(x, norm_mix, w_in, gdn_conv, gdn_a_log, gdn_dt_bias, gdn_norm, ml_conv, ml_b_i, ml_b_f,
           ml_norm, w_up_gdn, w_up_ml, w_out, norm_ffn, w_router, b_router, w_gate_up, b_gate_up,
           w_down, b_down, norm_final):
    assert norm_mix.shape[0] == 1, "single-layer stack"
    b, s, d = x.shape
    assert d == D_MODEL and s % CHUNK == 0
    t = b * s
    x2 = x.reshape(t, d)

    w = w_in[0]
    w_big = jnp.concatenate([_cols(w, n) for n in ("g_q", "g_k", "g_v", "g_z", "m_q", "m_k", "m_v",
                                                   "m_o", "gate_gdn", "gate_ml")], axis=1).astype(BF16)
    w_small = jnp.concatenate([_cols(w, n) for n in ("g_a", "g_b", "m_i", "m_f")], axis=1)
    w_small = jnp.pad(w_small, ((0, 0), (0, LANES - N_SMALL))).astype(BF16)

    tm = _tile_size(t, 512)
    big, small = _in_proj(x2, norm_mix[0][None, :], w_big, w_small, tm)
    big3 = big.reshape(b, s, N_BIG)
    small3 = small.reshape(b, s, LANES)

    cs = _tile_size(s, 512)
    gdn_prm = _lane_row(gdn_a_log[0], 0).at[1, 0:N_HEADS].set(gdn_dt_bias[0].astype(F32))
    oa = _gdn(big3, small3, gdn_conv[0].astype(F32), gdn_prm, gdn_norm[0][None, :].astype(F32), cs)
    ml_prm = _lane_row(ml_b_i[0], 2 * N_HEADS).at[0, 3 * N_HEADS:4 * N_HEADS].set(ml_b_f[0].astype(F32))
    ob = _mlstm(big3, small3, ml_conv[0].astype(F32), ml_prm, ml_norm[0][None, :].astype(F32), cs)

    w_r = jnp.pad(w_router[0], ((0, 0), (0, LANES - N_EXPERTS))).astype(BF16)
    b_r = jnp.full((1, LANES), -1e30, F32).at[0, :N_EXPERTS].set(b_router[0].astype(F32))
    x1, h2, idx, gates, rank, counts = _merge(
        oa.reshape(t, D_QK), ob.reshape(t, D_QK), big, x2, w_up_gdn[0].astype(BF16),
        w_up_ml[0].astype(BF16), w_out[0].astype(BF16), norm_ffn[0][None, :], w_r, b_r, tm)

    bm = EXPERT_TILE
    n_assign = t * TOP_K
    n_tiles = -(-n_assign // bm) + N_EXPERTS
    counts = counts[0, :N_EXPERTS]
    padded = (counts + bm - 1) // bm * bm
    pend = jnp.cumsum(padded)
    pstart = pend - padded
    dest = (pstart[idx] + rank).reshape(-1)
    n_used = (pend[-1] // bm).astype(I32)
    tile_ids = jnp.minimum(jnp.arange(n_tiles, dtype=I32), n_used - 1)
    tile_expert = jnp.minimum(jnp.sum((pend[None, :] <= (tile_ids * bm)[:, None]).astype(I32), axis=1),
                              N_EXPERTS - 1)
    tile_valid = jnp.clip(counts[tile_expert] - (tile_ids * bm - pstart[tile_expert]), 0, bm).astype(I32)
    assign = jnp.arange(n_assign, dtype=I32)
    row_slot = jnp.zeros((n_tiles * bm,), I32).at[dest].set((assign % TOP_K) * t + assign // TOP_K,
                                                            unique_indices=True)
    row_tok = row_slot % t

    bgu = b_gate_up[0]
    y_slots = _experts(tile_expert.astype(I32), n_used.reshape(1), tile_valid, row_tok, row_slot, h2,
                       w_gate_up[0], w_down[0],
                       bgu[:, None, 0::2].astype(F32), bgu[:, None, 1::2].astype(F32),
                       b_down[0][:, None, :].astype(F32), bm)
    out = _combine(y_slots, gates, x1, norm_final[None, :], _tile_size(t, 256))
    return out.reshape(b, s, d)
```

```python
import functools

import jax
import jax.numpy as jnp
from jax import lax
from jax.experimental import pallas as pl
from jax.experimental.pallas import tpu as pltpu

F32 = jnp.float32
BF16 = jnp.bfloat16
I32 = jnp.int32

D_MODEL = 1024
N_HEADS = 4
D_HEAD = 128
D_QK = N_HEADS * D_HEAD
CONV_WIDTH = 4
CHUNK = 64
GATE_SOFTCAP = 15.0
N_EXPERTS = 32
TOP_K = 4
D_EXPERT = 1024
SWIGLU_LIMIT = 7.0
SWIGLU_ALPHA = 1.702
NORM_EPS = 1e-6

LANES = 128
SUBLANES = 8
N_SMALL = 16
N_BIG = 6 * D_MODEL
VMEM_LIMIT = 56 * 1024 * 1024
EXPERT_TILE = 512
DMA_UNROLL = 8
DMA_GROUPS = 4
PERM_BLOCK = 256
LOCAL_CHUNKS = 2

_OFF = {}
_o = 0
for _name, _w in (("g_q", D_QK), ("g_k", D_QK), ("g_v", D_QK), ("g_z", D_QK), ("g_a", N_HEADS),
                  ("g_b", N_HEADS), ("m_q", D_QK), ("m_k", D_QK), ("m_v", D_QK), ("m_o", D_QK),
                  ("m_i", N_HEADS), ("m_f", N_HEADS), ("gate_gdn", D_MODEL), ("gate_ml", D_MODEL)):
    _OFF[_name] = (_o, _w)
    _o += _w


def _cols(w, name):
    o, n = _OFF[name]
    return w[:, o:o + n]


_NN = (((1,), (0,)), ((), ()))
_NT = (((1,), (1,)), ((), ()))
_TN = (((0,), (0,)), ((), ()))


def _bdot(a, b, dims=_NN):
    return lax.dot_general(a.astype(BF16), b.astype(BF16), dims, preferred_element_type=F32)


def _split2(a):
    hi = a.astype(BF16)
    return hi, (a - hi.astype(F32)).astype(BF16)


def _split3(a):
    hi = a.astype(BF16)
    r = a - hi.astype(F32)
    mid = r.astype(BF16)
    return hi, mid, (r - mid.astype(F32)).astype(BF16)


def _dot3(a, b, dims=_NN):
    ah, al = _split2(a)
    bh, bl = _split2(b)
    d = functools.partial(lax.dot_general, dimension_numbers=dims, preferred_element_type=F32)
    return d(ah, bh) + (d(ah, bl) + d(al, bh))


def _select_dot(sel01, x, dims=_NN):
    s = sel01.astype(BF16)
    h, m, l = _split3(x)
    d = functools.partial(lax.dot_general, dimension_numbers=dims, preferred_element_type=F32)
    return d(s, h) + (d(s, m) + d(s, l))


def _sigmoid(x):
    return 1.0 / (1.0 + jnp.exp(-x))


def _silu(x):
    return x * _sigmoid(x)


def _softplus(x):
    return jnp.maximum(x, 0.0) + jnp.log(1.0 + jnp.exp(-jnp.abs(x)))


def _rms(x, g):
    return x * lax.rsqrt(jnp.mean(x * x, axis=-1, keepdims=True) + NORM_EPS) * g


def _in_proj_kernel(x_ref, g_ref, wbig_ref, wsmall_ref, big_ref, small_ref, *, n_chunk):
    h = _rms(x_ref[...], g_ref[...]).astype(BF16)
    for c in range(N_BIG // n_chunk):
        sl = slice(c * n_chunk, (c + 1) * n_chunk)
        big_ref[:, sl] = jnp.dot(h, wbig_ref[:, sl], preferred_element_type=F32)
    small_ref[...] = jnp.dot(h, wsmall_ref[...], preferred_element_type=F32)


def _in_proj(x2, g, w_big, w_small, tm):
    t = x2.shape[0]
    return pl.pallas_call(
        functools.partial(_in_proj_kernel, n_chunk=1024),
        grid=(t // tm,),
        in_specs=[pl.BlockSpec((tm, D_MODEL), lambda i: (i, 0)),
                  pl.BlockSpec((1, D_MODEL), lambda i: (0, 0)),
                  pl.BlockSpec((D_MODEL, N_BIG), lambda i: (0, 0)),
                  pl.BlockSpec((D_MODEL, LANES), lambda i: (0, 0))],
        out_specs=[pl.BlockSpec((tm, N_BIG), lambda i: (i, 0)),
                   pl.BlockSpec((tm, LANES), lambda i: (i, 0))],
        out_shape=[jax.ShapeDtypeStruct((t, N_BIG), F32),
                   jax.ShapeDtypeStruct((t, LANES), F32)],
        compiler_params=pltpu.CompilerParams(dimension_semantics=("arbitrary",),
                                             vmem_limit_bytes=VMEM_LIMIT),
        name="in_proj",
    )(x2, g, w_big, w_small)


def _chunk_masks():
    r = lax.broadcasted_iota(I32, (CHUNK, CHUNK), 0)
    c = lax.broadcasted_iota(I32, (CHUNK, CHUNK), 1)
    return r > c, r >= c


def _block_cumsum_matrix(cs):
    r = lax.broadcasted_iota(I32, (cs, cs), 0)
    c = lax.broadcasted_iota(I32, (cs, cs), 1)
    same = (r // CHUNK) == (c // CHUNK)
    return jnp.where(same & (r >= c), 1.0, 0.0).astype(F32)


def _lane_onehot(lane):
    return jnp.where(lax.broadcasted_iota(I32, (CHUNK, LANES), 1) == lane, 1.0, 0.0).astype(F32)


def _conv_silu(x_ref, w_ref, buf_ref, cs):
    @pl.when(pl.program_id(1) == 0)
    def _():
        buf_ref[0:8, :] = jnp.zeros((8, buf_ref.shape[1]), F32)

    buf_ref[8:8 + cs, :] = x_ref[0]
    acc = w_ref[CONV_WIDTH - 1:CONV_WIDTH, :] * buf_ref[8:8 + cs, :]
    for j in range(CONV_WIDTH - 1):
        s = 8 - (CONV_WIDTH - 1) + j
        acc = acc + w_ref[j:j + 1, :] * buf_ref[s:s + cs, :]
    buf_ref[0:8, :] = buf_ref[cs:cs + 8, :]
    return _silu(acc)


def _gdn_kernel(qkv_ref, z_ref, sm_ref, conv_ref, prm_ref, gn_ref, o_ref,
                buf_ref, q_s, k_s, v_s, w_s, attn_s, gam_s, beta_s, state_s, *, cs):
    n_chunks = cs // CHUNK

    @pl.when(pl.program_id(1) == 0)
    def _():
        state_s[...] = jnp.zeros(state_s.shape, F32)

    qkv = _conv_silu(qkv_ref, conv_ref, buf_ref, cs)
    for h in range(N_HEADS):
        sl = slice(h * D_HEAD, (h + 1) * D_HEAD)
        q = qkv[:, h * D_HEAD:(h + 1) * D_HEAD]
        k = qkv[:, D_QK + h * D_HEAD:D_QK + (h + 1) * D_HEAD]
        q_s[:, sl] = q * lax.rsqrt(jnp.sum(q * q, axis=-1, keepdims=True) + NORM_EPS) * (D_HEAD ** -0.5)
        k_s[:, sl] = k * lax.rsqrt(jnp.sum(k * k, axis=-1, keepdims=True) + NORM_EPS)
    v_s[...] = qkv[:, 2 * D_QK:]

    sm = sm_ref[0]
    logdec = -jnp.exp(prm_ref[0:1, :]) * _softplus(sm + prm_ref[1:2, :])
    gam_s[...] = _select_dot(_block_cumsum_matrix(cs), logdec)
    beta_s[...] = _sigmoid(sm)

    strict, incl = _chunk_masks()
    eye = jnp.where(lax.broadcasted_iota(I32, (CHUNK, CHUNK), 0)
                    == lax.broadcasted_iota(I32, (CHUNK, CHUNK), 1), 1.0, 0.0).astype(F32)
    gnorm = gn_ref[...]

    heads = range(N_HEADS)
    hsl = [slice(h * D_HEAD, (h + 1) * D_HEAD) for h in heads]
    asl = [slice(h * CHUNK, (h + 1) * CHUNK) for h in heads]
    onehots = [_lane_onehot(h) for h in heads]

    def local_body(it, carry):
        rows = [pl.ds(pl.multiple_of((it * LOCAL_CHUNKS + ci) * CHUNK, CHUNK), CHUNK)
                for ci in range(LOCAL_CHUNKS)]
        probs = [(ci, h) for ci in range(LOCAL_CHUNKS) for h in heads]
        gam_all = [gam_s[r, :] for r in rows]
        beta_all = [beta_s[r, :] for r in rows]
        q = [q_s[rows[ci], hsl[h]] for ci, h in probs]
        k = [k_s[rows[ci], hsl[h]] for ci, h in probs]
        v = [v_s[rows[ci], hsl[h]] for ci, h in probs]
        gc = [gam_all[ci][:, h:h + 1] for ci, h in probs]
        beta = [beta_all[ci][:, N_HEADS + h:N_HEADS + h + 1] for ci, h in probs]
        g_row = [_select_dot(onehots[h], gam_all[ci], _NT) for ci, h in probs]
        kb = [a * b for a, b in zip(k, beta)]
        kk = [_bdot(a, b, _NT) for a, b in zip(kb, k)]
        qk = [_bdot(a, b, _NT) for a, b in zip(q, k)]
        decay = [jnp.where(incl, jnp.exp(jnp.minimum(a - b, 0.0)), 0.0) for a, b in zip(gc, g_row)]
        lower = [jnp.where(strict, a * b, 0.0) for a, b in zip(kk, decay)]
        inv = [eye - a for a in lower]
        pw = [_dot3(a, a) for a in lower]
        for lvl in range(5):
            upd = [_dot3(a, b) for a, b in zip(inv, pw)]
            if lvl < 4:
                pw = [_dot3(a, a) for a in pw]
            inv = [a + b for a, b in zip(inv, upd)]
        e_gc = [jnp.exp(a) for a in gc]
        rhs = [jnp.concatenate([a * b, c * d], axis=1) for a, b, c, d in zip(v, beta, kb, e_gc)]
        sol = [_dot3(a, b) for a, b in zip(inv, rhs)]
        for p, (ci, h) in enumerate(probs):
            v_s[rows[ci], hsl[h]] = sol[p][:, :D_HEAD]
            w_s[rows[ci], hsl[h]] = sol[p][:, D_HEAD:]
            attn_s[rows[ci], asl[h]] = qk[p] * decay[p]
            q_s[rows[ci], hsl[h]] = q[p] * e_gc[p]
            k_s[rows[ci], hsl[h]] = k[p] * jnp.exp(gc[p][CHUNK - 1:CHUNK, :] - gc[p])
        return carry

    lax.fori_loop(0, n_chunks // LOCAL_CHUNKS, local_body, 0)

    def state_body(c, carry):
        rows = pl.ds(pl.multiple_of(c * CHUNK, CHUNK), CHUNK)
        g_last = gam_s[pl.ds(c * CHUNK + CHUNK - 1, 1), :]
        st = [state_s[h] for h in heads]
        ws = [_bdot(w_s[rows, hsl[h]], st[h]) for h in heads]
        qs = [_bdot(q_s[rows, hsl[h]], st[h]) for h in heads]
        u = [v_s[rows, hsl[h]] - ws[h] for h in heads]
        au = [_bdot(attn_s[rows, asl[h]], u[h]) for h in heads]
        ku = [_bdot(k_s[rows, hsl[h]], u[h], _TN) for h in heads]
        for h in heads:
            state_s[h] = st[h] * jnp.exp(g_last[:, h:h + 1]) + ku[h]
            o = qs[h] + au[h]
            o = o * lax.rsqrt(jnp.mean(o * o, axis=-1, keepdims=True) + NORM_EPS) * gnorm
            o_ref[0, rows, hsl[h]] = (o * _silu(z_ref[0, rows, hsl[h]])).astype(o_ref.dtype)
        return carry

    lax.fori_loop(0, n_chunks, state_body, 0)


def _gdn(big3, small3, conv_w, prm, gnorm, cs):
    b, s, _ = big3.shape
    return pl.pallas_call(
        functools.partial(_gdn_kernel, cs=cs),
        grid=(b, s // cs),
        in_specs=[pl.BlockSpec((1, cs, 3 * D_QK), lambda i, j: (i, j, 0)),
                  pl.BlockSpec((1, cs, D_QK), lambda i, j: (i, j, 3)),
                  pl.BlockSpec((1, cs, LANES), lambda i, j: (i, j, 0)),
                  pl.BlockSpec((CONV_WIDTH, 3 * D_QK), lambda i, j: (0, 0)),
                  pl.BlockSpec((8, LANES), lambda i, j: (0, 0)),
                  pl.BlockSpec((1, D_HEAD), lambda i, j: (0, 0))],
        out_specs=pl.BlockSpec((1, cs, D_QK), lambda i, j: (i, j, 0)),
        out_shape=jax.ShapeDtypeStruct((b, s, D_QK), BF16),
        scratch_shapes=[pltpu.VMEM((cs + 8, 3 * D_QK), F32),
                        pltpu.VMEM((cs, D_QK), F32), pltpu.VMEM((cs, D_QK), F32),
                        pltpu.VMEM((cs, D_QK), F32), pltpu.VMEM((cs, D_QK), F32),
                        pltpu.VMEM((cs, N_HEADS * CHUNK), F32),
                        pltpu.VMEM((cs, LANES), F32), pltpu.VMEM((cs, LANES), F32),
                        pltpu.VMEM((N_HEADS, D_HEAD, D_HEAD), F32)],
        compiler_params=pltpu.CompilerParams(dimension_semantics=("arbitrary", "arbitrary"),
                                             vmem_limit_bytes=VMEM_LIMIT),
        name="gdn",
    )(big3, big3, small3, conv_w, prm, gnorm)


def _mlstm_kernel(qk_ref, v_ref, og_ref, sm_ref, conv_ref, prm_ref, gn_ref, o_ref,
                  buf_ref, q_s, k_s, bcum_s, ipre_s, state_s, m_s, *, cs):
    n_chunks = cs // CHUNK

    @pl.when(pl.program_id(1) == 0)
    def _():
        state_s[...] = jnp.zeros(state_s.shape, F32)
        m_s[...] = jnp.zeros(m_s.shape, F32)

    qk = _conv_silu(qk_ref, conv_ref, buf_ref, cs)
    q_s[...] = qk[:, :D_QK]
    k_s[...] = qk[:, D_QK:] * (D_HEAD ** -0.5)

    pre = sm_ref[0] + prm_ref[0:1, :]
    capped = GATE_SOFTCAP * jnp.tanh(pre / GATE_SOFTCAP)
    logf = -_softplus(-capped)
    bcum_s[...] = _select_dot(_block_cumsum_matrix(cs), logf)
    ipre_s[...] = pltpu.roll(capped, N_HEADS, axis=1)

    _, incl = _chunk_masks()
    ones_aug = jnp.ones((CHUNK, D_HEAD), F32)

    heads = range(N_HEADS)
    hsl = [slice(h * D_HEAD, (h + 1) * D_HEAD) for h in heads]
    lanes = [3 * N_HEADS + h for h in heads]
    onehots = [_lane_onehot(lane) for lane in lanes]

    def chunk_body(c, carry):
        rows = pl.ds(pl.multiple_of(c * CHUNK, CHUNK), CHUNK)
        b_all = bcum_s[rows, :]
        comb_all = ipre_s[rows, :] - b_all
        q = [q_s[rows, hsl[h]] for h in heads]
        k = [k_s[rows, hsl[h]] for h in heads]
        v_aug = [jnp.concatenate([v_ref[0, rows, hsl[h]], ones_aug], axis=1) for h in heads]
        st = [state_s[h] for h in heads]
        m_st = [m_s[h] for h in heads]
        bc = [b_all[:, lane:lane + 1] for lane in lanes]
        comb_row = [_select_dot(onehots[h], comb_all, _NT) for h in heads]
        qk = [_bdot(q[h], k[h], _NT) for h in heads]
        inter = [_bdot(q[h], st[h]) for h in heads]
        b_last = [bc[h][CHUNK - 1:CHUNK, :] for h in heads]
        a_log = [b_last[h] + comb_all[:, lanes[h]:lanes[h] + 1] for h in heads]
        m_chunk = [jnp.max(a_log[h], axis=0, keepdims=True) for h in heads]
        wk = [k[h] * jnp.exp(a_log[h] - m_chunk[h]) for h in heads]
        d_state = [_bdot(wk[h], v_aug[h], _TN) for h in heads]
        d_log = [jnp.where(incl, bc[h] + comb_row[h], -jnp.inf) for h in heads]
        m_loc = [jnp.max(d_log[h], axis=-1, keepdims=True) for h in heads]
        p = [jnp.exp(d_log[h] - m_loc[h]) * qk[h] for h in heads]
        loc = [_bdot(p[h], v_aug[h]) for h in heads]
        for h in heads:
            m_inter = bc[h] + m_st[h]
            m_t = jnp.maximum(m_loc[h], m_inter)
            tot = jnp.exp(m_inter - m_t) * inter[h] + jnp.exp(m_loc[h] - m_t) * loc[h]
            hh = tot[:, :D_HEAD] / jnp.maximum(jnp.abs(tot[:, D_HEAD:]), jnp.exp(-m_t))
            m_new = jnp.maximum(b_last[h] + m_st[h], m_chunk[h])
            state_s[h] = (jnp.exp(b_last[h] + m_st[h] - m_new) * st[h]
                          + jnp.exp(m_chunk[h] - m_new) * d_state[h])
            m_s[h] = m_new
            hh = hh * lax.rsqrt(jnp.mean(hh * hh, axis=-1, keepdims=True) + NORM_EPS) * gn_ref[:, hsl[h]]
            o_ref[0, rows, hsl[h]] = (hh * _sigmoid(og_ref[0, rows, hsl[h]])).astype(o_ref.dtype)
        return carry

    lax.fori_loop(0, n_chunks, chunk_body, 0)


def _mlstm(big3, small3, conv_w, prm, gnorm, cs):
    b, s, _ = big3.shape
    return pl.pallas_call(
        functools.partial(_mlstm_kernel, cs=cs),
        grid=(b, s // cs),
        in_specs=[pl.BlockSpec((1, cs, 2 * D_QK), lambda i, j: (i, j, 2)),
                  pl.BlockSpec((1, cs, D_QK), lambda i, j: (i, j, 6)),
                  pl.BlockSpec((1, cs, D_QK), lambda i, j: (i, j, 7)),
                  pl.BlockSpec((1, cs, LANES), lambda i, j: (i, j, 0)),
                  pl.BlockSpec((CONV_WIDTH, 2 * D_QK), lambda i, j: (0, 0)),
                  pl.BlockSpec((8, LANES), lambda i, j: (0, 0)),
                  pl.BlockSpec((1, D_QK), lambda i, j: (0, 0))],
        out_specs=pl.BlockSpec((1, cs, D_QK), lambda i, j: (i, j, 0)),
        out_shape=jax.ShapeDtypeStruct((b, s, D_QK), BF16),
        scratch_shapes=[pltpu.VMEM((cs + 8, 2 * D_QK), F32),
                        pltpu.VMEM((cs, D_QK), F32), pltpu.VMEM((cs, D_QK), F32),
                        pltpu.VMEM((cs, LANES), F32), pltpu.VMEM((cs, LANES), F32),
                        pltpu.VMEM((N_HEADS, D_HEAD, 2 * D_HEAD), F32),
                        pltpu.VMEM((N_HEADS, 1, 1), F32)],
        compiler_params=pltpu.CompilerParams(dimension_semantics=("arbitrary", "arbitrary"),
                                             vmem_limit_bytes=VMEM_LIMIT),
        name="mlstm",
    )(big3, big3, big3, small3, conv_w, prm, gnorm)


N_SEG = D_MODEL // LANES
assert N_SEG == SUBLANES


def _store_token_tiles(ref, val, n):
    for s in range(N_SEG):
        ref[pl.ds(s, n, stride=N_SEG), :] = val[:, s * LANES:(s + 1) * LANES]


def _load_token_tiles(ref, n):
    return jnp.concatenate([ref[pl.ds(s, n, stride=N_SEG), :] for s in range(N_SEG)], axis=1)


def _merge_kernel(oa_ref, ob_ref, gates_ref, x_ref, wa_ref, wb_ref, wo_ref, g_ref, wr_ref, br_ref,
                  x1_ref, h2_ref, idx_ref, gate_ref, rank_ref, cnt_ref, carry_s, *, tm):
    @pl.when(pl.program_id(0) == 0)
    def _():
        carry_s[...] = jnp.zeros(carry_s.shape, F32)

    y_a = jnp.dot(oa_ref[...], wa_ref[...], preferred_element_type=F32)
    y_b = jnp.dot(ob_ref[...], wb_ref[...], preferred_element_type=F32)
    y = _sigmoid(gates_ref[:, :D_MODEL]) * y_a + _sigmoid(gates_ref[:, D_MODEL:]) * y_b
    x1 = x_ref[...] + jnp.dot(y.astype(BF16), wo_ref[...], preferred_element_type=F32)
    x1_ref[...] = x1
    h2 = _rms(x1, g_ref[...])
    _store_token_tiles(h2_ref, h2, tm)
    logits = jnp.dot(h2.astype(BF16), wr_ref[...], preferred_element_type=F32) + br_ref[...]

    lane = lax.broadcasted_iota(I32, (tm, LANES), 1).astype(F32)
    vals, sels = [], []
    idx_t = jnp.zeros((tm, LANES), F32)
    work = logits
    for k in range(TOP_K):
        m = jnp.max(work, axis=-1, keepdims=True)
        i = jnp.min(jnp.where(work == m, lane, float(LANES)), axis=-1, keepdims=True)
        sel = lane == i
        work = jnp.where(sel, -jnp.inf, work)
        idx_t = jnp.where(lane == float(k), i, idx_t)
        vals.append(m)
        sels.append(sel)
    es = [jnp.exp(v - vals[0]) for v in vals]
    denom = es[0] + es[1] + es[2] + es[3]
    gate_t = jnp.zeros((tm, LANES), F32)
    for k in range(TOP_K):
        gate_t = jnp.where(lane == float(k), es[k] / denom, gate_t)
    idx_ref[...] = idx_t[:, :TOP_K].astype(I32)
    gate_ref[...] = gate_t[:, :TOP_K]

    onehot = jnp.zeros((tm, LANES), F32)
    for sel in sels:
        onehot = onehot + jnp.where(sel, 1.0, 0.0)
    r = lax.broadcasted_iota(I32, (tm, tm), 0)
    c = lax.broadcasted_iota(I32, (tm, tm), 1)
    before = jnp.where(r > c, 1.0, 0.0).astype(BF16)
    cum = jnp.dot(before, onehot.astype(BF16), preferred_element_type=F32) + carry_s[...]
    rank_t = jnp.zeros((tm, LANES), F32)
    for k, sel in enumerate(sels):
        rk = jnp.sum(jnp.where(sel, cum, 0.0), axis=-1, keepdims=True)
        rank_t = jnp.where(lane == float(k), rk, rank_t)
    rank_ref[...] = rank_t[:, :TOP_K].astype(I32)
    total = carry_s[...] + jnp.sum(onehot, axis=0, keepdims=True)
    carry_s[...] = total
    cnt_ref[...] = total.astype(I32)


def _merge(oa, ob, big, x2, wa, wb, wo, g, wr, br, tm):
    t = x2.shape[0]
    const = lambda i: (0, 0)
    return pl.pallas_call(
        functools.partial(_merge_kernel, tm=tm),
        grid=(t // tm,),
        in_specs=[pl.BlockSpec((tm, D_QK), lambda i: (i, 0)),
                  pl.BlockSpec((tm, D_QK), lambda i: (i, 0)),
                  pl.BlockSpec((tm, 2 * D_MODEL), lambda i: (i, 2)),
                  pl.BlockSpec((tm, D_MODEL), lambda i: (i, 0)),
                  pl.BlockSpec((D_QK, D_MODEL), const),
                  pl.BlockSpec((D_QK, D_MODEL), const),
                  pl.BlockSpec((D_MODEL, D_MODEL), const),
                  pl.BlockSpec((1, D_MODEL), const),
                  pl.BlockSpec((D_MODEL, LANES), const),
                  pl.BlockSpec((1, LANES), const)],
        out_specs=[pl.BlockSpec((tm, D_MODEL), lambda i: (i, 0)),
                   pl.BlockSpec((tm * N_SEG, LANES), lambda i: (i, 0)),
                   pl.BlockSpec((tm, TOP_K), lambda i: (i, 0)),
                   pl.BlockSpec((tm, TOP_K), lambda i: (i, 0)),
                   pl.BlockSpec((tm, TOP_K), lambda i: (i, 0)),
                   pl.BlockSpec((1, LANES), const)],
        out_shape=[jax.ShapeDtypeStruct((t, D_MODEL), F32),
                   jax.ShapeDtypeStruct((t * N_SEG, LANES), F32),
                   jax.ShapeDtypeStruct((t, TOP_K), I32),
                   jax.ShapeDtypeStruct((t, TOP_K), F32),
                   jax.ShapeDtypeStruct((t, TOP_K), I32),
                   jax.ShapeDtypeStruct((1, LANES), I32)],
        scratch_shapes=[pltpu.VMEM((1, LANES), F32)],
        compiler_params=pltpu.CompilerParams(dimension_semantics=("arbitrary",),
                                             vmem_limit_bytes=VMEM_LIMIT),
        name="merge_router",
    )(oa, ob, big, x2, wa, wb, wo, g, wr, br)


def _expert_kernel(te_ref, nu_ref, nv_ref, tok_ref, tok_next_ref, slot_ref, slot_prev_ref, h2_hbm,
                   wgu_ref, wd_ref, perm_ref, bg_ref, bu_ref, bd_ref, y_hbm,
                   xbuf0, xbuf1, ybuf0, ybuf1, wg_s, wu_s, wd_s, in_sem, out_sem, *, bm):
    i = pl.program_id(0)
    n_used = nu_ref[0]
    new_expert = (i == 0) | (te_ref[i] != te_ref[jnp.maximum(i - 1, 0)])
    xbufs = (xbuf0, xbuf1)
    ybufs = (ybuf0, ybuf1)

    def token_tile(ref, r):
        return ref.at[pl.ds(pl.multiple_of(r * N_SEG, N_SEG), N_SEG)]

    def gather_row(ids_ref, s, r):
        pltpu.make_async_copy(token_tile(h2_hbm, ids_ref[r]), token_tile(xbufs[s], r),
                              in_sem.at[s]).start()

    def wait_gather(s):
        pltpu.make_async_copy(h2_hbm.at[pl.ds(0, bm * N_SEG)], xbufs[s], in_sem.at[s]).wait()

    def wait_scatter(s, n_rows):
        n = pl.multiple_of(n_rows * N_SEG, N_SEG)
        pltpu.make_async_copy(ybufs[s].at[pl.ds(0, n)], y_hbm.at[pl.ds(0, n)], out_sem.at[s]).wait()

    @pl.when(i == 0)
    def _():
        def body(r, carry):
            gather_row(tok_ref, 0, r)
            return carry
        lax.fori_loop(0, bm, body, 0, unroll=DMA_UNROLL)

    n_prev = jnp.where(i >= 1, nv_ref[jnp.maximum(i - 1, 0)], 0)

    def scatter_row(ids_ref, s, r):
        pltpu.make_async_copy(token_tile(ybufs[s], r), token_tile(y_hbm, ids_ref[r]),
                              out_sem.at[s]).start()

    def scatter_loop(ids_ref, s, n_rows):
        def one(r, carry):
            scatter_row(ids_ref, s, r)
            return carry

        def group(j, carry):
            for q in range(DMA_UNROLL):
                scatter_row(ids_ref, s, j * DMA_UNROLL + q)
            return carry
        n_groups = n_rows // DMA_UNROLL
        lax.fori_loop(0, n_groups, group, 0)
        lax.fori_loop(n_groups * DMA_UNROLL, n_rows, one, 0)

    def tile_body(cur, prev_full):
        nxt = 1 - cur
        wait_gather(cur)

        @pl.when(i >= 2)
        def _():
            wait_scatter(cur, nv_ref[i - 2])

        if not prev_full:
            scatter_loop(slot_prev_ref, nxt, n_prev)

        @pl.when(new_expert)
        def _():
            half = PERM_BLOCK // 2
            for j in range(2 * D_EXPERT // PERM_BLOCK):
                blk = wgu_ref[0, :, j * PERM_BLOCK:(j + 1) * PERM_BLOCK].astype(BF16)
                split = jnp.dot(blk, perm_ref[...], preferred_element_type=F32)
                wg_s[:, j * half:(j + 1) * half] = split[:, :half].astype(BF16)
                wu_s[:, j * half:(j + 1) * half] = split[:, half:].astype(BF16)
            wd_s[...] = wd_ref[0].astype(BF16)

        n_grp = bm // DMA_GROUPS

        def dma_group(gi):
            for r in range(gi * n_grp, (gi + 1) * n_grp):
                gather_row(tok_next_ref, nxt, r)
                if prev_full:
                    scatter_row(slot_prev_ref, nxt, r)

        dma_group(0)
        x = _load_token_tiles(xbufs[cur], bm).astype(BF16)
        g = jnp.dot(x, wg_s[...], preferred_element_type=F32) + bg_ref[0]
        dma_group(1)
        u = jnp.dot(x, wu_s[...], preferred_element_type=F32) + bu_ref[0]
        dma_group(2)
        gate = jnp.minimum(g, SWIGLU_LIMIT)
        up = jnp.clip(u, -SWIGLU_LIMIT, SWIGLU_LIMIT)
        act = gate * _sigmoid(SWIGLU_ALPHA * gate) * (up + 1.0)
        dma_group(3)
        y = jnp.dot(act.astype(BF16), wd_s[...], preferred_element_type=F32) + bd_ref[0]
        _store_token_tiles(ybufs[cur], y, bm)

        @pl.when(i == n_used - 1)
        def _():
            scatter_loop(slot_ref, cur, nv_ref[i])
            wait_gather(nxt)
            wait_scatter(cur, nv_ref[i])

            @pl.when(i >= 1)
            def _():
                wait_scatter(nxt, nv_ref[i - 1])

    for parity in range(2):
        for prev_full in (False, True):
            full = (n_prev == bm) if prev_full else (n_prev != bm)
            cond = (i < n_used) & (lax.rem(i, 2) == parity) & full
            pl.when(cond)(functools.partial(tile_body, parity, prev_full))


def _split_permutation():
    half = PERM_BLOCK // 2
    src = jnp.arange(PERM_BLOCK)[:, None]
    dst = jnp.arange(PERM_BLOCK)[None, :]
    return (src == jnp.where(dst < half, 2 * dst, 2 * (dst - half) + 1)).astype(BF16)


def _experts(tile_expert, n_used, tile_valid, row_tok, row_slot, h2, wgu, wd, bg, bu, bd, bm):
    n_rows = row_tok.shape[0]
    n_tiles = n_rows // bm
    n_slots = h2.shape[0] // N_SEG * TOP_K
    cur = lambda i, te, nu, nv: (jnp.minimum(i, nu[0] - 1),)
    nxt = lambda i, te, nu, nv: (jnp.minimum(i + 1, nu[0] - 1),)
    prv = lambda i, te, nu, nv: (jnp.clip(i - 1, 0, nu[0] - 1),)
    w_map = lambda i, te, nu, nv: (te[i], 0, 0)
    grid_spec = pltpu.PrefetchScalarGridSpec(
        num_scalar_prefetch=3,
        grid=(n_tiles,),
        in_specs=[pl.BlockSpec((bm,), cur, memory_space=pltpu.SMEM),
                  pl.BlockSpec((bm,), nxt, memory_space=pltpu.SMEM),
                  pl.BlockSpec((bm,), cur, memory_space=pltpu.SMEM),
                  pl.BlockSpec((bm,), prv, memory_space=pltpu.SMEM),
                  pl.BlockSpec(memory_space=pl.ANY),
                  pl.BlockSpec((1, D_MODEL, 2 * D_EXPERT), w_map),
                  pl.BlockSpec((1, D_EXPERT, D_MODEL), w_map),
                  pl.BlockSpec((PERM_BLOCK, PERM_BLOCK), lambda i, te, nu, nv: (0, 0)),
                  pl.BlockSpec((1, 1, D_EXPERT), w_map),
                  pl.BlockSpec((1, 1, D_EXPERT), w_map),
                  pl.BlockSpec((1, 1, D_MODEL), w_map)],
        out_specs=pl.BlockSpec(memory_space=pl.ANY),
        scratch_shapes=[pltpu.VMEM((bm * N_SEG, LANES), F32), pltpu.VMEM((bm * N_SEG, LANES), F32),
                        pltpu.VMEM((bm * N_SEG, LANES), F32), pltpu.VMEM((bm * N_SEG, LANES), F32),
                        pltpu.VMEM((D_MODEL, D_EXPERT), BF16), pltpu.VMEM((D_MODEL, D_EXPERT), BF16),
                        pltpu.VMEM((D_EXPERT, D_MODEL), BF16),
                        pltpu.SemaphoreType.DMA((2,)), pltpu.SemaphoreType.DMA((2,))],
    )
    return pl.pallas_call(
        functools.partial(_expert_kernel, bm=bm),
        grid_spec=grid_spec,
        out_shape=jax.ShapeDtypeStruct((n_slots * N_SEG, LANES), F32),
        compiler_params=pltpu.CompilerParams(dimension_semantics=("arbitrary",),
                                             vmem_limit_bytes=VMEM_LIMIT,
                                             has_side_effects=True),
        name="experts",
    )(tile_expert, n_used, tile_valid, row_tok, row_tok, row_slot, row_slot, h2, wgu, wd, _split_permutation(),
      bg, bu, bd)


def _combine_kernel(y0_ref, y1_ref, y2_ref, y3_ref, gate_ref, x1_ref, g_ref, o_ref, *, tm):
    acc = x1_ref[...]
    for k, y_ref in enumerate((y0_ref, y1_ref, y2_ref, y3_ref)):
        acc = acc + gate_ref[:, k:k + 1] * _load_token_tiles(y_ref, tm)
    o_ref[...] = _rms(acc, g_ref[...])


def _combine(y_slots, gates, x1, g, tm):
    t = x1.shape[0]
    nb = t // tm

    def slot_spec(k):
        return pl.BlockSpec((tm * N_SEG, LANES), lambda i: (k * nb + i, 0))

    return pl.pallas_call(
        functools.partial(_combine_kernel, tm=tm),
        grid=(nb,),
        in_specs=[slot_spec(0), slot_spec(1), slot_spec(2), slot_spec(3),
                  pl.BlockSpec((tm, TOP_K), lambda i: (i, 0)),
                  pl.BlockSpec((tm, D_MODEL), lambda i: (i, 0)),
                  pl.BlockSpec((1, D_MODEL), lambda i: (0, 0))],
        out_specs=pl.BlockSpec((tm, D_MODEL), lambda i: (i, 0)),
        out_shape=jax.ShapeDtypeStruct((t, D_MODEL), F32),
        compiler_params=pltpu.CompilerParams(dimension_semantics=("arbitrary",),
                                             vmem_limit_bytes=VMEM_LIMIT),
        name="combine",
    )(y_slots, y_slots, y_slots, y_slots, gates, x1, g)


def _lane_row(vec, lane0):
    n = vec.shape[0]
    return jnp.zeros((8, LANES), F32).at[0, lane0:lane0 + n].set(vec.astype(F32))


def _tile_size(n, pref):
    return pref if n % pref == 0 else n


def kernel(x, norm_mix, w_in, gdn_conv, gdn_a_log, gdn_dt_bias, gdn_norm, ml_conv, ml_b_i, ml_b_f,
           ml_norm, w_up_gdn, w_up_ml, w_out, norm_ffn, w_router, b_router, w_gate_up, b_gate_up,
           w_down, b_down, norm_final):
    assert norm_mix.shape[0] == 1, "single-layer stack"
    b, s, d = x.shape
    assert d == D_MODEL and s % CHUNK == 0
    t = b * s
    x2 = x.reshape(t, d)

    w = w_in[0]
    w_big = jnp.concatenate([_cols(w, n) for n in ("g_q", "g_k", "g_v", "g_z", "m_q", "m_k", "m_v",
                                                   "m_o", "gate_gdn", "gate_ml")], axis=1).astype(BF16)
    w_small = jnp.concatenate([_cols(w, n) for n in ("g_a", "g_b", "m_i", "m_f")], axis=1)
    w_small = jnp.pad(w_small, ((0, 0), (0, LANES - N_SMALL))).astype(BF16)

    tm = _tile_size(t, 512)
    big, small = _in_proj(x2, norm_mix[0][None, :], w_big, w_small, tm)
    big3 = big.reshape(b, s, N_BIG)
    small3 = small.reshape(b, s, LANES)

    cs = _tile_size(s, 512)
    gdn_prm = _lane_row(gdn_a_log[0], 0).at[1, 0:N_HEADS].set(gdn_dt_bias[0].astype(F32))
    oa = _gdn(big3, small3, gdn_conv[0].astype(F32), gdn_prm, gdn_norm[0][None, :].astype(F32), cs)
    ml_prm = _lane_row(ml_b_i[0], 2 * N_HEADS).at[0, 3 * N_HEADS:4 * N_HEADS].set(ml_b_f[0].astype(F32))
    ob = _mlstm(big3, small3, ml_conv[0].astype(F32), ml_prm, ml_norm[0][None, :].astype(F32), cs)

    w_r = jnp.pad(w_router[0], ((0, 0), (0, LANES - N_EXPERTS))).astype(BF16)
    b_r = jnp.full((1, LANES), -1e30, F32).at[0, :N_EXPERTS].set(b_router[0].astype(F32))
    x1, h2, idx, gates, rank, counts = _merge(
        oa.reshape(t, D_QK), ob.reshape(t, D_QK), big, x2, w_up_gdn[0].astype(BF16),
        w_up_ml[0].astype(BF16), w_out[0].astype(BF16), norm_ffn[0][None, :], w_r, b_r, tm)

    bm = EXPERT_TILE
    n_assign = t * TOP_K
    n_tiles = -(-n_assign // bm) + N_EXPERTS
    counts = counts[0, :N_EXPERTS]
    padded = (counts + bm - 1) // bm * bm
    pend = jnp.cumsum(padded)
    pstart = pend - padded
    dest = (pstart[idx] + rank).reshape(-1)
    n_used = (pend[-1] // bm).astype(I32)
    tile_ids = jnp.minimum(jnp.arange(n_tiles, dtype=I32), n_used - 1)
    tile_expert = jnp.minimum(jnp.sum((pend[None, :] <= (tile_ids * bm)[:, None]).astype(I32), axis=1),
                              N_EXPERTS - 1)
    tile_valid = jnp.clip(counts[tile_expert] - (tile_ids * bm - pstart[tile_expert]), 0, bm).astype(I32)
    assign = jnp.arange(n_assign, dtype=I32)
    row_slot = jnp.zeros((n_tiles * bm,), I32).at[dest].set((assign % TOP_K) * t + assign // TOP_K,
                                                            unique_indices=True)
    row_tok = row_slot % t

    bgu = b_gate_up[0]
    y_slots = _experts(tile_expert.astype(I32), n_used.reshape(1), tile_valid, row_tok, row_slot, h2,
                       w_gate_up[0], w_down[0],
                       bgu[:, None, 0::2].astype(F32), bgu[:, None, 1::2].astype(F32),
                       b_down[0][:, None, :].astype(F32), bm)
    out = _combine(y_slots, gates, x1, norm_final[None, :], _tile_size(t, 256))
    return out.reshape(b, s, d)
```

```python
import functools

import jax
import jax.numpy as jnp
from jax import lax
from jax.experimental import pallas as pl
from jax.experimental.pallas import tpu as pltpu

F32 = jnp.float32
BF16 = jnp.bfloat16
I32 = jnp.int32

D_MODEL = 1024
N_HEADS = 4
D_HEAD = 128
D_QK = N_HEADS * D_HEAD
CONV_WIDTH = 4
CHUNK = 64
GATE_SOFTCAP = 15.0
N_EXPERTS = 32
TOP_K = 4
D_EXPERT = 1024
SWIGLU_LIMIT = 7.0
SWIGLU_ALPHA = 1.702
NORM_EPS = 1e-6

LANES = 128
SUBLANES = 8
N_SMALL = 16
N_BIG = 6 * D_MODEL
VMEM_LIMIT = 56 * 1024 * 1024
EXPERT_TILE = 512
DMA_UNROLL = 8
DMA_GROUPS = 4
PERM_BLOCK = 256
LOCAL_CHUNKS = 2

_OFF = {}
_o = 0
for _name, _w in (("g_q", D_QK), ("g_k", D_QK), ("g_v", D_QK), ("g_z", D_QK), ("g_a", N_HEADS),
                  ("g_b", N_HEADS), ("m_q", D_QK), ("m_k", D_QK), ("m_v", D_QK), ("m_o", D_QK),
                  ("m_i", N_HEADS), ("m_f", N_HEADS), ("gate_gdn", D_MODEL), ("gate_ml", D_MODEL)):
    _OFF[_name] = (_o, _w)
    _o += _w


def _cols(w, name):
    o, n = _OFF[name]
    return w[:, o:o + n]


_NN = (((1,), (0,)), ((), ()))
_NT = (((1,), (1,)), ((), ()))
_TN = (((0,), (0,)), ((), ()))


def _bdot(a, b, dims=_NN):
    return lax.dot_general(a.astype(BF16), b.astype(BF16), dims, preferred_element_type=F32)


def _split3(a):
    hi = a.astype(BF16)
    r = a - hi.astype(F32)
    mid = r.astype(BF16)
    return hi, mid, (r - mid.astype(F32)).astype(BF16)


def _select_dot(sel01, x, dims=_NN):
    s = sel01.astype(BF16)
    h, m, l = _split3(x)
    d = functools.partial(lax.dot_general, dimension_numbers=dims, preferred_element_type=F32)
    return d(s, h) + (d(s, m) + d(s, l))


def _sigmoid(x):
    return 1.0 / (1.0 + jnp.exp(-x))


def _silu(x):
    return x * _sigmoid(x)


def _softplus(x):
    return jnp.maximum(x, 0.0) + jnp.log(1.0 + jnp.exp(-jnp.abs(x)))


def _rms(x, g):
    return x * lax.rsqrt(jnp.mean(x * x, axis=-1, keepdims=True) + NORM_EPS) * g


def _in_proj_kernel(x_ref, g_ref, wbig_ref, wsmall_ref, big_ref, small_ref, *, n_chunk):
    h = _rms(x_ref[...], g_ref[...]).astype(BF16)
    for c in range(N_BIG // n_chunk):
        sl = slice(c * n_chunk, (c + 1) * n_chunk)
        big_ref[:, sl] = jnp.dot(h, wbig_ref[:, sl], preferred_element_type=F32)
    small_ref[...] = jnp.dot(h, wsmall_ref[...], preferred_element_type=F32)


def _in_proj(x2, g, w_big, w_small, tm):
    t = x2.shape[0]
    return pl.pallas_call(
        functools.partial(_in_proj_kernel, n_chunk=1024),
        grid=(t // tm,),
        in_specs=[pl.BlockSpec((tm, D_MODEL), lambda i: (i, 0)),
                  pl.BlockSpec((1, D_MODEL), lambda i: (0, 0)),
                  pl.BlockSpec((D_MODEL, N_BIG), lambda i: (0, 0)),
                  pl.BlockSpec((D_MODEL, LANES), lambda i: (0, 0))],
        out_specs=[pl.BlockSpec((tm, N_BIG), lambda i: (i, 0)),
                   pl.BlockSpec((tm, LANES), lambda i: (i, 0))],
        out_shape=[jax.ShapeDtypeStruct((t, N_BIG), F32),
                   jax.ShapeDtypeStruct((t, LANES), F32)],
        compiler_params=pltpu.CompilerParams(dimension_semantics=("arbitrary",),
                                             vmem_limit_bytes=VMEM_LIMIT),
        name="in_proj",
    )(x2, g, w_big, w_small)


def _chunk_masks():
    r = lax.broadcasted_iota(I32, (CHUNK, CHUNK), 0)
    c = lax.broadcasted_iota(I32, (CHUNK, CHUNK), 1)
    return r > c, r >= c


def _block_cumsum_matrix(cs):
    r = lax.broadcasted_iota(I32, (cs, cs), 0)
    c = lax.broadcasted_iota(I32, (cs, cs), 1)
    same = (r // CHUNK) == (c // CHUNK)
    return jnp.where(same & (r >= c), 1.0, 0.0).astype(F32)


def _lane_onehot(lane):
    return jnp.where(lax.broadcasted_iota(I32, (CHUNK, LANES), 1) == lane, 1.0, 0.0).astype(F32)


def _conv_silu_pieces(x_ref, w_ref, buf_ref, cs, width):
    @pl.when(pl.program_id(1) == 0)
    def _():
        buf_ref[0:8, :] = jnp.zeros((8, buf_ref.shape[1]), F32)

    buf_ref[8:8 + cs, :] = x_ref[0]
    for r0 in range(0, cs, CHUNK):
        for c0 in range(0, buf_ref.shape[1], width):
            cols = slice(c0, c0 + width)
            acc = w_ref[CONV_WIDTH - 1:CONV_WIDTH, cols] * buf_ref[8 + r0:8 + r0 + CHUNK, cols]
            for j in range(CONV_WIDTH - 1):
                s = 8 - (CONV_WIDTH - 1) + j + r0
                acc = acc + w_ref[j:j + 1, cols] * buf_ref[s:s + CHUNK, cols]
            yield r0, c0, _silu(acc)
    buf_ref[0:8, :] = buf_ref[cs:cs + 8, :]


def _gdn_kernel(qkv_ref, z_ref, sm_ref, conv_ref, prm_ref, gn_ref, o_ref,
                buf_ref, q_s, k_s, v_s, w_s, attn_s, gam_s, beta_s, state_s, *, cs):
    n_chunks = cs // CHUNK

    @pl.when(pl.program_id(1) == 0)
    def _():
        state_s[...] = jnp.zeros(state_s.shape, F32)

    for r0, c0, piece in _conv_silu_pieces(qkv_ref, conv_ref, buf_ref, cs, D_QK):
        rows = slice(r0, r0 + CHUNK)
        if c0 == 2 * D_QK:
            v_s[rows, :] = piece
            continue
        dst, scale = (q_s, D_HEAD ** -0.5) if c0 == 0 else (k_s, 1.0)
        for h in range(N_HEADS):
            sl = slice(h * D_HEAD, (h + 1) * D_HEAD)
            a = piece[:, sl]
            dst[rows, sl] = a * (lax.rsqrt(jnp.sum(a * a, axis=-1, keepdims=True) + NORM_EPS) * scale)

    sm = sm_ref[0]
    logdec = -jnp.exp(prm_ref[0:1, :]) * _softplus(sm + prm_ref[1:2, :])
    gam_s[...] = _select_dot(_block_cumsum_matrix(cs), logdec)
    beta_s[...] = _sigmoid(sm)

    strict, incl = _chunk_masks()
    eye = jnp.where(lax.broadcasted_iota(I32, (CHUNK, CHUNK), 0)
                    == lax.broadcasted_iota(I32, (CHUNK, CHUNK), 1), 1.0, 0.0).astype(F32)
    gnorm = gn_ref[...]

    heads = range(N_HEADS)
    hsl = [slice(h * D_HEAD, (h + 1) * D_HEAD) for h in heads]
    asl = [slice(h * CHUNK, (h + 1) * CHUNK) for h in heads]
    onehots = [_lane_onehot(h) for h in heads]

    def local_body(it, carry):
        rows = [pl.ds(pl.multiple_of((it * LOCAL_CHUNKS + ci) * CHUNK, CHUNK), CHUNK)
                for ci in range(LOCAL_CHUNKS)]
        probs = [(ci, h) for ci in range(LOCAL_CHUNKS) for h in heads]
        gam_all = [gam_s[r, :] for r in rows]
        beta_all = [beta_s[r, :] for r in rows]
        q = [q_s[rows[ci], hsl[h]] for ci, h in probs]
        k = [k_s[rows[ci], hsl[h]] for ci, h in probs]
        v = [v_s[rows[ci], hsl[h]] for ci, h in probs]
        gc = [gam_all[ci][:, h:h + 1] for ci, h in probs]
        beta = [beta_all[ci][:, N_HEADS + h:N_HEADS + h + 1] for ci, h in probs]
        g_row = [_select_dot(onehots[h], gam_all[ci], _NT) for ci, h in probs]
        kb = [a * b for a, b in zip(k, beta)]
        kk = [_bdot(a, b, _NT) for a, b in zip(kb, k)]
        qk = [_bdot(a, b, _NT) for a, b in zip(q, k)]
        decay = [jnp.where(incl, jnp.exp(jnp.minimum(a - b, 0.0)), 0.0) for a, b in zip(gc, g_row)]
        lower = [jnp.where(strict, a * b, 0.0) for a, b in zip(kk, decay)]
        inv = [eye - a for a in lower]
        pw = [_bdot(a, a) for a in lower]
        for lvl in range(5):
            upd = [_bdot(a, b) for a, b in zip(inv, pw)]
            if lvl < 4:
                pw = [_bdot(a, a) for a in pw]
            inv = [a + b for a, b in zip(inv, upd)]
        e_gc = [jnp.exp(a) for a in gc]
        rhs = [jnp.concatenate([a * b, c * d], axis=1) for a, b, c, d in zip(v, beta, kb, e_gc)]
        sol = [_bdot(a, b) for a, b in zip(inv, rhs)]
        for p, (ci, h) in enumerate(probs):
            v_s[rows[ci], hsl[h]] = sol[p][:, :D_HEAD]
            w_s[rows[ci], hsl[h]] = sol[p][:, D_HEAD:]
            attn_s[rows[ci], asl[h]] = qk[p] * decay[p]
            q_s[rows[ci], hsl[h]] = q[p] * e_gc[p]
            k_s[rows[ci], hsl[h]] = k[p] * jnp.exp(gc[p][CHUNK - 1:CHUNK, :] - gc[p])
        return carry

    lax.fori_loop(0, n_chunks // LOCAL_CHUNKS, local_body, 0)

    def state_body(c, carry):
        rows = pl.ds(pl.multiple_of(c * CHUNK, CHUNK), CHUNK)
        g_last = gam_s[pl.ds(c * CHUNK + CHUNK - 1, 1), :]
        st = [state_s[h] for h in heads]
        ws = [_bdot(w_s[rows, hsl[h]], st[h]) for h in heads]
        qs = [_bdot(q_s[rows, hsl[h]], st[h]) for h in heads]
        u = [v_s[rows, hsl[h]] - ws[h] for h in heads]
        au = [_bdot(attn_s[rows, asl[h]], u[h]) for h in heads]
        ku = [_bdot(k_s[rows, hsl[h]], u[h], _TN) for h in heads]
        for h in heads:
            state_s[h] = st[h] * jnp.exp(g_last[:, h:h + 1]) + ku[h]
            o = qs[h] + au[h]
            o = o * lax.rsqrt(jnp.mean(o * o, axis=-1, keepdims=True) + NORM_EPS) * gnorm
            o_ref[0, rows, hsl[h]] = (o * _silu(z_ref[0, rows, hsl[h]])).astype(o_ref.dtype)
        return carry

    lax.fori_loop(0, n_chunks, state_body, 0)


def _gdn(big3, small3, conv_w, prm, gnorm, cs):
    b, s, _ = big3.shape
    return pl.pallas_call(
        functools.partial(_gdn_kernel, cs=cs),
        grid=(b, s // cs),
        in_specs=[pl.BlockSpec((1, cs, 3 * D_QK), lambda i, j: (i, j, 0)),
                  pl.BlockSpec((1, cs, D_QK), lambda i, j: (i, j, 3)),
                  pl.BlockSpec((1, cs, LANES), lambda i, j: (i, j, 0)),
                  pl.BlockSpec((CONV_WIDTH, 3 * D_QK), lambda i, j: (0, 0)),
                  pl.BlockSpec((8, LANES), lambda i, j: (0, 0)),
                  pl.BlockSpec((1, D_HEAD), lambda i, j: (0, 0))],
        out_specs=pl.BlockSpec((1, cs, D_QK), lambda i, j: (i, j, 0)),
        out_shape=jax.ShapeDtypeStruct((b, s, D_QK), BF16),
        scratch_shapes=[pltpu.VMEM((cs + 8, 3 * D_QK), F32),
                        pltpu.VMEM((cs, D_QK), F32), pltpu.VMEM((cs, D_QK), F32),
                        pltpu.VMEM((cs, D_QK), F32), pltpu.VMEM((cs, D_QK), F32),
                        pltpu.VMEM((cs, N_HEADS * CHUNK), F32),
                        pltpu.VMEM((cs, LANES), F32), pltpu.VMEM((cs, LANES), F32),
                        pltpu.VMEM((N_HEADS, D_HEAD, D_HEAD), F32)],
        compiler_params=pltpu.CompilerParams(dimension_semantics=("arbitrary", "arbitrary"),
                                             vmem_limit_bytes=VMEM_LIMIT),
        name="gdn",
    )(big3, big3, small3, conv_w, prm, gnorm)


def _mlstm_kernel(qk_ref, v_ref, og_ref, sm_ref, conv_ref, prm_ref, gn_ref, o_ref,
                  buf_ref, q_s, k_s, bcum_s, ipre_s, state_s, m_s, *, cs):
    n_chunks = cs // CHUNK

    @pl.when(pl.program_id(1) == 0)
    def _():
        state_s[...] = jnp.zeros(state_s.shape, F32)
        m_s[...] = jnp.zeros(m_s.shape, F32)

    for r0, c0, piece in _conv_silu_pieces(qk_ref, conv_ref, buf_ref, cs, D_QK):
        if c0 == 0:
            q_s[r0:r0 + CHUNK, :] = piece
        else:
            k_s[r0:r0 + CHUNK, :] = piece * (D_HEAD ** -0.5)

    pre = sm_ref[0] + prm_ref[0:1, :]
    capped = GATE_SOFTCAP * jnp.tanh(pre / GATE_SOFTCAP)
    logf = -_softplus(-capped)
    bcum_s[...] = _select_dot(_block_cumsum_matrix(cs), logf)
    ipre_s[...] = pltpu.roll(capped, N_HEADS, axis=1)

    _, incl = _chunk_masks()
    ones_aug = jnp.ones((CHUNK, D_HEAD), F32)

    heads = range(N_HEADS)
    hsl = [slice(h * D_HEAD, (h + 1) * D_HEAD) for h in heads]
    lanes = [3 * N_HEADS + h for h in heads]
    onehots = [_lane_onehot(lane) for lane in lanes]

    def chunk_body(c, carry):
        rows = pl.ds(pl.multiple_of(c * CHUNK, CHUNK), CHUNK)
        b_all = bcum_s[rows, :]
        comb_all = ipre_s[rows, :] - b_all
        q = [q_s[rows, hsl[h]] for h in heads]
        k = [k_s[rows, hsl[h]] for h in heads]
        v_aug = [jnp.concatenate([v_ref[0, rows, hsl[h]], ones_aug], axis=1) for h in heads]
        st = [state_s[h] for h in heads]
        m_st = [m_s[h] for h in heads]
        bc = [b_all[:, lane:lane + 1] for lane in lanes]
        comb_row = [_select_dot(onehots[h], comb_all, _NT) for h in heads]
        qk = [_bdot(q[h], k[h], _NT) for h in heads]
        inter = [_bdot(q[h], st[h]) for h in heads]
        b_last = [bc[h][CHUNK - 1:CHUNK, :] for h in heads]
        a_log = [b_last[h] + comb_all[:, lanes[h]:lanes[h] + 1] for h in heads]
        m_chunk = [jnp.max(a_log[h], axis=0, keepdims=True) for h in heads]
        wk = [k[h] * jnp.exp(a_log[h] - m_chunk[h]) for h in heads]
        d_state = [_bdot(wk[h], v_aug[h], _TN) for h in heads]
        d_log = [jnp.where(incl, bc[h] + comb_row[h], -jnp.inf) for h in heads]
        m_loc = [jnp.max(d_log[h], axis=-1, keepdims=True) for h in heads]
        p = [jnp.exp(d_log[h] - m_loc[h]) * qk[h] for h in heads]
        loc = [_bdot(p[h], v_aug[h]) for h in heads]
        for h in heads:
            m_inter = bc[h] + m_st[h]
            m_t = jnp.maximum(m_loc[h], m_inter)
            tot = jnp.exp(m_inter - m_t) * inter[h] + jnp.exp(m_loc[h] - m_t) * loc[h]
            hh = tot[:, :D_HEAD] / jnp.maximum(jnp.abs(tot[:, D_HEAD:]), jnp.exp(-m_t))
            m_new = jnp.maximum(b_last[h] + m_st[h], m_chunk[h])
            state_s[h] = (jnp.exp(b_last[h] + m_st[h] - m_new) * st[h]
                          + jnp.exp(m_chunk[h] - m_new) * d_state[h])
            m_s[h] = m_new
            hh = hh * lax.rsqrt(jnp.mean(hh * hh, axis=-1, keepdims=True) + NORM_EPS) * gn_ref[:, hsl[h]]
            o_ref[0, rows, hsl[h]] = (hh * _sigmoid(og_ref[0, rows, hsl[h]])).astype(o_ref.dtype)
        return carry

    lax.fori_loop(0, n_chunks, chunk_body, 0)


def _mlstm(big3, small3, conv_w, prm, gnorm, cs):
    b, s, _ = big3.shape
    return pl.pallas_call(
        functools.partial(_mlstm_kernel, cs=cs),
        grid=(b, s // cs),
        in_specs=[pl.BlockSpec((1, cs, 2 * D_QK), lambda i, j: (i, j, 2)),
                  pl.BlockSpec((1, cs, D_QK), lambda i, j: (i, j, 6)),
                  pl.BlockSpec((1, cs, D_QK), lambda i, j: (i, j, 7)),
                  pl.BlockSpec((1, cs, LANES), lambda i, j: (i, j, 0)),
                  pl.BlockSpec((CONV_WIDTH, 2 * D_QK), lambda i, j: (0, 0)),
                  pl.BlockSpec((8, LANES), lambda i, j: (0, 0)),
                  pl.BlockSpec((1, D_QK), lambda i, j: (0, 0))],
        out_specs=pl.BlockSpec((1, cs, D_QK), lambda i, j: (i, j, 0)),
        out_shape=jax.ShapeDtypeStruct((b, s, D_QK), BF16),
        scratch_shapes=[pltpu.VMEM((cs + 8, 2 * D_QK), F32),
                        pltpu.VMEM((cs, D_QK), F32), pltpu.VMEM((cs, D_QK), F32),
                        pltpu.VMEM((cs, LANES), F32), pltpu.VMEM((cs, LANES), F32),
                        pltpu.VMEM((N_HEADS, D_HEAD, 2 * D_HEAD), F32),
                        pltpu.VMEM((N_HEADS, 1, 1), F32)],
        compiler_params=pltpu.CompilerParams(dimension_semantics=("arbitrary", "arbitrary"),
                                             vmem_limit_bytes=VMEM_LIMIT),
        name="mlstm",
    )(big3, big3, big3, small3, conv_w, prm, gnorm)


N_SEG = D_MODEL // LANES
assert N_SEG == SUBLANES


def _store_token_tiles(ref, val, n):
    for s in range(N_SEG):
        ref[pl.ds(s, n, stride=N_SEG), :] = val[:, s * LANES:(s + 1) * LANES]


def _load_token_tiles(ref, n):
    return jnp.concatenate([ref[pl.ds(s, n, stride=N_SEG), :] for s in range(N_SEG)], axis=1)


def _merge_kernel(oa_ref, ob_ref, gates_ref, x_ref, wa_ref, wb_ref, wo_ref, g_ref, wr_ref, br_ref,
                  x1_ref, h2_ref, idx_ref, gate_ref, rank_ref, cnt_ref, carry_s, *, tm):
    @pl.when(pl.program_id(0) == 0)
    def _():
        carry_s[...] = jnp.zeros(carry_s.shape, F32)

    y_a = jnp.dot(oa_ref[...], wa_ref[...], preferred_element_type=F32)
    y_b = jnp.dot(ob_ref[...], wb_ref[...], preferred_element_type=F32)
    y = _sigmoid(gates_ref[:, :D_MODEL]) * y_a + _sigmoid(gates_ref[:, D_MODEL:]) * y_b
    x1 = x_ref[...] + jnp.dot(y.astype(BF16), wo_ref[...], preferred_element_type=F32)
    x1_ref[...] = x1
    h2 = _rms(x1, g_ref[...])
    _store_token_tiles(h2_ref, h2, tm)
    logits = jnp.dot(h2.astype(BF16), wr_ref[...], preferred_element_type=F32) + br_ref[...]

    lane = lax.broadcasted_iota(I32, (tm, LANES), 1).astype(F32)
    vals, sels = [], []
    idx_t = jnp.zeros((tm, LANES), F32)
    work = logits
    for k in range(TOP_K):
        m = jnp.max(work, axis=-1, keepdims=True)
        i = jnp.min(jnp.where(work == m, lane, float(LANES)), axis=-1, keepdims=True)
        sel = lane == i
        work = jnp.where(sel, -jnp.inf, work)
        idx_t = jnp.where(lane == float(k), i, idx_t)
        vals.append(m)
        sels.append(sel)
    es = [jnp.exp(v - vals[0]) for v in vals]
    denom = es[0] + es[1] + es[2] + es[3]
    gate_t = jnp.zeros((tm, LANES), F32)
    for k in range(TOP_K):
        gate_t = jnp.where(lane == float(k), es[k] / denom, gate_t)
    idx_ref[...] = idx_t[:, :TOP_K].astype(I32)
    gate_ref[...] = gate_t[:, :TOP_K]

    onehot = jnp.zeros((tm, LANES), F32)
    for sel in sels:
        onehot = onehot + jnp.where(sel, 1.0, 0.0)
    r = lax.broadcasted_iota(I32, (tm, tm), 0)
    c = lax.broadcasted_iota(I32, (tm, tm), 1)
    before = jnp.where(r > c, 1.0, 0.0).astype(BF16)
    cum = jnp.dot(before, onehot.astype(BF16), preferred_element_type=F32) + carry_s[...]
    rank_t = jnp.zeros((tm, LANES), F32)
    for k, sel in enumerate(sels):
        rk = jnp.sum(jnp.where(sel, cum, 0.0), axis=-1, keepdims=True)
        rank_t = jnp.where(lane == float(k), rk, rank_t)
    rank_ref[...] = rank_t[:, :TOP_K].astype(I32)
    total = carry_s[...] + jnp.sum(onehot, axis=0, keepdims=True)
    carry_s[...] = total
    cnt_ref[...] = total.astype(I32)


def _merge(oa, ob, big, x2, wa, wb, wo, g, wr, br, tm):
    t = x2.shape[0]
    const = lambda i: (0, 0)
    return pl.pallas_call(
        functools.partial(_merge_kernel, tm=tm),
        grid=(t // tm,),
        in_specs=[pl.BlockSpec((tm, D_QK), lambda i: (i, 0)),
                  pl.BlockSpec((tm, D_QK), lambda i: (i, 0)),
                  pl.BlockSpec((tm, 2 * D_MODEL), lambda i: (i, 2)),
                  pl.BlockSpec((tm, D_MODEL), lambda i: (i, 0)),
                  pl.BlockSpec((D_QK, D_MODEL), const),
                  pl.BlockSpec((D_QK, D_MODEL), const),
                  pl.BlockSpec((D_MODEL, D_MODEL), const),
                  pl.BlockSpec((1, D_MODEL), const),
                  pl.BlockSpec((D_MODEL, LANES), const),
                  pl.BlockSpec((1, LANES), const)],
        out_specs=[pl.BlockSpec((tm, D_MODEL), lambda i: (i, 0)),
                   pl.BlockSpec((tm * N_SEG, LANES), lambda i: (i, 0)),
                   pl.BlockSpec((tm, TOP_K), lambda i: (i, 0)),
                   pl.BlockSpec((tm, TOP_K), lambda i: (i, 0)),
                   pl.BlockSpec((tm, TOP_K), lambda i: (i, 0)),
                   pl.BlockSpec((1, LANES), const)],
        out_shape=[jax.ShapeDtypeStruct((t, D_MODEL), F32),
                   jax.ShapeDtypeStruct((t * N_SEG, LANES), F32),
                   jax.ShapeDtypeStruct((t, TOP_K), I32),
                   jax.ShapeDtypeStruct((t, TOP_K), F32),
                   jax.ShapeDtypeStruct((t, TOP_K), I32),
                   jax.ShapeDtypeStruct((1, LANES), I32)],
        scratch_shapes=[pltpu.VMEM((1, LANES), F32)],
        compiler_params=pltpu.CompilerParams(dimension_semantics=("arbitrary",),
                                             vmem_limit_bytes=VMEM_LIMIT),
        name="merge_router",
    )(oa, ob, big, x2, wa, wb, wo, g, wr, br)


def _expert_kernel(te_ref, nu_ref, nv_ref, tok_ref, tok_next_ref, slot_ref, slot_prev_ref, h2_hbm,
                   wgu_ref, wd_ref, perm_ref, bg_ref, bu_ref, bd_ref, y_hbm,
                   xbuf0, xbuf1, ybuf0, ybuf1, wg_s, wu_s, wd_s, in_sem, out_sem, *, bm):
    i = pl.program_id(0)
    n_used = nu_ref[0]
    new_expert = (i == 0) | (te_ref[i] != te_ref[jnp.maximum(i - 1, 0)])
    xbufs = (xbuf0, xbuf1)
    ybufs = (ybuf0, ybuf1)

    def token_tile(ref, r):
        return ref.at[pl.ds(pl.multiple_of(r * N_SEG, N_SEG), N_SEG)]

    def gather_row(ids_ref, s, r):
        pltpu.make_async_copy(token_tile(h2_hbm, ids_ref[r]), token_tile(xbufs[s], r),
                              in_sem.at[s]).start()

    def wait_gather(s):
        pltpu.make_async_copy(h2_hbm.at[pl.ds(0, bm * N_SEG)], xbufs[s], in_sem.at[s]).wait()

    def wait_scatter(s, n_rows):
        n = pl.multiple_of(n_rows * N_SEG, N_SEG)
        pltpu.make_async_copy(ybufs[s].at[pl.ds(0, n)], y_hbm.at[pl.ds(0, n)], out_sem.at[s]).wait()

    @pl.when(i == 0)
    def _():
        def body(r, carry):
            gather_row(tok_ref, 0, r)
            return carry
        lax.fori_loop(0, bm, body, 0, unroll=DMA_UNROLL)

    n_prev = jnp.where(i >= 1, nv_ref[jnp.maximum(i - 1, 0)], 0)

    def scatter_row(ids_ref, s, r):
        pltpu.make_async_copy(token_tile(ybufs[s], r), token_tile(y_hbm, ids_ref[r]),
                              out_sem.at[s]).start()

    def scatter_loop(ids_ref, s, n_rows):
        def one(r, carry):
            scatter_row(ids_ref, s, r)
            return carry

        def group(j, carry):
            for q in range(DMA_UNROLL):
                scatter_row(ids_ref, s, j * DMA_UNROLL + q)
            return carry
        n_groups = n_rows // DMA_UNROLL
        lax.fori_loop(0, n_groups, group, 0)
        lax.fori_loop(n_groups * DMA_UNROLL, n_rows, one, 0)

    def tile_body(cur, prev_full):
        nxt = 1 - cur
        wait_gather(cur)

        @pl.when(i >= 2)
        def _():
            wait_scatter(cur, nv_ref[i - 2])

        if not prev_full:
            scatter_loop(slot_prev_ref, nxt, n_prev)

        @pl.when(new_expert)
        def _():
            half = PERM_BLOCK // 2
            for j in range(2 * D_EXPERT // PERM_BLOCK):
                blk = wgu_ref[0, :, j * PERM_BLOCK:(j + 1) * PERM_BLOCK].astype(BF16)
                split = jnp.dot(blk, perm_ref[...], preferred_element_type=F32)
                wg_s[:, j * half:(j + 1) * half] = split[:, :half].astype(BF16)
                wu_s[:, j * half:(j + 1) * half] = split[:, half:].astype(BF16)
            wd_s[...] = wd_ref[0].astype(BF16)

        n_grp = bm // DMA_GROUPS

        def dma_group(gi):
            for r in range(gi * n_grp, (gi + 1) * n_grp):
                gather_row(tok_next_ref, nxt, r)
                if prev_full:
                    scatter_row(slot_prev_ref, nxt, r)

        dma_group(0)
        x = _load_token_tiles(xbufs[cur], bm).astype(BF16)
        g = jnp.dot(x, wg_s[...], preferred_element_type=F32) + bg_ref[0]
        dma_group(1)
        u = jnp.dot(x, wu_s[...], preferred_element_type=F32) + bu_ref[0]
        dma_group(2)
        gate = jnp.minimum(g, SWIGLU_LIMIT)
        up = jnp.clip(u, -SWIGLU_LIMIT, SWIGLU_LIMIT)
        act = gate * _sigmoid(SWIGLU_ALPHA * gate) * (up + 1.0)
        dma_group(3)
        y = jnp.dot(act.astype(BF16), wd_s[...], preferred_element_type=F32) + bd_ref[0]
        _store_token_tiles(ybufs[cur], y, bm)

        @pl.when(i == n_used - 1)
        def _():
            scatter_loop(slot_ref, cur, nv_ref[i])
            wait_gather(nxt)
            wait_scatter(cur, nv_ref[i])

            @pl.when(i >= 1)
            def _():
                wait_scatter(nxt, nv_ref[i - 1])

    for parity in range(2):
        for prev_full in (False, True):
            full = (n_prev == bm) if prev_full else (n_prev != bm)
            cond = (i < n_used) & (lax.rem(i, 2) == parity) & full
            pl.when(cond)(functools.partial(tile_body, parity, prev_full))


def _split_permutation():
    half = PERM_BLOCK // 2
    src = jnp.arange(PERM_BLOCK)[:, None]
    dst = jnp.arange(PERM_BLOCK)[None, :]
    return (src == jnp.where(dst < half, 2 * dst, 2 * (dst - half) + 1)).astype(BF16)


def _experts(tile_expert, n_used, tile_valid, row_tok, row_slot, h2, wgu, wd, bg, bu, bd, bm):
    n_rows = row_tok.shape[0]
    n_tiles = n_rows // bm
    n_slots = h2.shape[0] // N_SEG * TOP_K
    cur = lambda i, te, nu, nv: (jnp.minimum(i, nu[0] - 1),)
    nxt = lambda i, te, nu, nv: (jnp.minimum(i + 1, nu[0] - 1),)
    prv = lambda i, te, nu, nv: (jnp.clip(i - 1, 0, nu[0] - 1),)
    w_map = lambda i, te, nu, nv: (te[i], 0, 0)
    grid_spec = pltpu.PrefetchScalarGridSpec(
        num_scalar_prefetch=3,
        grid=(n_tiles,),
        in_specs=[pl.BlockSpec((bm,), cur, memory_space=pltpu.SMEM),
                  pl.BlockSpec((bm,), nxt, memory_space=pltpu.SMEM),
                  pl.BlockSpec((bm,), cur, memory_space=pltpu.SMEM),
                  pl.BlockSpec((bm,), prv, memory_space=pltpu.SMEM),
                  pl.BlockSpec(memory_space=pl.ANY),
                  pl.BlockSpec((1, D_MODEL, 2 * D_EXPERT), w_map),
                  pl.BlockSpec((1, D_EXPERT, D_MODEL), w_map),
                  pl.BlockSpec((PERM_BLOCK, PERM_BLOCK), lambda i, te, nu, nv: (0, 0)),
                  pl.BlockSpec((1, 1, D_EXPERT), w_map),
                  pl.BlockSpec((1, 1, D_EXPERT), w_map),
                  pl.BlockSpec((1, 1, D_MODEL), w_map)],
        out_specs=pl.BlockSpec(memory_space=pl.ANY),
        scratch_shapes=[pltpu.VMEM((bm * N_SEG, LANES), F32), pltpu.VMEM((bm * N_SEG, LANES), F32),
                        pltpu.VMEM((bm * N_SEG, LANES), F32), pltpu.VMEM((bm * N_SEG, LANES), F32),
                        pltpu.VMEM((D_MODEL, D_EXPERT), BF16), pltpu.VMEM((D_MODEL, D_EXPERT), BF16),
                        pltpu.VMEM((D_EXPERT, D_MODEL), BF16),
                        pltpu.SemaphoreType.DMA((2,)), pltpu.SemaphoreType.DMA((2,))],
    )
    return pl.pallas_call(
        functools.partial(_expert_kernel, bm=bm),
        grid_spec=grid_spec,
        out_shape=jax.ShapeDtypeStruct((n_slots * N_SEG, LANES), F32),
        compiler_params=pltpu.CompilerParams(dimension_semantics=("arbitrary",),
                                             vmem_limit_bytes=VMEM_LIMIT,
                                             has_side_effects=True),
        name="experts",
    )(tile_expert, n_used, tile_valid, row_tok, row_tok, row_slot, row_slot, h2, wgu, wd, _split_permutation(),
      bg, bu, bd)


def _combine_kernel(y0_ref, y1_ref, y2_ref, y3_ref, gate_ref, x1_ref, g_ref, o_ref, *, tm):
    acc = x1_ref[...]
    for k, y_ref in enumerate((y0_ref, y1_ref, y2_ref, y3_ref)):
        acc = acc + gate_ref[:, k:k + 1] * _load_token_tiles(y_ref, tm)
    o_ref[...] = _rms(acc, g_ref[...])


def _combine(y_slots, gates, x1, g, tm):
    t = x1.shape[0]
    nb = t // tm

    def slot_spec(k):
        return pl.BlockSpec((tm * N_SEG, LANES), lambda i: (k * nb + i, 0))

    return pl.pallas_call(
        functools.partial(_combine_kernel, tm=tm),
        grid=(nb,),
        in_specs=[slot_spec(0), slot_spec(1), slot_spec(2), slot_spec(3),
                  pl.BlockSpec((tm, TOP_K), lambda i: (i, 0)),
                  pl.BlockSpec((tm, D_MODEL), lambda i: (i, 0)),
                  pl.BlockSpec((1, D_MODEL), lambda i: (0, 0))],
        out_specs=pl.BlockSpec((tm, D_MODEL), lambda i: (i, 0)),
        out_shape=jax.ShapeDtypeStruct((t, D_MODEL), F32),
        compiler_params=pltpu.CompilerParams(dimension_semantics=("arbitrary",),
                                             vmem_limit_bytes=VMEM_LIMIT),
        name="combine",
    )(y_slots, y_slots, y_slots, y_slots, gates, x1, g)


def _lane_row(vec, lane0):
    n = vec.shape[0]
    return jnp.zeros((8, LANES), F32).at[0, lane0:lane0 + n].set(vec.astype(F32))


def _tile_size(n, pref):
    return pref if n % pref == 0 else n


def kernel(x, norm_mix, w_in, gdn_conv, gdn_a_log, gdn_dt_bias, gdn_norm, ml_conv, ml_b_i, ml_b_f,
           ml_norm, w_up_gdn, w_up_ml, w_out, norm_ffn, w_router, b_router, w_gate_up, b_gate_up,
           w_down, b_down, norm_final):
    assert norm_mix.shape[0] == 1, "single-layer stack"
    b, s, d = x.shape
    assert d == D_MODEL and s % CHUNK == 0
    t = b * s
    x2 = x.reshape(t, d)

    w = w_in[0]
    w_big = jnp.concatenate([_cols(w, n) for n in ("g_q", "g_k", "g_v", "g_z", "m_q", "m_k", "m_v",
                                                   "m_o", "gate_gdn", "gate_ml")], axis=1).astype(BF16)
    w_small = jnp.concatenate([_cols(w, n) for n in ("g_a", "g_b", "m_i", "m_f")], axis=1)
    w_small = jnp.pad(w_small, ((0, 0), (0, LANES - N_SMALL))).astype(BF16)

    tm = _tile_size(t, 512)
    big, small = _in_proj(x2, norm_mix[0][None, :], w_big, w_small, tm)
    big3 = big.reshape(b, s, N_BIG)
    small3 = small.reshape(b, s, LANES)

    cs = _tile_size(s, 512)
    gdn_prm = _lane_row(gdn_a_log[0], 0).at[1, 0:N_HEADS].set(gdn_dt_bias[0].astype(F32))
    oa = _gdn(big3, small3, gdn_conv[0].astype(F32), gdn_prm, gdn_norm[0][None, :].astype(F32), cs)
    ml_prm = _lane_row(ml_b_i[0], 2 * N_HEADS).at[0, 3 * N_HEADS:4 * N_HEADS].set(ml_b_f[0].astype(F32))
    ob = _mlstm(big3, small3, ml_conv[0].astype(F32), ml_prm, ml_norm[0][None, :].astype(F32), cs)

    w_r = jnp.pad(w_router[0], ((0, 0), (0, LANES - N_EXPERTS))).astype(BF16)
    b_r = jnp.full((1, LANES), -1e30, F32).at[0, :N_EXPERTS].set(b_router[0].astype(F32))
    x1, h2, idx, gates, rank, counts = _merge(
        oa.reshape(t, D_QK), ob.reshape(t, D_QK), big, x2, w_up_gdn[0].astype(BF16),
        w_up_ml[0].astype(BF16), w_out[0].astype(BF16), norm_ffn[0][None, :], w_r, b_r, tm)

    bm = EXPERT_TILE
    n_assign = t * TOP_K
    n_tiles = -(-n_assign // bm) + N_EXPERTS
    counts = counts[0, :N_EXPERTS]
    padded = (counts + bm - 1) // bm * bm
    pend = jnp.cumsum(padded)
    pstart = pend - padded
    dest = (pstart[idx] + rank).reshape(-1)
    n_used = (pend[-1] // bm).astype(I32)
    tile_ids = jnp.minimum(jnp.arange(n_tiles, dtype=I32), n_used - 1)
    tile_expert = jnp.minimum(jnp.sum((pend[None, :] <= (tile_ids * bm)[:, None]).astype(I32), axis=1),
                              N_EXPERTS - 1)
    tile_valid = jnp.clip(counts[tile_expert] - (tile_ids * bm - pstart[tile_expert]), 0, bm).astype(I32)
    assign = jnp.arange(n_assign, dtype=I32)
    row_slot = jnp.zeros((n_tiles * bm,), I32).at[dest].set((assign % TOP_K) * t + assign // TOP_K,
                                                            unique_indices=True)
    row_tok = row_slot % t

    bgu = b_gate_up[0]
    y_slots = _experts(tile_expert.astype(I32), n_used.reshape(1), tile_valid, row_tok, row_slot, h2,
                       w_gate_up[0], w_down[0],
                       bgu[:, None, 0::2].astype(F32), bgu[:, None, 1::2].astype(F32),
                       b_down[0][:, None, :].astype(F32), bm)
    out = _combine(y_slots, gates, x1, norm_final[None, :], _tile_size(t, 256))
    return out.reshape(b, s, d)
```

```python
import functools

import jax
import jax.numpy as jnp
from jax import lax
from jax.experimental import pallas as pl
from jax.experimental.pallas import tpu as pltpu

F32 = jnp.float32
BF16 = jnp.bfloat16
I32 = jnp.int32

D_MODEL = 1024
N_HEADS = 4
D_HEAD = 128
D_QK = N_HEADS * D_HEAD
CONV_WIDTH = 4
CHUNK = 64
GATE_SOFTCAP = 15.0
N_EXPERTS = 32
TOP_K = 4
D_EXPERT = 1024
SWIGLU_LIMIT = 7.0
SWIGLU_ALPHA = 1.702
NORM_EPS = 1e-6

LANES = 128
SUBLANES = 8
N_SMALL = 16
N_BIG = 6 * D_MODEL
VMEM_LIMIT = 56 * 1024 * 1024
EXPERT_TILE = 512
DMA_UNROLL = 8
DMA_GROUPS = 4
PERM_BLOCK = 256
LOCAL_CHUNKS = 2

_OFF = {}
_o = 0
for _name, _w in (("g_q", D_QK), ("g_k", D_QK), ("g_v", D_QK), ("g_z", D_QK), ("g_a", N_HEADS),
                  ("g_b", N_HEADS), ("m_q", D_QK), ("m_k", D_QK), ("m_v", D_QK), ("m_o", D_QK),
                  ("m_i", N_HEADS), ("m_f", N_HEADS), ("gate_gdn", D_MODEL), ("gate_ml", D_MODEL)):
    _OFF[_name] = (_o, _w)
    _o += _w


def _cols(w, name):
    o, n = _OFF[name]
    return w[:, o:o + n]


_NN = (((1,), (0,)), ((), ()))
_NT = (((1,), (1,)), ((), ()))
_TN = (((0,), (0,)), ((), ()))


def _bdot(a, b, dims=_NN):
    return lax.dot_general(a.astype(BF16), b.astype(BF16), dims, preferred_element_type=F32)


def _split3(a):
    hi = a.astype(BF16)
    r = a - hi.astype(F32)
    mid = r.astype(BF16)
    return hi, mid, (r - mid.astype(F32)).astype(BF16)


def _select_dot(sel01, x, dims=_NN):
    s = sel01.astype(BF16)
    h, m, l = _split3(x)
    d = functools.partial(lax.dot_general, dimension_numbers=dims, preferred_element_type=F32)
    return d(s, h) + (d(s, m) + d(s, l))


def _sigmoid(x):
    return 1.0 / (1.0 + jnp.exp(-x))


def _silu(x):
    return x * _sigmoid(x)


def _softplus(x):
    return jnp.maximum(x, 0.0) + jnp.log(1.0 + jnp.exp(-jnp.abs(x)))


def _rms(x, g):
    return x * lax.rsqrt(jnp.mean(x * x, axis=-1, keepdims=True) + NORM_EPS) * g


def _in_proj_kernel(x_ref, g_ref, wbig_ref, wsmall_ref, big_ref, small_ref, *, n_chunk):
    h = _rms(x_ref[...], g_ref[...]).astype(BF16)
    for c in range(N_BIG // n_chunk):
        sl = slice(c * n_chunk, (c + 1) * n_chunk)
        big_ref[:, sl] = jnp.dot(h, wbig_ref[:, sl], preferred_element_type=F32)
    small_ref[...] = jnp.dot(h, wsmall_ref[...], preferred_element_type=F32)


def _in_proj(x2, g, w_big, w_small, tm):
    t = x2.shape[0]
    return pl.pallas_call(
        functools.partial(_in_proj_kernel, n_chunk=1024),
        grid=(t // tm,),
        in_specs=[pl.BlockSpec((tm, D_MODEL), lambda i: (i, 0)),
                  pl.BlockSpec((1, D_MODEL), lambda i: (0, 0)),
                  pl.BlockSpec((D_MODEL, N_BIG), lambda i: (0, 0)),
                  pl.BlockSpec((D_MODEL, LANES), lambda i: (0, 0))],
        out_specs=[pl.BlockSpec((tm, N_BIG), lambda i: (i, 0)),
                   pl.BlockSpec((tm, LANES), lambda i: (i, 0))],
        out_shape=[jax.ShapeDtypeStruct((t, N_BIG), F32),
                   jax.ShapeDtypeStruct((t, LANES), F32)],
        compiler_params=pltpu.CompilerParams(dimension_semantics=("arbitrary",),
                                             vmem_limit_bytes=VMEM_LIMIT),
        name="in_proj",
    )(x2, g, w_big, w_small)


def _chunk_masks():
    r = lax.broadcasted_iota(I32, (CHUNK, CHUNK), 0)
    c = lax.broadcasted_iota(I32, (CHUNK, CHUNK), 1)
    return r > c, r >= c


def _block_cumsum_matrix(cs):
    r = lax.broadcasted_iota(I32, (cs, cs), 0)
    c = lax.broadcasted_iota(I32, (cs, cs), 1)
    same = (r // CHUNK) == (c // CHUNK)
    return jnp.where(same & (r >= c), 1.0, 0.0).astype(F32)


def _lane_onehot(lane):
    return jnp.where(lax.broadcasted_iota(I32, (CHUNK, LANES), 1) == lane, 1.0, 0.0).astype(F32)


def _conv_silu_pieces(x_ref, w_ref, buf_ref, cs, width):
    @pl.when(pl.program_id(1) == 0)
    def _():
        buf_ref[0:8, :] = jnp.zeros((8, buf_ref.shape[1]), F32)

    buf_ref[8:8 + cs, :] = x_ref[0]
    for r0 in range(0, cs, CHUNK):
        for c0 in range(0, buf_ref.shape[1], width):
            cols = slice(c0, c0 + width)
            acc = w_ref[CONV_WIDTH - 1:CONV_WIDTH, cols] * buf_ref[8 + r0:8 + r0 + CHUNK, cols]
            for j in range(CONV_WIDTH - 1):
                s = 8 - (CONV_WIDTH - 1) + j + r0
                acc = acc + w_ref[j:j + 1, cols] * buf_ref[s:s + CHUNK, cols]
            yield r0, c0, _silu(acc)
    buf_ref[0:8, :] = buf_ref[cs:cs + 8, :]


def _gdn_kernel(qkv_ref, z_ref, sm_ref, conv_ref, prm_ref, gn_ref, o_ref,
                buf_ref, q_s, k_s, v_s, w_s, attn_s, gam_s, beta_s, state_s, *, cs):
    n_chunks = cs // CHUNK

    @pl.when(pl.program_id(1) == 0)
    def _():
        state_s[...] = jnp.zeros(state_s.shape, F32)

    for r0, c0, piece in _conv_silu_pieces(qkv_ref, conv_ref, buf_ref, cs, D_QK):
        rows = slice(r0, r0 + CHUNK)
        if c0 == 2 * D_QK:
            v_s[rows, :] = piece
            continue
        dst, scale = (q_s, D_HEAD ** -0.5) if c0 == 0 else (k_s, 1.0)
        for h in range(N_HEADS):
            sl = slice(h * D_HEAD, (h + 1) * D_HEAD)
            a = piece[:, sl]
            dst[rows, sl] = a * (lax.rsqrt(jnp.sum(a * a, axis=-1, keepdims=True) + NORM_EPS) * scale)

    sm = sm_ref[0]
    logdec = -jnp.exp(prm_ref[0:1, :]) * _softplus(sm + prm_ref[1:2, :])
    gam_s[...] = _select_dot(_block_cumsum_matrix(cs), logdec)
    beta_s[...] = _sigmoid(sm)

    strict, incl = _chunk_masks()
    eye = jnp.where(lax.broadcasted_iota(I32, (CHUNK, CHUNK), 0)
                    == lax.broadcasted_iota(I32, (CHUNK, CHUNK), 1), 1.0, 0.0).astype(F32)
    gnorm = gn_ref[...]

    heads = range(N_HEADS)
    hsl = [slice(h * D_HEAD, (h + 1) * D_HEAD) for h in heads]
    asl = [slice(h * CHUNK, (h + 1) * CHUNK) for h in heads]
    onehots = [_lane_onehot(h) for h in heads]

    def local_body(it, carry):
        rows = [pl.ds(pl.multiple_of((it * LOCAL_CHUNKS + ci) * CHUNK, CHUNK), CHUNK)
                for ci in range(LOCAL_CHUNKS)]
        probs = [(ci, h) for ci in range(LOCAL_CHUNKS) for h in heads]
        gam_all = [gam_s[r, :] for r in rows]
        beta_all = [beta_s[r, :] for r in rows]
        q = [q_s[rows[ci], hsl[h]] for ci, h in probs]
        k = [k_s[rows[ci], hsl[h]] for ci, h in probs]
        v = [v_s[rows[ci], hsl[h]] for ci, h in probs]
        gc = [gam_all[ci][:, h:h + 1] for ci, h in probs]
        beta = [beta_all[ci][:, N_HEADS + h:N_HEADS + h + 1] for ci, h in probs]
        g_row = [_select_dot(onehots[h], gam_all[ci], _NT) for ci, h in probs]
        kb = [a * b for a, b in zip(k, beta)]
        kk = [_bdot(a, b, _NT) for a, b in zip(kb, k)]
        qk = [_bdot(a, b, _NT) for a, b in zip(q, k)]
        decay = [jnp.where(incl, jnp.exp(jnp.minimum(a - b, 0.0)), 0.0) for a, b in zip(gc, g_row)]
        lower = [jnp.where(strict, a * b, 0.0) for a, b in zip(kk, decay)]
        inv = [eye - a for a in lower]
        pw = [_bdot(a, a) for a in lower]
        for lvl in range(5):
            upd = [_bdot(a, b) for a, b in zip(inv, pw)]
            if lvl < 4:
                pw = [_bdot(a, a) for a in pw]
            inv = [a + b for a, b in zip(inv, upd)]
        e_gc = [jnp.exp(a) for a in gc]
        rhs = [jnp.concatenate([a * b, c * d], axis=1) for a, b, c, d in zip(v, beta, kb, e_gc)]
        sol = [_bdot(a, b) for a, b in zip(inv, rhs)]
        for p, (ci, h) in enumerate(probs):
            v_s[rows[ci], hsl[h]] = sol[p][:, :D_HEAD]
            w_s[rows[ci], hsl[h]] = sol[p][:, D_HEAD:]
            attn_s[rows[ci], asl[h]] = qk[p] * decay[p]
            q_s[rows[ci], hsl[h]] = q[p] * e_gc[p]
            k_s[rows[ci], hsl[h]] = k[p] * jnp.exp(gc[p][CHUNK - 1:CHUNK, :] - gc[p])
        return carry

    lax.fori_loop(0, n_chunks // LOCAL_CHUNKS, local_body, 0)

    def state_body(c, carry):
        rows = pl.ds(pl.multiple_of(c * CHUNK, CHUNK), CHUNK)
        g_last = gam_s[pl.ds(c * CHUNK + CHUNK - 1, 1), :]
        st = [state_s[h] for h in heads]
        ws = [_bdot(w_s[rows, hsl[h]], st[h]) for h in heads]
        qs = [_bdot(q_s[rows, hsl[h]], st[h]) for h in heads]
        u = [v_s[rows, hsl[h]] - ws[h] for h in heads]
        au = [_bdot(attn_s[rows, asl[h]], u[h]) for h in heads]
        ku = [_bdot(k_s[rows, hsl[h]], u[h], _TN) for h in heads]
        for h in heads:
            state_s[h] = st[h] * jnp.exp(g_last[:, h:h + 1]) + ku[h]
            o = qs[h] + au[h]
            o = o * lax.rsqrt(jnp.mean(o * o, axis=-1, keepdims=True) + NORM_EPS) * gnorm
            o_ref[0, rows, hsl[h]] = (o * _silu(z_ref[0, rows, hsl[h]])).astype(o_ref.dtype)
        return carry

    lax.fori_loop(0, n_chunks, state_body, 0)


def _gdn(big3, small3, conv_w, prm, gnorm, cs):
    b, s, _ = big3.shape
    return pl.pallas_call(
        functools.partial(_gdn_kernel, cs=cs),
        grid=(b, s // cs),
        in_specs=[pl.BlockSpec((1, cs, 3 * D_QK), lambda i, j: (i, j, 0)),
                  pl.BlockSpec((1, cs, D_QK), lambda i, j: (i, j, 3)),
                  pl.BlockSpec((1, cs, LANES), lambda i, j: (i, j, 0)),
                  pl.BlockSpec((CONV_WIDTH, 3 * D_QK), lambda i, j: (0, 0)),
                  pl.BlockSpec((8, LANES), lambda i, j: (0, 0)),
                  pl.BlockSpec((1, D_HEAD), lambda i, j: (0, 0))],
        out_specs=pl.BlockSpec((1, cs, D_QK), lambda i, j: (i, j, 0)),
        out_shape=jax.ShapeDtypeStruct((b, s, D_QK), BF16),
        scratch_shapes=[pltpu.VMEM((cs + 8, 3 * D_QK), F32),
                        pltpu.VMEM((cs, D_QK), F32), pltpu.VMEM((cs, D_QK), F32),
                        pltpu.VMEM((cs, D_QK), F32), pltpu.VMEM((cs, D_QK), F32),
                        pltpu.VMEM((cs, N_HEADS * CHUNK), F32),
                        pltpu.VMEM((cs, LANES), F32), pltpu.VMEM((cs, LANES), F32),
                        pltpu.VMEM((N_HEADS, D_HEAD, D_HEAD), F32)],
        compiler_params=pltpu.CompilerParams(dimension_semantics=("arbitrary", "arbitrary"),
                                             vmem_limit_bytes=VMEM_LIMIT),
        name="gdn",
    )(big3, big3, small3, conv_w, prm, gnorm)


def _mlstm_kernel(qk_ref, v_ref, og_ref, sm_ref, conv_ref, prm_ref, gn_ref, o_ref,
                  buf_ref, q_s, k_s, bcum_s, ipre_s, state_s, m_s, *, cs):
    n_chunks = cs // CHUNK

    @pl.when(pl.program_id(1) == 0)
    def _():
        state_s[...] = jnp.zeros(state_s.shape, F32)
        m_s[...] = jnp.zeros(m_s.shape, F32)

    for r0, c0, piece in _conv_silu_pieces(qk_ref, conv_ref, buf_ref, cs, D_QK):
        if c0 == 0:
            q_s[r0:r0 + CHUNK, :] = piece
        else:
            k_s[r0:r0 + CHUNK, :] = piece * (D_HEAD ** -0.5)

    pre = sm_ref[0] + prm_ref[0:1, :]
    capped = GATE_SOFTCAP * jnp.tanh(pre / GATE_SOFTCAP)
    logf = -_softplus(-capped)
    bcum_s[...] = _select_dot(_block_cumsum_matrix(cs), logf)
    ipre_s[...] = pltpu.roll(capped, N_HEADS, axis=1)

    _, incl = _chunk_masks()
    ones_aug = jnp.ones((CHUNK, D_HEAD), F32)

    heads = range(N_HEADS)
    hsl = [slice(h * D_HEAD, (h + 1) * D_HEAD) for h in heads]
    lanes = [3 * N_HEADS + h for h in heads]
    onehots = [_lane_onehot(lane) for lane in lanes]

    def chunk_body(c, carry):
        rows = pl.ds(pl.multiple_of(c * CHUNK, CHUNK), CHUNK)
        b_all = bcum_s[rows, :]
        comb_all = ipre_s[rows, :] - b_all
        q = [q_s[rows, hsl[h]] for h in heads]
        k = [k_s[rows, hsl[h]] for h in heads]
        v_aug = [jnp.concatenate([v_ref[0, rows, hsl[h]], ones_aug], axis=1) for h in heads]
        st = [state_s[h] for h in heads]
        m_st = [m_s[h] for h in heads]
        bc = [b_all[:, lane:lane + 1] for lane in lanes]
        comb_row = [_select_dot(onehots[h], comb_all, _NT) for h in heads]
        qk = [_bdot(q[h], k[h], _NT) for h in heads]
        inter = [_bdot(q[h], st[h]) for h in heads]
        b_last = [bc[h][CHUNK - 1:CHUNK, :] for h in heads]
        a_log = [b_last[h] + comb_all[:, lanes[h]:lanes[h] + 1] for h in heads]
        m_chunk = [jnp.max(a_log[h], axis=0, keepdims=True) for h in heads]
        wk = [k[h] * jnp.exp(a_log[h] - m_chunk[h]) for h in heads]
        d_state = [_bdot(wk[h], v_aug[h], _TN) for h in heads]
        d_log = [jnp.where(incl, bc[h] + comb_row[h], -jnp.inf) for h in heads]
        m_loc = [jnp.max(d_log[h], axis=-1, keepdims=True) for h in heads]
        p = [jnp.exp(d_log[h] - m_loc[h]) * qk[h] for h in heads]
        loc = [_bdot(p[h], v_aug[h]) for h in heads]
        for h in heads:
            m_inter = bc[h] + m_st[h]
            m_t = jnp.maximum(m_loc[h], m_inter)
            tot = jnp.exp(m_inter - m_t) * inter[h] + jnp.exp(m_loc[h] - m_t) * loc[h]
            hh = tot[:, :D_HEAD] / jnp.maximum(jnp.abs(tot[:, D_HEAD:]), jnp.exp(-m_t))
            m_new = jnp.maximum(b_last[h] + m_st[h], m_chunk[h])
            state_s[h] = (jnp.exp(b_last[h] + m_st[h] - m_new) * st[h]
                          + jnp.exp(m_chunk[h] - m_new) * d_state[h])
            m_s[h] = m_new
            hh = hh * lax.rsqrt(jnp.mean(hh * hh, axis=-1, keepdims=True) + NORM_EPS) * gn_ref[:, hsl[h]]
            o_ref[0, rows, hsl[h]] = (hh * _sigmoid(og_ref[0, rows, hsl[h]])).astype(o_ref.dtype)
        return carry

    lax.fori_loop(0, n_chunks, chunk_body, 0)


def _mlstm(big3, small3, conv_w, prm, gnorm, cs):
    b, s, _ = big3.shape
    return pl.pallas_call(
        functools.partial(_mlstm_kernel, cs=cs),
        grid=(b, s // cs),
        in_specs=[pl.BlockSpec((1, cs, 2 * D_QK), lambda i, j: (i, j, 2)),
                  pl.BlockSpec((1, cs, D_QK), lambda i, j: (i, j, 6)),
                  pl.BlockSpec((1, cs, D_QK), lambda i, j: (i, j, 7)),
                  pl.BlockSpec((1, cs, LANES), lambda i, j: (i, j, 0)),
                  pl.BlockSpec((CONV_WIDTH, 2 * D_QK), lambda i, j: (0, 0)),
                  pl.BlockSpec((8, LANES), lambda i, j: (0, 0)),
                  pl.BlockSpec((1, D_QK), lambda i, j: (0, 0))],
        out_specs=pl.BlockSpec((1, cs, D_QK), lambda i, j: (i, j, 0)),
        out_shape=jax.ShapeDtypeStruct((b, s, D_QK), BF16),
        scratch_shapes=[pltpu.VMEM((cs + 8, 2 * D_QK), F32),
                        pltpu.VMEM((cs, D_QK), F32), pltpu.VMEM((cs, D_QK), F32),
                        pltpu.VMEM((cs, LANES), F32), pltpu.VMEM((cs, LANES), F32),
                        pltpu.VMEM((N_HEADS, D_HEAD, 2 * D_HEAD), F32),
                        pltpu.VMEM((N_HEADS, 1, 1), F32)],
        compiler_params=pltpu.CompilerParams(dimension_semantics=("arbitrary", "arbitrary"),
                                             vmem_limit_bytes=VMEM_LIMIT),
        name="mlstm",
    )(big3, big3, big3, small3, conv_w, prm, gnorm)


N_SEG = D_MODEL // LANES
assert N_SEG == SUBLANES


def _store_token_tiles(ref, val, n):
    for s in range(N_SEG):
        ref[pl.ds(s, n, stride=N_SEG), :] = val[:, s * LANES:(s + 1) * LANES]


def _load_token_tiles(ref, n):
    return jnp.concatenate([ref[pl.ds(s, n, stride=N_SEG), :] for s in range(N_SEG)], axis=1)


def _merge_kernel(oa_ref, ob_ref, gates_ref, x_ref, wa_ref, wb_ref, wo_ref, g_ref, wr_ref, br_ref,
                  x1_ref, h2_ref, idx_ref, gate_ref, rank_ref, cnt_ref, carry_s, *, tm):
    @pl.when(pl.program_id(0) == 0)
    def _():
        carry_s[...] = jnp.zeros(carry_s.shape, F32)

    y_a = jnp.dot(oa_ref[...], wa_ref[...], preferred_element_type=F32)
    y_b = jnp.dot(ob_ref[...], wb_ref[...], preferred_element_type=F32)
    y = _sigmoid(gates_ref[:, :D_MODEL]) * y_a + _sigmoid(gates_ref[:, D_MODEL:]) * y_b
    x1 = x_ref[...] + jnp.dot(y.astype(BF16), wo_ref[...], preferred_element_type=F32)
    x1_ref[...] = x1
    h2 = _rms(x1, g_ref[...])
    _store_token_tiles(h2_ref, h2, tm)
    logits = jnp.dot(h2.astype(BF16), wr_ref[...], preferred_element_type=F32) + br_ref[...]

    lane = lax.broadcasted_iota(I32, (tm, LANES), 1).astype(F32)
    vals, sels = [], []
    idx_t = jnp.zeros((tm, LANES), F32)
    work = logits
    for k in range(TOP_K):
        m = jnp.max(work, axis=-1, keepdims=True)
        i = jnp.min(jnp.where(work == m, lane, float(LANES)), axis=-1, keepdims=True)
        sel = lane == i
        work = jnp.where(sel, -jnp.inf, work)
        idx_t = jnp.where(lane == float(k), i, idx_t)
        vals.append(m)
        sels.append(sel)
    es = [jnp.exp(v - vals[0]) for v in vals]
    denom = es[0] + es[1] + es[2] + es[3]
    gate_t = jnp.zeros((tm, LANES), F32)
    for k in range(TOP_K):
        gate_t = jnp.where(lane == float(k), es[k] / denom, gate_t)
    idx_ref[...] = idx_t[:, :TOP_K].astype(I32)
    gate_ref[...] = gate_t[:, :TOP_K]

    onehot = jnp.zeros((tm, LANES), F32)
    for sel in sels:
        onehot = onehot + jnp.where(sel, 1.0, 0.0)
    r = lax.broadcasted_iota(I32, (tm, tm), 0)
    c = lax.broadcasted_iota(I32, (tm, tm), 1)
    before = jnp.where(r > c, 1.0, 0.0).astype(BF16)
    cum = jnp.dot(before, onehot.astype(BF16), preferred_element_type=F32) + carry_s[...]
    rank_t = jnp.zeros((tm, LANES), F32)
    for k, sel in enumerate(sels):
        rk = jnp.sum(jnp.where(sel, cum, 0.0), axis=-1, keepdims=True)
        rank_t = jnp.where(lane == float(k), rk, rank_t)
    rank_ref[...] = rank_t[:, :TOP_K].astype(I32)
    total = carry_s[...] + jnp.sum(onehot, axis=0, keepdims=True)
    carry_s[...] = total
    cnt_ref[...] = total.astype(I32)


def _merge(oa, ob, big, x2, wa, wb, wo, g, wr, br, tm):
    t = x2.shape[0]
    const = lambda i: (0, 0)
    return pl.pallas_call(
        functools.partial(_merge_kernel, tm=tm),
        grid=(t // tm,),
        in_specs=[pl.BlockSpec((tm, D_QK), lambda i: (i, 0)),
                  pl.BlockSpec((tm, D_QK), lambda i: (i, 0)),
                  pl.BlockSpec((tm, 2 * D_MODEL), lambda i: (i, 2)),
                  pl.BlockSpec((tm, D_MODEL), lambda i: (i, 0)),
                  pl.BlockSpec((D_QK, D_MODEL), const),
                  pl.BlockSpec((D_QK, D_MODEL), const),
                  pl.BlockSpec((D_MODEL, D_MODEL), const),
                  pl.BlockSpec((1, D_MODEL), const),
                  pl.BlockSpec((D_MODEL, LANES), const),
                  pl.BlockSpec((1, LANES), const)],
        out_specs=[pl.BlockSpec((tm, D_MODEL), lambda i: (i, 0)),
                   pl.BlockSpec((tm * N_SEG, LANES), lambda i: (i, 0)),
                   pl.BlockSpec((tm, TOP_K), lambda i: (i, 0)),
                   pl.BlockSpec((tm, TOP_K), lambda i: (i, 0)),
                   pl.BlockSpec((tm, TOP_K), lambda i: (i, 0)),
                   pl.BlockSpec((1, LANES), const)],
        out_shape=[jax.ShapeDtypeStruct((t, D_MODEL), F32),
                   jax.ShapeDtypeStruct((t * N_SEG, LANES), F32),
                   jax.ShapeDtypeStruct((t, TOP_K), I32),
                   jax.ShapeDtypeStruct((t, TOP_K), F32),
                   jax.ShapeDtypeStruct((t, TOP_K), I32),
                   jax.ShapeDtypeStruct((1, LANES), I32)],
        scratch_shapes=[pltpu.VMEM((1, LANES), F32)],
        compiler_params=pltpu.CompilerParams(dimension_semantics=("arbitrary",),
                                             vmem_limit_bytes=VMEM_LIMIT),
        name="merge_router",
    )(oa, ob, big, x2, wa, wb, wo, g, wr, br)


def _expert_kernel(te_ref, nu_ref, nv_ref, tok_ref, tok_next_ref, slot_ref, slot_prev_ref, h2_hbm,
                   wgu_ref, wd_ref, perm_ref, bg_ref, bu_ref, bd_ref, y_hbm,
                   xbuf0, xbuf1, ybuf0, ybuf1, wg_s, wu_s, wd_s, in_sem, out_sem, *, bm):
    i = pl.program_id(0)
    n_used = nu_ref[0]
    new_expert = (i == 0) | (te_ref[i] != te_ref[jnp.maximum(i - 1, 0)])
    xbufs = (xbuf0, xbuf1)
    ybufs = (ybuf0, ybuf1)

    def token_tile(ref, r):
        return ref.at[pl.ds(pl.multiple_of(r * N_SEG, N_SEG), N_SEG)]

    def gather_row(ids_ref, s, r):
        pltpu.make_async_copy(token_tile(h2_hbm, ids_ref[r]), token_tile(xbufs[s], r),
                              in_sem.at[s]).start()

    def wait_gather(s):
        pltpu.make_async_copy(h2_hbm.at[pl.ds(0, bm * N_SEG)], xbufs[s], in_sem.at[s]).wait()

    def wait_scatter(s, n_rows):
        n = pl.multiple_of(n_rows * N_SEG, N_SEG)
        pltpu.make_async_copy(ybufs[s].at[pl.ds(0, n)], y_hbm.at[pl.ds(0, n)], out_sem.at[s]).wait()

    @pl.when(i == 0)
    def _():
        def body(r, carry):
            gather_row(tok_ref, 0, r)
            return carry
        lax.fori_loop(0, bm, body, 0, unroll=DMA_UNROLL)

    n_prev = jnp.where(i >= 1, nv_ref[jnp.maximum(i - 1, 0)], 0)

    def scatter_row(ids_ref, s, r):
        pltpu.make_async_copy(token_tile(ybufs[s], r), token_tile(y_hbm, ids_ref[r]),
                              out_sem.at[s]).start()

    def scatter_loop(ids_ref, s, n_rows):
        def one(r, carry):
            scatter_row(ids_ref, s, r)
            return carry

        def group(j, carry):
            for q in range(DMA_UNROLL):
                scatter_row(ids_ref, s, j * DMA_UNROLL + q)
            return carry
        n_groups = n_rows // DMA_UNROLL
        lax.fori_loop(0, n_groups, group, 0)
        lax.fori_loop(n_groups * DMA_UNROLL, n_rows, one, 0)

    def tile_body(cur, prev_full):
        nxt = 1 - cur
        wait_gather(cur)

        @pl.when(i >= 2)
        def _():
            wait_scatter(cur, nv_ref[i - 2])

        if not prev_full:
            scatter_loop(slot_prev_ref, nxt, n_prev)

        @pl.when(new_expert)
        def _():
            half = PERM_BLOCK // 2
            for j in range(2 * D_EXPERT // PERM_BLOCK):
                blk = wgu_ref[0, :, j * PERM_BLOCK:(j + 1) * PERM_BLOCK].astype(BF16)
                split = jnp.dot(blk, perm_ref[...], preferred_element_type=F32)
                wg_s[:, j * half:(j + 1) * half] = split[:, :half].astype(BF16)
                wu_s[:, j * half:(j + 1) * half] = split[:, half:].astype(BF16)
            wd_s[...] = wd_ref[0].astype(BF16)

        n_grp = bm // DMA_GROUPS

        def dma_group(gi):
            for r in range(gi * n_grp, (gi + 1) * n_grp):
                gather_row(tok_next_ref, nxt, r)
                if prev_full:
                    scatter_row(slot_prev_ref, nxt, r)

        dma_group(0)
        x = _load_token_tiles(xbufs[cur], bm).astype(BF16)
        g = jnp.dot(x, wg_s[...], preferred_element_type=F32) + bg_ref[0]
        dma_group(1)
        u = jnp.dot(x, wu_s[...], preferred_element_type=F32) + bu_ref[0]
        dma_group(2)
        gate = jnp.minimum(g, SWIGLU_LIMIT)
        up = jnp.clip(u, -SWIGLU_LIMIT, SWIGLU_LIMIT)
        act = gate * _sigmoid(SWIGLU_ALPHA * gate) * (up + 1.0)
        dma_group(3)
        y = jnp.dot(act.astype(BF16), wd_s[...], preferred_element_type=F32) + bd_ref[0]
        _store_token_tiles(ybufs[cur], y, bm)

        @pl.when(i == n_used - 1)
        def _():
            scatter_loop(slot_ref, cur, nv_ref[i])
            wait_gather(nxt)
            wait_scatter(cur, nv_ref[i])

            @pl.when(i >= 1)
            def _():
                wait_scatter(nxt, nv_ref[i - 1])

    for parity in range(2):
        for prev_full in (False, True):
            full = (n_prev == bm) if prev_full else (n_prev != bm)
            cond = (i < n_used) & (lax.rem(i, 2) == parity) & full
            pl.when(cond)(functools.partial(tile_body, parity, prev_full))


def _split_permutation():
    half = PERM_BLOCK // 2
    src = jnp.arange(PERM_BLOCK)[:, None]
    dst = jnp.arange(PERM_BLOCK)[None, :]
    return (src == jnp.where(dst < half, 2 * dst, 2 * (dst - half) + 1)).astype(BF16)


def _experts(tile_expert, n_used, tile_valid, row_tok, row_slot, h2, wgu, wd, bg, bu, bd, bm):
    n_rows = row_tok.shape[0]
    n_tiles = n_rows // bm
    n_slots = h2.shape[0] // N_SEG * TOP_K
    cur = lambda i, te, nu, nv: (jnp.minimum(i, nu[0] - 1),)
    nxt = lambda i, te, nu, nv: (jnp.minimum(i + 1, nu[0] - 1),)
    prv = lambda i, te, nu, nv: (jnp.clip(i - 1, 0, nu[0] - 1),)
    w_map = lambda i, te, nu, nv: (te[i], 0, 0)
    grid_spec = pltpu.PrefetchScalarGridSpec(
        num_scalar_prefetch=3,
        grid=(n_tiles,),
        in_specs=[pl.BlockSpec((bm,), cur, memory_space=pltpu.SMEM),
                  pl.BlockSpec((bm,), nxt, memory_space=pltpu.SMEM),
                  pl.BlockSpec((bm,), cur, memory_space=pltpu.SMEM),
                  pl.BlockSpec((bm,), prv, memory_space=pltpu.SMEM),
                  pl.BlockSpec(memory_space=pl.ANY),
                  pl.BlockSpec((1, D_MODEL, 2 * D_EXPERT), w_map),
                  pl.BlockSpec((1, D_EXPERT, D_MODEL), w_map),
                  pl.BlockSpec((PERM_BLOCK, PERM_BLOCK), lambda i, te, nu, nv: (0, 0)),
                  pl.BlockSpec((1, 1, D_EXPERT), w_map),
                  pl.BlockSpec((1, 1, D_EXPERT), w_map),
                  pl.BlockSpec((1, 1, D_MODEL), w_map)],
        out_specs=pl.BlockSpec(memory_space=pl.ANY),
        scratch_shapes=[pltpu.VMEM((bm * N_SEG, LANES), F32), pltpu.VMEM((bm * N_SEG, LANES), F32),
                        pltpu.VMEM((bm * N_SEG, LANES), F32), pltpu.VMEM((bm * N_SEG, LANES), F32),
                        pltpu.VMEM((D_MODEL, D_EXPERT), BF16), pltpu.VMEM((D_MODEL, D_EXPERT), BF16),
                        pltpu.VMEM((D_EXPERT, D_MODEL), BF16),
                        pltpu.SemaphoreType.DMA((2,)), pltpu.SemaphoreType.DMA((2,))],
    )
    return pl.pallas_call(
        functools.partial(_expert_kernel, bm=bm),
        grid_spec=grid_spec,
        out_shape=jax.ShapeDtypeStruct((n_slots * N_SEG, LANES), F32),
        compiler_params=pltpu.CompilerParams(dimension_semantics=("arbitrary",),
                                             vmem_limit_bytes=VMEM_LIMIT,
                                             has_side_effects=True),
        name="experts",
    )(tile_expert, n_used, tile_valid, row_tok, row_tok, row_slot, row_slot, h2, wgu, wd, _split_permutation(),
      bg, bu, bd)


def _combine_kernel(y0_ref, y1_ref, y2_ref, y3_ref, gate_ref, x1_ref, g_ref, o_ref, *, tm):
    acc = x1_ref[...]
    for k, y_ref in enumerate((y0_ref, y1_ref, y2_ref, y3_ref)):
        acc = acc + gate_ref[:, k:k + 1] * _load_token_tiles(y_ref, tm)
    o_ref[...] = _rms(acc, g_ref[...])


def _combine(y_slots, gates, x1, g, tm):
    t = x1.shape[0]
    nb = t // tm

    def slot_spec(k):
        return pl.BlockSpec((tm * N_SEG, LANES), lambda i: (k * nb + i, 0))

    return pl.pallas_call(
        functools.partial(_combine_kernel, tm=tm),
        grid=(nb,),
        in_specs=[slot_spec(0), slot_spec(1), slot_spec(2), slot_spec(3),
                  pl.BlockSpec((tm, TOP_K), lambda i: (i, 0)),
                  pl.BlockSpec((tm, D_MODEL), lambda i: (i, 0)),
                  pl.BlockSpec((1, D_MODEL), lambda i: (0, 0))],
        out_specs=pl.BlockSpec((tm, D_MODEL), lambda i: (i, 0)),
        out_shape=jax.ShapeDtypeStruct((t, D_MODEL), F32),
        compiler_params=pltpu.CompilerParams(dimension_semantics=("arbitrary",),
                                             vmem_limit_bytes=VMEM_LIMIT),
        name="combine",
    )(y_slots, y_slots, y_slots, y_slots, gates, x1, g)


def _lane_row(vec, lane0):
    n = vec.shape[0]
    return jnp.zeros((8, LANES), F32).at[0, lane0:lane0 + n].set(vec.astype(F32))


def _tile_size(n, pref):
    return pref if n % pref == 0 else n


def kernel(x, norm_mix, w_in, gdn_conv, gdn_a_log, gdn_dt_bias, gdn_norm, ml_conv, ml_b_i, ml_b_f,
           ml_norm, w_up_gdn, w_up_ml, w_out, norm_ffn, w_router, b_router, w_gate_up, b_gate_up,
           w_down, b_down, norm_final):
    assert norm_mix.shape[0] == 1, "single-layer stack"
    b, s, d = x.shape
    assert d == D_MODEL and s % CHUNK == 0
    t = b * s
    x2 = x.reshape(t, d)

    w = w_in[0]
    w_big = jnp.concatenate([_cols(w, n) for n in ("g_q", "g_k", "g_v", "g_z", "m_q", "m_k", "m_v",
                                                   "m_o", "gate_gdn", "gate_ml")], axis=1).astype(BF16)
    w_small = jnp.concatenate([_cols(w, n) for n in ("g_a", "g_b", "m_i", "m_f")], axis=1)
    w_small = jnp.pad(w_small, ((0, 0), (0, LANES - N_SMALL))).astype(BF16)

    tm = _tile_size(t, 512)
    big, small = _in_proj(x2, norm_mix[0][None, :], w_big, w_small, tm)
    big3 = big.reshape(b, s, N_BIG)
    small3 = small.reshape(b, s, LANES)

    cs = _tile_size(s, 512)
    gdn_prm = _lane_row(gdn_a_log[0], 0).at[1, 0:N_HEADS].set(gdn_dt_bias[0].astype(F32))
    oa = _gdn(big3, small3, gdn_conv[0].astype(F32), gdn_prm, gdn_norm[0][None, :].astype(F32), cs)
    ml_prm = _lane_row(ml_b_i[0], 2 * N_HEADS).at[0, 3 * N_HEADS:4 * N_HEADS].set(ml_b_f[0].astype(F32))
    ob = _mlstm(big3, small3, ml_conv[0].astype(F32), ml_prm, ml_norm[0][None, :].astype(F32), cs)

    w_r = jnp.pad(w_router[0], ((0, 0), (0, LANES - N_EXPERTS))).astype(BF16)
    b_r = jnp.full((1, LANES), -1e30, F32).at[0, :N_EXPERTS].set(b_router[0].astype(F32))
    x1, h2, idx, gates, rank, counts = _merge(
        oa.reshape(t, D_QK), ob.reshape(t, D_QK), big, x2, w_up_gdn[0].astype(BF16),
        w_up_ml[0].astype(BF16), w_out[0].astype(BF16), norm_ffn[0][None, :], w_r, b_r, tm)

    bm = EXPERT_TILE
    n_assign = t * TOP_K
    n_tiles = -(-n_assign // bm) + N_EXPERTS
    counts = counts[0, :N_EXPERTS]
    padded = (counts + bm - 1) // bm * bm
    pend = jnp.cumsum(padded)
    pstart = pend - padded
    dest = (pstart[idx] + rank).reshape(-1)
    n_used = (pend[-1] // bm).astype(I32)
    tile_ids = jnp.minimum(jnp.arange(n_tiles, dtype=I32), n_used - 1)
    tile_expert = jnp.minimum(jnp.sum((pend[None, :] <= (tile_ids * bm)[:, None]).astype(I32), axis=1),
                              N_EXPERTS - 1)
    tile_valid = jnp.clip(counts[tile_expert] - (tile_ids * bm - pstart[tile_expert]), 0, bm).astype(I32)
    assign = jnp.arange(n_assign, dtype=I32)
    pads = padded - counts
    cpad = jnp.cumsum(pads)
    j = jnp.arange(n_tiles * bm - n_assign, dtype=I32)
    pe = jnp.minimum(jnp.sum((cpad[None, :] <= j[:, None]).astype(I32), axis=1), N_EXPERTS - 1)
    pad_row = jnp.where(j < cpad[-1], pstart[pe] + counts[pe] + j - (cpad[pe] - pads[pe]),
                        pend[-1] + j - cpad[-1])
    _, row_slot = lax.sort((jnp.concatenate([dest, pad_row]).astype(I32),
                            jnp.concatenate([(assign % TOP_K) * t + assign // TOP_K, jnp.zeros_like(j)])),
                           num_keys=1)
    row_tok = row_slot % t

    bgu = b_gate_up[0]
    y_slots = _experts(tile_expert.astype(I32), n_used.reshape(1), tile_valid, row_tok, row_slot, h2,
                       w_gate_up[0], w_down[0],
                       bgu[:, None, 0::2].astype(F32), bgu[:, None, 1::2].astype(F32),
                       b_down[0][:, None, :].astype(F32), bm)
    out = _combine(y_slots, gates, x1, norm_final[None, :], _tile_size(t, 256))
    return out.reshape(b, s, d)
```

```python
import functools

import jax
import jax.numpy as jnp
from jax import lax
from jax.experimental import pallas as pl
from jax.experimental.pallas import tpu as pltpu
from jax.experimental.pallas import tpu_sc as plsc

F32 = jnp.float32
BF16 = jnp.bfloat16
I32 = jnp.int32
U32 = jnp.uint32

D_MODEL = 1024
N_HEADS = 4
D_HEAD = 128
D_QK = N_HEADS * D_HEAD
CONV_WIDTH = 4
CHUNK = 64
GATE_SOFTCAP = 15.0
N_EXPERTS = 32
TOP_K = 4
D_EXPERT = 1024
SWIGLU_LIMIT = 7.0
SWIGLU_ALPHA = 1.702
NORM_EPS = 1e-6

LANES = 128
SUBLANES = 8
N_SMALL = 16
N_BIG = 6 * D_MODEL
VMEM_LIMIT = 56 * 1024 * 1024
EXPERT_TILE = 512
DMA_UNROLL = 8
GATHER_WINDOW = 64
DMA_GROUPS = 4
PERM_BLOCK = 256
LOCAL_CHUNKS = 2

_OFF = {}
_o = 0
for _name, _w in (("g_q", D_QK), ("g_k", D_QK), ("g_v", D_QK), ("g_z", D_QK), ("g_a", N_HEADS),
                  ("g_b", N_HEADS), ("m_q", D_QK), ("m_k", D_QK), ("m_v", D_QK), ("m_o", D_QK),
                  ("m_i", N_HEADS), ("m_f", N_HEADS), ("gate_gdn", D_MODEL), ("gate_ml", D_MODEL)):
    _OFF[_name] = (_o, _w)
    _o += _w


def _cols(w, name):
    o, n = _OFF[name]
    return w[:, o:o + n]


_NN = (((1,), (0,)), ((), ()))
_NT = (((1,), (1,)), ((), ()))
_TN = (((0,), (0,)), ((), ()))


def _bdot(a, b, dims=_NN):
    return lax.dot_general(a.astype(BF16), b.astype(BF16), dims, preferred_element_type=F32)


def _split3(a):
    hi = a.astype(BF16)
    r = a - hi.astype(F32)
    mid = r.astype(BF16)
    return hi, mid, (r - mid.astype(F32)).astype(BF16)


def _select_dot(sel01, x, dims=_NN):
    s = sel01.astype(BF16)
    h, m, l = _split3(x)
    d = functools.partial(lax.dot_general, dimension_numbers=dims, preferred_element_type=F32)
    return d(s, h) + (d(s, m) + d(s, l))


def _sigmoid(x):
    return 1.0 / (1.0 + jnp.exp(-x))


def _silu(x):
    return x * _sigmoid(x)


def _softplus(x):
    return jnp.maximum(x, 0.0) + jnp.log(1.0 + jnp.exp(-jnp.abs(x)))


def _rms(x, g):
    return x * lax.rsqrt(jnp.mean(x * x, axis=-1, keepdims=True) + NORM_EPS) * g


def _in_proj_kernel(x_ref, g_ref, wbig_ref, wsmall_ref, big_ref, small_ref, *, n_chunk):
    h = _rms(x_ref[...], g_ref[...]).astype(BF16)
    for c in range(N_BIG // n_chunk):
        sl = slice(c * n_chunk, (c + 1) * n_chunk)
        big_ref[:, sl] = jnp.dot(h, wbig_ref[:, sl], preferred_element_type=F32)
    small_ref[...] = jnp.dot(h, wsmall_ref[...], preferred_element_type=F32)


def _in_proj(x2, g, w_big, w_small, tm):
    t = x2.shape[0]
    return pl.pallas_call(
        functools.partial(_in_proj_kernel, n_chunk=1024),
        grid=(t // tm,),
        in_specs=[pl.BlockSpec((tm, D_MODEL), lambda i: (i, 0)),
                  pl.BlockSpec((1, D_MODEL), lambda i: (0, 0)),
                  pl.BlockSpec((D_MODEL, N_BIG), lambda i: (0, 0)),
                  pl.BlockSpec((D_MODEL, LANES), lambda i: (0, 0))],
        out_specs=[pl.BlockSpec((tm, N_BIG), lambda i: (i, 0)),
                   pl.BlockSpec((tm, LANES), lambda i: (i, 0))],
        out_shape=[jax.ShapeDtypeStruct((t, N_BIG), F32),
                   jax.ShapeDtypeStruct((t, LANES), F32)],
        compiler_params=pltpu.CompilerParams(dimension_semantics=("arbitrary",),
                                             vmem_limit_bytes=VMEM_LIMIT),
        name="in_proj",
    )(x2, g, w_big, w_small)


def _chunk_masks():
    r = lax.broadcasted_iota(I32, (CHUNK, CHUNK), 0)
    c = lax.broadcasted_iota(I32, (CHUNK, CHUNK), 1)
    return r > c, r >= c


def _block_cumsum_matrix(cs):
    r = lax.broadcasted_iota(I32, (cs, cs), 0)
    c = lax.broadcasted_iota(I32, (cs, cs), 1)
    same = (r // CHUNK) == (c // CHUNK)
    return jnp.where(same & (r >= c), 1.0, 0.0).astype(F32)


def _lane_onehot(lane):
    return jnp.where(lax.broadcasted_iota(I32, (CHUNK, LANES), 1) == lane, 1.0, 0.0).astype(F32)


def _conv_silu_pieces(x_ref, w_ref, buf_ref, cs, width):
    @pl.when(pl.program_id(1) == 0)
    def _():
        buf_ref[0:8, :] = jnp.zeros((8, buf_ref.shape[1]), F32)

    buf_ref[8:8 + cs, :] = x_ref[0]
    for r0 in range(0, cs, CHUNK):
        for c0 in range(0, buf_ref.shape[1], width):
            cols = slice(c0, c0 + width)
            acc = w_ref[CONV_WIDTH - 1:CONV_WIDTH, cols] * buf_ref[8 + r0:8 + r0 + CHUNK, cols]
            for j in range(CONV_WIDTH - 1):
                s = 8 - (CONV_WIDTH - 1) + j + r0
                acc = acc + w_ref[j:j + 1, cols] * buf_ref[s:s + CHUNK, cols]
            yield r0, c0, _silu(acc)
    buf_ref[0:8, :] = buf_ref[cs:cs + 8, :]


def _gdn_kernel(qkv_ref, z_ref, sm_ref, conv_ref, prm_ref, gn_ref, o_ref,
                buf_ref, q_s, k_s, v_s, w_s, attn_s, gam_s, beta_s, state_s, *, cs):
    n_chunks = cs // CHUNK

    @pl.when(pl.program_id(1) == 0)
    def _():
        state_s[...] = jnp.zeros(state_s.shape, F32)

    for r0, c0, piece in _conv_silu_pieces(qkv_ref, conv_ref, buf_ref, cs, D_QK):
        rows = slice(r0, r0 + CHUNK)
        if c0 == 2 * D_QK:
            v_s[rows, :] = piece
            continue
        dst, scale = (q_s, D_HEAD ** -0.5) if c0 == 0 else (k_s, 1.0)
        for h in range(N_HEADS):
            sl = slice(h * D_HEAD, (h + 1) * D_HEAD)
            a = piece[:, sl]
            dst[rows, sl] = a * (lax.rsqrt(jnp.sum(a * a, axis=-1, keepdims=True) + NORM_EPS) * scale)

    sm = sm_ref[0]
    logdec = -jnp.exp(prm_ref[0:1, :]) * _softplus(sm + prm_ref[1:2, :])
    gam_s[...] = _select_dot(_block_cumsum_matrix(cs), logdec)
    beta_s[...] = _sigmoid(sm)

    strict, incl = _chunk_masks()
    eye = jnp.where(lax.broadcasted_iota(I32, (CHUNK, CHUNK), 0)
                    == lax.broadcasted_iota(I32, (CHUNK, CHUNK), 1), 1.0, 0.0).astype(F32)
    gnorm = gn_ref[...]

    heads = range(N_HEADS)
    hsl = [slice(h * D_HEAD, (h + 1) * D_HEAD) for h in heads]
    asl = [slice(h * CHUNK, (h + 1) * CHUNK) for h in heads]
    onehots = [_lane_onehot(h) for h in heads]

    def local_body(it, carry):
        rows = [pl.ds(pl.multiple_of((it * LOCAL_CHUNKS + ci) * CHUNK, CHUNK), CHUNK)
                for ci in range(LOCAL_CHUNKS)]
        probs = [(ci, h) for ci in range(LOCAL_CHUNKS) for h in heads]
        gam_all = [gam_s[r, :] for r in rows]
        beta_all = [beta_s[r, :] for r in rows]
        q = [q_s[rows[ci], hsl[h]] for ci, h in probs]
        k = [k_s[rows[ci], hsl[h]] for ci, h in probs]
        v = [v_s[rows[ci], hsl[h]] for ci, h in probs]
        gc = [gam_all[ci][:, h:h + 1] for ci, h in probs]
        beta = [beta_all[ci][:, N_HEADS + h:N_HEADS + h + 1] for ci, h in probs]
        g_row = [_select_dot(onehots[h], gam_all[ci], _NT) for ci, h in probs]
        kb = [a * b for a, b in zip(k, beta)]
        kk = [_bdot(a, b, _NT) for a, b in zip(kb, k)]
        qk = [_bdot(a, b, _NT) for a, b in zip(q, k)]
        decay = [jnp.where(incl, jnp.exp(jnp.minimum(a - b, 0.0)), 0.0) for a, b in zip(gc, g_row)]
        lower = [jnp.where(strict, a * b, 0.0) for a, b in zip(kk, decay)]
        inv = [eye - a for a in lower]
        pw = [_bdot(a, a) for a in lower]
        for lvl in range(5):
            upd = [_bdot(a, b) for a, b in zip(inv, pw)]
            if lvl < 4:
                pw = [_bdot(a, a) for a in pw]
            inv = [a + b for a, b in zip(inv, upd)]
        e_gc = [jnp.exp(a) for a in gc]
        rhs = [jnp.concatenate([a * b, c * d], axis=1) for a, b, c, d in zip(v, beta, kb, e_gc)]
        sol = [_bdot(a, b) for a, b in zip(inv, rhs)]
        for p, (ci, h) in enumerate(probs):
            v_s[rows[ci], hsl[h]] = sol[p][:, :D_HEAD]
            w_s[rows[ci], hsl[h]] = sol[p][:, D_HEAD:]
            attn_s[rows[ci], asl[h]] = qk[p] * decay[p]
            q_s[rows[ci], hsl[h]] = q[p] * e_gc[p]
            k_s[rows[ci], hsl[h]] = k[p] * jnp.exp(gc[p][CHUNK - 1:CHUNK, :] - gc[p])
        return carry

    lax.fori_loop(0, n_chunks // LOCAL_CHUNKS, local_body, 0)

    def state_body(c, carry):
        rows = pl.ds(pl.multiple_of(c * CHUNK, CHUNK), CHUNK)
        g_last = gam_s[pl.ds(c * CHUNK + CHUNK - 1, 1), :]
        st = [state_s[h] for h in heads]
        ws = [_bdot(w_s[rows, hsl[h]], st[h]) for h in heads]
        qs = [_bdot(q_s[rows, hsl[h]], st[h]) for h in heads]
        u = [v_s[rows, hsl[h]] - ws[h] for h in heads]
        au = [_bdot(attn_s[rows, asl[h]], u[h]) for h in heads]
        ku = [_bdot(k_s[rows, hsl[h]], u[h], _TN) for h in heads]
        for h in heads:
            state_s[h] = st[h] * jnp.exp(g_last[:, h:h + 1]) + ku[h]
            o = qs[h] + au[h]
            o = o * lax.rsqrt(jnp.mean(o * o, axis=-1, keepdims=True) + NORM_EPS) * gnorm
            o_ref[0, rows, hsl[h]] = (o * _silu(z_ref[0, rows, hsl[h]])).astype(o_ref.dtype)
        return carry

    lax.fori_loop(0, n_chunks, state_body, 0)


def _gdn(big3, small3, conv_w, prm, gnorm, cs):
    b, s, _ = big3.shape
    return pl.pallas_call(
        functools.partial(_gdn_kernel, cs=cs),
        grid=(b, s // cs),
        in_specs=[pl.BlockSpec((1, cs, 3 * D_QK), lambda i, j: (i, j, 0)),
                  pl.BlockSpec((1, cs, D_QK), lambda i, j: (i, j, 3)),
                  pl.BlockSpec((1, cs, LANES), lambda i, j: (i, j, 0)),
                  pl.BlockSpec((CONV_WIDTH, 3 * D_QK), lambda i, j: (0, 0)),
                  pl.BlockSpec((8, LANES), lambda i, j: (0, 0)),
                  pl.BlockSpec((1, D_HEAD), lambda i, j: (0, 0))],
        out_specs=pl.BlockSpec((1, cs, D_QK), lambda i, j: (i, j, 0)),
        out_shape=jax.ShapeDtypeStruct((b, s, D_QK), BF16),
        scratch_shapes=[pltpu.VMEM((cs + 8, 3 * D_QK), F32),
                        pltpu.VMEM((cs, D_QK), F32), pltpu.VMEM((cs, D_QK), F32),
                        pltpu.VMEM((cs, D_QK), F32), pltpu.VMEM((cs, D_QK), F32),
                        pltpu.VMEM((cs, N_HEADS * CHUNK), F32),
                        pltpu.VMEM((cs, LANES), F32), pltpu.VMEM((cs, LANES), F32),
                        pltpu.VMEM((N_HEADS, D_HEAD, D_HEAD), F32)],
        compiler_params=pltpu.CompilerParams(dimension_semantics=("arbitrary", "arbitrary"),
                                             vmem_limit_bytes=VMEM_LIMIT),
        name="gdn",
    )(big3, big3, small3, conv_w, prm, gnorm)


def _mlstm_kernel(qk_ref, v_ref, og_ref, sm_ref, conv_ref, prm_ref, gn_ref, o_ref,
                  buf_ref, q_s, k_s, bcum_s, ipre_s, state_s, m_s, *, cs):
    n_chunks = cs // CHUNK

    @pl.when(pl.program_id(1) == 0)
    def _():
        state_s[...] = jnp.zeros(state_s.shape, F32)
        m_s[...] = jnp.zeros(m_s.shape, F32)

    for r0, c0, piece in _conv_silu_pieces(qk_ref, conv_ref, buf_ref, cs, D_QK):
        if c0 == 0:
            q_s[r0:r0 + CHUNK, :] = piece
        else:
            k_s[r0:r0 + CHUNK, :] = piece * (D_HEAD ** -0.5)

    pre = sm_ref[0] + prm_ref[0:1, :]
    capped = GATE_SOFTCAP * jnp.tanh(pre / GATE_SOFTCAP)
    logf = -_softplus(-capped)
    bcum_s[...] = _select_dot(_block_cumsum_matrix(cs), logf)
    ipre_s[...] = pltpu.roll(capped, N_HEADS, axis=1)

    _, incl = _chunk_masks()
    ones_aug = jnp.ones((CHUNK, D_HEAD), F32)

    heads = range(N_HEADS)
    hsl = [slice(h * D_HEAD, (h + 1) * D_HEAD) for h in heads]
    lanes = [3 * N_HEADS + h for h in heads]
    onehots = [_lane_onehot(lane) for lane in lanes]

    def chunk_body(c, carry):
        rows = pl.ds(pl.multiple_of(c * CHUNK, CHUNK), CHUNK)
        b_all = bcum_s[rows, :]
        comb_all = ipre_s[rows, :] - b_all
        q = [q_s[rows, hsl[h]] for h in heads]
        k = [k_s[rows, hsl[h]] for h in heads]
        v_aug = [jnp.concatenate([v_ref[0, rows, hsl[h]], ones_aug], axis=1) for h in heads]
        st = [state_s[h] for h in heads]
        m_st = [m_s[h] for h in heads]
        bc = [b_all[:, lane:lane + 1] for lane in lanes]
        comb_row = [_select_dot(onehots[h], comb_all, _NT) for h in heads]
        qk = [_bdot(q[h], k[h], _NT) for h in heads]
        inter = [_bdot(q[h], st[h]) for h in heads]
        b_last = [bc[h][CHUNK - 1:CHUNK, :] for h in heads]
        a_log = [b_last[h] + comb_all[:, lanes[h]:lanes[h] + 1] for h in heads]
        m_chunk = [jnp.max(a_log[h], axis=0, keepdims=True) for h in heads]
        wk = [k[h] * jnp.exp(a_log[h] - m_chunk[h]) for h in heads]
        d_state = [_bdot(wk[h], v_aug[h], _TN) for h in heads]
        d_log = [jnp.where(incl, bc[h] + comb_row[h], -jnp.inf) for h in heads]
        m_loc = [jnp.max(d_log[h], axis=-1, keepdims=True) for h in heads]
        p = [jnp.exp(d_log[h] - m_loc[h]) * qk[h] for h in heads]
        loc = [_bdot(p[h], v_aug[h]) for h in heads]
        for h in heads:
            m_inter = bc[h] + m_st[h]
            m_t = jnp.maximum(m_loc[h], m_inter)
            tot = jnp.exp(m_inter - m_t) * inter[h] + jnp.exp(m_loc[h] - m_t) * loc[h]
            hh = tot[:, :D_HEAD] / jnp.maximum(jnp.abs(tot[:, D_HEAD:]), jnp.exp(-m_t))
            m_new = jnp.maximum(b_last[h] + m_st[h], m_chunk[h])
            state_s[h] = (jnp.exp(b_last[h] + m_st[h] - m_new) * st[h]
                          + jnp.exp(m_chunk[h] - m_new) * d_state[h])
            m_s[h] = m_new
            hh = hh * lax.rsqrt(jnp.mean(hh * hh, axis=-1, keepdims=True) + NORM_EPS) * gn_ref[:, hsl[h]]
            o_ref[0, rows, hsl[h]] = (hh * _sigmoid(og_ref[0, rows, hsl[h]])).astype(o_ref.dtype)
        return carry

    lax.fori_loop(0, n_chunks, chunk_body, 0)


def _mlstm(big3, small3, conv_w, prm, gnorm, cs):
    b, s, _ = big3.shape
    return pl.pallas_call(
        functools.partial(_mlstm_kernel, cs=cs),
        grid=(b, s // cs),
        in_specs=[pl.BlockSpec((1, cs, 2 * D_QK), lambda i, j: (i, j, 2)),
                  pl.BlockSpec((1, cs, D_QK), lambda i, j: (i, j, 6)),
                  pl.BlockSpec((1, cs, D_QK), lambda i, j: (i, j, 7)),
                  pl.BlockSpec((1, cs, LANES), lambda i, j: (i, j, 0)),
                  pl.BlockSpec((CONV_WIDTH, 2 * D_QK), lambda i, j: (0, 0)),
                  pl.BlockSpec((8, LANES), lambda i, j: (0, 0)),
                  pl.BlockSpec((1, D_QK), lambda i, j: (0, 0))],
        out_specs=pl.BlockSpec((1, cs, D_QK), lambda i, j: (i, j, 0)),
        out_shape=jax.ShapeDtypeStruct((b, s, D_QK), BF16),
        scratch_shapes=[pltpu.VMEM((cs + 8, 2 * D_QK), F32),
                        pltpu.VMEM((cs, D_QK), F32), pltpu.VMEM((cs, D_QK), F32),
                        pltpu.VMEM((cs, LANES), F32), pltpu.VMEM((cs, LANES), F32),
                        pltpu.VMEM((N_HEADS, D_HEAD, 2 * D_HEAD), F32),
                        pltpu.VMEM((N_HEADS, 1, 1), F32)],
        compiler_params=pltpu.CompilerParams(dimension_semantics=("arbitrary", "arbitrary"),
                                             vmem_limit_bytes=VMEM_LIMIT),
        name="mlstm",
    )(big3, big3, big3, small3, conv_w, prm, gnorm)


N_SEG = D_MODEL // LANES
assert N_SEG == SUBLANES


def _store_token_tiles(ref, val, n):
    for s in range(N_SEG):
        ref[pl.ds(s, n, stride=N_SEG), :] = val[:, s * LANES:(s + 1) * LANES]


def _load_token_tiles(ref, n):
    return jnp.concatenate([ref[pl.ds(s, n, stride=N_SEG), :] for s in range(N_SEG)], axis=1)


def _merge_kernel(oa_ref, ob_ref, gates_ref, x_ref, wa_ref, wb_ref, wo_ref, g_ref, wr_ref, br_ref,
                  x1_ref, h2_ref, idx_ref, gate_ref, rank_ref, cnt_ref, carry_s, *, tm):
    @pl.when(pl.program_id(0) == 0)
    def _():
        carry_s[...] = jnp.zeros(carry_s.shape, F32)

    y_a = jnp.dot(oa_ref[...], wa_ref[...], preferred_element_type=F32)
    y_b = jnp.dot(ob_ref[...], wb_ref[...], preferred_element_type=F32)
    y = _sigmoid(gates_ref[:, :D_MODEL]) * y_a + _sigmoid(gates_ref[:, D_MODEL:]) * y_b
    x1 = x_ref[...] + jnp.dot(y.astype(BF16), wo_ref[...], preferred_element_type=F32)
    x1_ref[...] = x1
    h2 = _rms(x1, g_ref[...])
    h2_ref[...] = _pack_bf16_pair(h2[:, :D_MODEL // 2], h2[:, D_MODEL // 2:])
    logits = jnp.dot(h2.astype(BF16), wr_ref[...], preferred_element_type=F32) + br_ref[...]

    lane = lax.broadcasted_iota(I32, (tm, LANES), 1).astype(F32)
    vals, sels = [], []
    idx_t = jnp.zeros((tm, LANES), F32)
    work = logits
    for k in range(TOP_K):
        m = jnp.max(work, axis=-1, keepdims=True)
        i = jnp.min(jnp.where(work == m, lane, float(LANES)), axis=-1, keepdims=True)
        sel = lane == i
        work = jnp.where(sel, -jnp.inf, work)
        idx_t = jnp.where(lane == float(k), i, idx_t)
        vals.append(m)
        sels.append(sel)
    es = [jnp.exp(v - vals[0]) for v in vals]
    denom = es[0] + es[1] + es[2] + es[3]
    gate_t = jnp.zeros((tm, LANES), F32)
    for k in range(TOP_K):
        gate_t = jnp.where(lane == float(k), es[k] / denom, gate_t)
    idx_ref[...] = idx_t[:, :TOP_K].astype(I32)
    gate_ref[...] = gate_t[:, :TOP_K]

    onehot = jnp.zeros((tm, LANES), F32)
    for sel in sels:
        onehot = onehot + jnp.where(sel, 1.0, 0.0)
    r = lax.broadcasted_iota(I32, (tm, tm), 0)
    c = lax.broadcasted_iota(I32, (tm, tm), 1)
    before = jnp.where(r > c, 1.0, 0.0).astype(BF16)
    cum = jnp.dot(before, onehot.astype(BF16), preferred_element_type=F32) + carry_s[...]
    rank_t = jnp.zeros((tm, LANES), F32)
    for k, sel in enumerate(sels):
        rk = jnp.sum(jnp.where(sel, cum, 0.0), axis=-1, keepdims=True)
        rank_t = jnp.where(lane == float(k), rk, rank_t)
    rank_ref[...] = rank_t[:, :TOP_K].astype(I32)
    total = carry_s[...] + jnp.sum(onehot, axis=0, keepdims=True)
    carry_s[...] = total
    cnt_ref[...] = total.astype(I32)


def _merge(oa, ob, big, x2, wa, wb, wo, g, wr, br, tm):
    t = x2.shape[0]
    const = lambda i: (0, 0)
    return pl.pallas_call(
        functools.partial(_merge_kernel, tm=tm),
        grid=(t // tm,),
        in_specs=[pl.BlockSpec((tm, D_QK), lambda i: (i, 0)),
                  pl.BlockSpec((tm, D_QK), lambda i: (i, 0)),
                  pl.BlockSpec((tm, 2 * D_MODEL), lambda i: (i, 2)),
                  pl.BlockSpec((tm, D_MODEL), lambda i: (i, 0)),
                  pl.BlockSpec((D_QK, D_MODEL), const),
                  pl.BlockSpec((D_QK, D_MODEL), const),
                  pl.BlockSpec((D_MODEL, D_MODEL), const),
                  pl.BlockSpec((1, D_MODEL), const),
                  pl.BlockSpec((D_MODEL, LANES), const),
                  pl.BlockSpec((1, LANES), const)],
        out_specs=[pl.BlockSpec((tm, D_MODEL), lambda i: (i, 0)),
                   pl.BlockSpec((tm, D_MODEL // 2), lambda i: (i, 0)),
                   pl.BlockSpec((tm, TOP_K), lambda i: (i, 0)),
                   pl.BlockSpec((tm, TOP_K), lambda i: (i, 0)),
                   pl.BlockSpec((tm, TOP_K), lambda i: (i, 0)),
                   pl.BlockSpec((1, LANES), const)],
        out_shape=[jax.ShapeDtypeStruct((t, D_MODEL), F32),
                   jax.ShapeDtypeStruct((t, D_MODEL // 2), U32),
                   jax.ShapeDtypeStruct((t, TOP_K), I32),
                   jax.ShapeDtypeStruct((t, TOP_K), F32),
                   jax.ShapeDtypeStruct((t, TOP_K), I32),
                   jax.ShapeDtypeStruct((1, LANES), I32)],
        scratch_shapes=[pltpu.VMEM((1, LANES), F32)],
        compiler_params=pltpu.CompilerParams(dimension_semantics=("arbitrary",),
                                             vmem_limit_bytes=VMEM_LIMIT),
        name="merge_router",
    )(oa, ob, big, x2, wa, wb, wo, g, wr, br)


def _pack_bf16_pair(a, b):
    lo = lax.bitcast_convert_type(a.astype(BF16).astype(F32), U32) >> 16
    hi = lax.bitcast_convert_type(b.astype(BF16).astype(F32), U32) & jnp.uint32(0xFFFF0000)
    return lo | hi


def _unpack_bf16_pair(w):
    a = lax.bitcast_convert_type(w << 16, F32).astype(BF16)
    b = lax.bitcast_convert_type(w & jnp.uint32(0xFFFF0000), F32).astype(BF16)
    return a, b


def _gather_rows(table, idx):
    n = idx.shape[0]
    d = table.shape[1]
    idx2 = jnp.pad(idx.reshape(n // GATHER_WINDOW, GATHER_WINDOW), ((0, 0), (0, LANES - GATHER_WINDOW)))
    mesh = plsc.VectorSubcoreMesh(core_axis_name="core", subcore_axis_name="subcore")

    @pl.kernel(out_type=jax.ShapeDtypeStruct((n, d), table.dtype), mesh=mesh)
    def gather_kernel(table_hbm, idx_hbm, out_hbm):
        def body(idx_vmem, out_vmem):
            pltpu.sync_copy(table_hbm.at[idx_vmem.at[0, pl.ds(0, GATHER_WINDOW)]], out_vmem)

        pltpu.emit_pipeline(
            body,
            grid=(n // GATHER_WINDOW,),
            in_specs=[pl.BlockSpec((1, LANES), index_map=lambda i: (i, 0))],
            out_specs=[pl.BlockSpec((GATHER_WINDOW, d), index_map=lambda i: (i, 0))],
            core_axis_name=("core", "subcore"),
            dimension_semantics=(pltpu.PARALLEL,),
        )(idx_hbm, out_hbm)

    return gather_kernel(table, idx2)


def _expert_kernel(te_ref, nu_ref, nv_ref, slot_ref, slot_prev_ref, x_ref,
                   wgu_ref, wd_ref, perm_ref, bg_ref, bu_ref, bd_ref, y_hbm,
                   ybuf0, ybuf1, wg_s, wu_s, wd_s, out_sem, *, bm):
    i = pl.program_id(0)
    n_used = nu_ref[0]
    new_expert = (i == 0) | (te_ref[i] != te_ref[jnp.maximum(i - 1, 0)])
    ybufs = (ybuf0, ybuf1)

    def token_tile(ref, r):
        return ref.at[pl.ds(pl.multiple_of(r * N_SEG, N_SEG), N_SEG)]

    def wait_scatter(s, n_rows):
        n = pl.multiple_of(n_rows * N_SEG, N_SEG)
        pltpu.make_async_copy(ybufs[s].at[pl.ds(0, n)], y_hbm.at[pl.ds(0, n)], out_sem.at[s]).wait()

    n_prev = jnp.where(i >= 1, nv_ref[jnp.maximum(i - 1, 0)], 0)

    def scatter_row(ids_ref, s, r):
        pltpu.make_async_copy(token_tile(ybufs[s], r), token_tile(y_hbm, ids_ref[r]),
                              out_sem.at[s]).start()

    def scatter_loop(ids_ref, s, n_rows):
        def one(r, carry):
            scatter_row(ids_ref, s, r)
            return carry

        def group(j, carry):
            for q in range(DMA_UNROLL):
                scatter_row(ids_ref, s, j * DMA_UNROLL + q)
            return carry
        n_groups = n_rows // DMA_UNROLL
        lax.fori_loop(0, n_groups, group, 0)
        lax.fori_loop(n_groups * DMA_UNROLL, n_rows, one, 0)

    def tile_body(cur, prev_full):
        nxt = 1 - cur

        @pl.when(i >= 2)
        def _():
            wait_scatter(cur, nv_ref[i - 2])

        if not prev_full:
            scatter_loop(slot_prev_ref, nxt, n_prev)

        @pl.when(new_expert)
        def _():
            half = PERM_BLOCK // 2
            for j in range(2 * D_EXPERT // PERM_BLOCK):
                blk = wgu_ref[0, :, j * PERM_BLOCK:(j + 1) * PERM_BLOCK].astype(BF16)
                split = jnp.dot(blk, perm_ref[...], preferred_element_type=F32)
                wg_s[:, j * half:(j + 1) * half] = split[:, :half].astype(BF16)
                wu_s[:, j * half:(j + 1) * half] = split[:, half:].astype(BF16)
            wd_s[...] = wd_ref[0].astype(BF16)

        n_grp = bm // DMA_GROUPS

        def dma_group(gi):
            if prev_full:
                for r in range(gi * n_grp, (gi + 1) * n_grp):
                    scatter_row(slot_prev_ref, nxt, r)

        dma_group(0)
        x = jnp.concatenate(_unpack_bf16_pair(x_ref[...]), axis=1)
        g = jnp.dot(x, wg_s[...], preferred_element_type=F32) + bg_ref[0]
        dma_group(1)
        u = jnp.dot(x, wu_s[...], preferred_element_type=F32) + bu_ref[0]
        dma_group(2)
        gate = jnp.minimum(g, SWIGLU_LIMIT)
        up = jnp.clip(u, -SWIGLU_LIMIT, SWIGLU_LIMIT)
        act = gate * _sigmoid(SWIGLU_ALPHA * gate) * (up + 1.0)
        dma_group(3)
        y = jnp.dot(act.astype(BF16), wd_s[...], preferred_element_type=F32) + bd_ref[0]
        _store_token_tiles(ybufs[cur], y, bm)

        @pl.when(i == n_used - 1)
        def _():
            scatter_loop(slot_ref, cur, nv_ref[i])
            wait_scatter(cur, nv_ref[i])

            @pl.when(i >= 1)
            def _():
                wait_scatter(nxt, nv_ref[i - 1])

    for parity in range(2):
        for prev_full in (False, True):
            full = (n_prev == bm) if prev_full else (n_prev != bm)
            cond = (i < n_used) & (lax.rem(i, 2) == parity) & full
            pl.when(cond)(functools.partial(tile_body, parity, prev_full))


def _split_permutation():
    half = PERM_BLOCK // 2
    src = jnp.arange(PERM_BLOCK)[:, None]
    dst = jnp.arange(PERM_BLOCK)[None, :]
    return (src == jnp.where(dst < half, 2 * dst, 2 * (dst - half) + 1)).astype(BF16)


def _experts(tile_expert, n_used, tile_valid, row_slot, xs, n_slots, wgu, wd, bg, bu, bd, bm):
    n_tiles = xs.shape[0] // bm
    cur = lambda i, te, nu, nv: (jnp.minimum(i, nu[0] - 1),)
    prv = lambda i, te, nu, nv: (jnp.clip(i - 1, 0, nu[0] - 1),)
    row_map = lambda i, te, nu, nv: (jnp.minimum(i, nu[0] - 1), 0)
    w_map = lambda i, te, nu, nv: (te[i], 0, 0)
    grid_spec = pltpu.PrefetchScalarGridSpec(
        num_scalar_prefetch=3,
        grid=(n_tiles,),
        in_specs=[pl.BlockSpec((bm,), cur, memory_space=pltpu.SMEM),
                  pl.BlockSpec((bm,), prv, memory_space=pltpu.SMEM),
                  pl.BlockSpec((bm, D_MODEL // 2), row_map),
                  pl.BlockSpec((1, D_MODEL, 2 * D_EXPERT), w_map),
                  pl.BlockSpec((1, D_EXPERT, D_MODEL), w_map),
                  pl.BlockSpec((PERM_BLOCK, PERM_BLOCK), lambda i, te, nu, nv: (0, 0)),
                  pl.BlockSpec((1, 1, D_EXPERT), w_map),
                  pl.BlockSpec((1, 1, D_EXPERT), w_map),
                  pl.BlockSpec((1, 1, D_MODEL), w_map)],
        out_specs=pl.BlockSpec(memory_space=pl.ANY),
        scratch_shapes=[pltpu.VMEM((bm * N_SEG, LANES), F32), pltpu.VMEM((bm * N_SEG, LANES), F32),
                        pltpu.VMEM((D_MODEL, D_EXPERT), BF16), pltpu.VMEM((D_MODEL, D_EXPERT), BF16),
                        pltpu.VMEM((D_EXPERT, D_MODEL), BF16),
                        pltpu.SemaphoreType.DMA((2,))],
    )
    return pl.pallas_call(
        functools.partial(_expert_kernel, bm=bm),
        grid_spec=grid_spec,
        out_shape=jax.ShapeDtypeStruct((n_slots * N_SEG, LANES), F32),
        compiler_params=pltpu.CompilerParams(dimension_semantics=("arbitrary",),
                                             vmem_limit_bytes=VMEM_LIMIT,
                                             has_side_effects=True),
        name="experts",
    )(tile_expert, n_used, tile_valid, row_slot, row_slot, xs, wgu, wd, _split_permutation(),
      bg, bu, bd)


def _combine_kernel(y0_ref, y1_ref, y2_ref, y3_ref, gate_ref, x1_ref, g_ref, o_ref, *, tm):
    acc = x1_ref[...]
    for k, y_ref in enumerate((y0_ref, y1_ref, y2_ref, y3_ref)):
        acc = acc + gate_ref[:, k:k + 1] * _load_token_tiles(y_ref, tm)
    o_ref[...] = _rms(acc, g_ref[...])


def _combine(y_slots, gates, x1, g, tm):
    t = x1.shape[0]
    nb = t // tm

    def slot_spec(k):
        return pl.BlockSpec((tm * N_SEG, LANES), lambda i: (k * nb + i, 0))

    return pl.pallas_call(
        functools.partial(_combine_kernel, tm=tm),
        grid=(nb,),
        in_specs=[slot_spec(0), slot_spec(1), slot_spec(2), slot_spec(3),
                  pl.BlockSpec((tm, TOP_K), lambda i: (i, 0)),
                  pl.BlockSpec((tm, D_MODEL), lambda i: (i, 0)),
                  pl.BlockSpec((1, D_MODEL), lambda i: (0, 0))],
        out_specs=pl.BlockSpec((tm, D_MODEL), lambda i: (i, 0)),
        out_shape=jax.ShapeDtypeStruct((t, D_MODEL), F32),
        compiler_params=pltpu.CompilerParams(dimension_semantics=("arbitrary",),
                                             vmem_limit_bytes=VMEM_LIMIT),
        name="combine",
    )(y_slots, y_slots, y_slots, y_slots, gates, x1, g)


def _lane_row(vec, lane0):
    n = vec.shape[0]
    return jnp.zeros((8, LANES), F32).at[0, lane0:lane0 + n].set(vec.astype(F32))


def _tile_size(n, pref):
    return pref if n % pref == 0 else n


def kernel(x, norm_mix, w_in, gdn_conv, gdn_a_log, gdn_dt_bias, gdn_norm, ml_conv, ml_b_i, ml_b_f,
           ml_norm, w_up_gdn, w_up_ml, w_out, norm_ffn, w_router, b_router, w_gate_up, b_gate_up,
           w_down, b_down, norm_final):
    assert norm_mix.shape[0] == 1, "single-layer stack"
    b, s, d = x.shape
    assert d == D_MODEL and s % CHUNK == 0
    t = b * s
    x2 = x.reshape(t, d)

    w = w_in[0]
    w_big = jnp.concatenate([_cols(w, n) for n in ("g_q", "g_k", "g_v", "g_z", "m_q", "m_k", "m_v",
                                                   "m_o", "gate_gdn", "gate_ml")], axis=1).astype(BF16)
    w_small = jnp.concatenate([_cols(w, n) for n in ("g_a", "g_b", "m_i", "m_f")], axis=1)
    w_small = jnp.pad(w_small, ((0, 0), (0, LANES - N_SMALL))).astype(BF16)

    tm = _tile_size(t, 512)
    big, small = _in_proj(x2, norm_mix[0][None, :], w_big, w_small, tm)
    big3 = big.reshape(b, s, N_BIG)
    small3 = small.reshape(b, s, LANES)

    cs = _tile_size(s, 512)
    gdn_prm = _lane_row(gdn_a_log[0], 0).at[1, 0:N_HEADS].set(gdn_dt_bias[0].astype(F32))
    oa = _gdn(big3, small3, gdn_conv[0].astype(F32), gdn_prm, gdn_norm[0][None, :].astype(F32), cs)
    ml_prm = _lane_row(ml_b_i[0], 2 * N_HEADS).at[0, 3 * N_HEADS:4 * N_HEADS].set(ml_b_f[0].astype(F32))
    ob = _mlstm(big3, small3, ml_conv[0].astype(F32), ml_prm, ml_norm[0][None, :].astype(F32), cs)

    w_r = jnp.pad(w_router[0], ((0, 0), (0, LANES - N_EXPERTS))).astype(BF16)
    b_r = jnp.full((1, LANES), -1e30, F32).at[0, :N_EXPERTS].set(b_router[0].astype(F32))
    x1, h2, idx, gates, rank, counts = _merge(
        oa.reshape(t, D_QK), ob.reshape(t, D_QK), big, x2, w_up_gdn[0].astype(BF16),
        w_up_ml[0].astype(BF16), w_out[0].astype(BF16), norm_ffn[0][None, :], w_r, b_r, tm)

    bm = EXPERT_TILE
    n_assign = t * TOP_K
    n_tiles = -(-n_assign // bm) + N_EXPERTS
    counts = counts[0, :N_EXPERTS]
    padded = (counts + bm - 1) // bm * bm
    pend = jnp.cumsum(padded)
    pstart = pend - padded
    dest = (pstart[idx] + rank).reshape(-1)
    n_used = (pend[-1] // bm).astype(I32)
    tile_ids = jnp.minimum(jnp.arange(n_tiles, dtype=I32), n_used - 1)
    tile_expert = jnp.minimum(jnp.sum((pend[None, :] <= (tile_ids * bm)[:, None]).astype(I32), axis=1),
                              N_EXPERTS - 1)
    tile_valid = jnp.clip(counts[tile_expert] - (tile_ids * bm - pstart[tile_expert]), 0, bm).astype(I32)
    assign = jnp.arange(n_assign, dtype=I32)
    pads = padded - counts
    cpad = jnp.cumsum(pads)
    j = jnp.arange(n_tiles * bm - n_assign, dtype=I32)
    pe = jnp.minimum(jnp.sum((cpad[None, :] <= j[:, None]).astype(I32), axis=1), N_EXPERTS - 1)
    pad_row = jnp.where(j < cpad[-1], pstart[pe] + counts[pe] + j - (cpad[pe] - pads[pe]),
                        pend[-1] + j - cpad[-1])
    _, row_slot = lax.sort((jnp.concatenate([dest, pad_row]).astype(I32),
                            jnp.concatenate([(assign % TOP_K) * t + assign // TOP_K, jnp.zeros_like(j)])),
                           num_keys=1)
    row_tok = row_slot % t

    bgu = b_gate_up[0]
    xs = _gather_rows(h2, row_tok)
    y_slots = _experts(tile_expert.astype(I32), n_used.reshape(1), tile_valid, row_slot, xs, n_assign,
                       w_gate_up[0], w_down[0],
                       bgu[:, None, 0::2].astype(F32), bgu[:, None, 1::2].astype(F32),
                       b_down[0][:, None, :].astype(F32), bm)
    out = _combine(y_slots, gates, x1, norm_final[None, :], _tile_size(t, 256))
    return out.reshape(b, s, d)
```

```python
import functools

import jax
import jax.numpy as jnp
from jax import lax
from jax.experimental import pallas as pl
from jax.experimental.pallas import tpu as pltpu

F32 = jnp.float32
BF16 = jnp.bfloat16
I32 = jnp.int32

D_MODEL = 1024
N_HEADS = 4
D_HEAD = 128
D_QK = N_HEADS * D_HEAD
CONV_WIDTH = 4
CHUNK = 64
GATE_SOFTCAP = 15.0
N_EXPERTS = 32
TOP_K = 4
D_EXPERT = 1024
SWIGLU_LIMIT = 7.0
SWIGLU_ALPHA = 1.702
NORM_EPS = 1e-6

LANES = 128
SUBLANES = 8
N_SMALL = 16
N_BIG = 6 * D_MODEL
VMEM_LIMIT = 56 * 1024 * 1024
EXPERT_TILE = 512
DMA_UNROLL = 8
DMA_GROUPS = 4
PERM_BLOCK = 256
LOCAL_CHUNKS = 4

_OFF = {}
_o = 0
for _name, _w in (("g_q", D_QK), ("g_k", D_QK), ("g_v", D_QK), ("g_z", D_QK), ("g_a", N_HEADS),
                  ("g_b", N_HEADS), ("m_q", D_QK), ("m_k", D_QK), ("m_v", D_QK), ("m_o", D_QK),
                  ("m_i", N_HEADS), ("m_f", N_HEADS), ("gate_gdn", D_MODEL), ("gate_ml", D_MODEL)):
    _OFF[_name] = (_o, _w)
    _o += _w


def _cols(w, name):
    o, n = _OFF[name]
    return w[:, o:o + n]


_NN = (((1,), (0,)), ((), ()))
_NT = (((1,), (1,)), ((), ()))
_TN = (((0,), (0,)), ((), ()))


def _bdot(a, b, dims=_NN):
    return lax.dot_general(a.astype(BF16), b.astype(BF16), dims, preferred_element_type=F32)


def _split3(a):
    hi = a.astype(BF16)
    r = a - hi.astype(F32)
    mid = r.astype(BF16)
    return hi, mid, (r - mid.astype(F32)).astype(BF16)


def _select_dot(sel01, x, dims=_NN):
    s = sel01.astype(BF16)
    h, m, l = _split3(x)
    d = functools.partial(lax.dot_general, dimension_numbers=dims, preferred_element_type=F32)
    return d(s, h) + (d(s, m) + d(s, l))


def _sigmoid(x):
    return 1.0 / (1.0 + jnp.exp(-x))


def _silu(x):
    return x * _sigmoid(x)


def _softplus(x):
    return jnp.maximum(x, 0.0) + jnp.log(1.0 + jnp.exp(-jnp.abs(x)))


def _rms(x, g):
    return x * lax.rsqrt(jnp.mean(x * x, axis=-1, keepdims=True) + NORM_EPS) * g


def _in_proj_kernel(x_ref, g_ref, wbig_ref, wsmall_ref, big_ref, small_ref, *, n_chunk):
    h = _rms(x_ref[...], g_ref[...]).astype(BF16)
    for c in range(N_BIG // n_chunk):
        sl = slice(c * n_chunk, (c + 1) * n_chunk)
        big_ref[:, sl] = jnp.dot(h, wbig_ref[:, sl], preferred_element_type=F32)
    small_ref[...] = jnp.dot(h, wsmall_ref[...], preferred_element_type=F32)


def _in_proj(x2, g, w_big, w_small, tm):
    t = x2.shape[0]
    return pl.pallas_call(
        functools.partial(_in_proj_kernel, n_chunk=1024),
        grid=(t // tm,),
        in_specs=[pl.BlockSpec((tm, D_MODEL), lambda i: (i, 0)),
                  pl.BlockSpec((1, D_MODEL), lambda i: (0, 0)),
                  pl.BlockSpec((D_MODEL, N_BIG), lambda i: (0, 0)),
                  pl.BlockSpec((D_MODEL, LANES), lambda i: (0, 0))],
        out_specs=[pl.BlockSpec((tm, N_BIG), lambda i: (i, 0)),
                   pl.BlockSpec((tm, LANES), lambda i: (i, 0))],
        out_shape=[jax.ShapeDtypeStruct((t, N_BIG), F32),
                   jax.ShapeDtypeStruct((t, LANES), F32)],
        compiler_params=pltpu.CompilerParams(dimension_semantics=("arbitrary",),
                                             vmem_limit_bytes=VMEM_LIMIT),
        name="in_proj",
    )(x2, g, w_big, w_small)


def _chunk_masks():
    r = lax.broadcasted_iota(I32, (CHUNK, CHUNK), 0)
    c = lax.broadcasted_iota(I32, (CHUNK, CHUNK), 1)
    return r > c, r >= c


def _block_cumsum_matrix(cs):
    r = lax.broadcasted_iota(I32, (cs, cs), 0)
    c = lax.broadcasted_iota(I32, (cs, cs), 1)
    same = (r // CHUNK) == (c // CHUNK)
    return jnp.where(same & (r >= c), 1.0, 0.0).astype(F32)


def _lane_onehot(lane):
    return jnp.where(lax.broadcasted_iota(I32, (CHUNK, LANES), 1) == lane, 1.0, 0.0).astype(F32)


def _conv_silu_pieces(x_ref, w_ref, buf_ref, cs, width):
    @pl.when(pl.program_id(1) == 0)
    def _():
        buf_ref[0:8, :] = jnp.zeros((8, buf_ref.shape[1]), F32)

    buf_ref[8:8 + cs, :] = x_ref[0]
    for r0 in range(0, cs, CHUNK):
        for c0 in range(0, buf_ref.shape[1], width):
            cols = slice(c0, c0 + width)
            acc = w_ref[CONV_WIDTH - 1:CONV_WIDTH, cols] * buf_ref[8 + r0:8 + r0 + CHUNK, cols]
            for j in range(CONV_WIDTH - 1):
                s = 8 - (CONV_WIDTH - 1) + j + r0
                acc = acc + w_ref[j:j + 1, cols] * buf_ref[s:s + CHUNK, cols]
            yield r0, c0, _silu(acc)
    buf_ref[0:8, :] = buf_ref[cs:cs + 8, :]


def _gdn_kernel(qkv_ref, z_ref, sm_ref, conv_ref, prm_ref, gn_ref, o_ref,
                buf_ref, q_s, k_s, v_s, w_s, attn_s, gam_s, beta_s, state_s, *, cs):
    n_chunks = cs // CHUNK

    @pl.when(pl.program_id(1) == 0)
    def _():
        state_s[...] = jnp.zeros(state_s.shape, F32)

    for r0, c0, piece in _conv_silu_pieces(qkv_ref, conv_ref, buf_ref, cs, D_QK):
        rows = slice(r0, r0 + CHUNK)
        if c0 == 2 * D_QK:
            v_s[rows, :] = piece
            continue
        dst, scale = (q_s, D_HEAD ** -0.5) if c0 == 0 else (k_s, 1.0)
        for h in range(N_HEADS):
            sl = slice(h * D_HEAD, (h + 1) * D_HEAD)
            a = piece[:, sl]
            dst[rows, sl] = a * (lax.rsqrt(jnp.sum(a * a, axis=-1, keepdims=True) + NORM_EPS) * scale)

    sm = sm_ref[0]
    logdec = -jnp.exp(prm_ref[0:1, :]) * _softplus(sm + prm_ref[1:2, :])
    gam_s[...] = _select_dot(_block_cumsum_matrix(cs), logdec)
    beta_s[...] = _sigmoid(sm)

    strict, incl = _chunk_masks()
    eye = jnp.where(lax.broadcasted_iota(I32, (CHUNK, CHUNK), 0)
                    == lax.broadcasted_iota(I32, (CHUNK, CHUNK), 1), 1.0, 0.0).astype(F32)
    gnorm = gn_ref[...]

    heads = range(N_HEADS)
    hsl = [slice(h * D_HEAD, (h + 1) * D_HEAD) for h in heads]
    asl = [slice(h * CHUNK, (h + 1) * CHUNK) for h in heads]
    onehots = [_lane_onehot(h) for h in heads]

    def local_body(it, carry):
        rows = [pl.ds(pl.multiple_of((it * LOCAL_CHUNKS + ci) * CHUNK, CHUNK), CHUNK)
                for ci in range(LOCAL_CHUNKS)]
        probs = [(ci, h) for ci in range(LOCAL_CHUNKS) for h in heads]
        gam_all = [gam_s[r, :] for r in rows]
        beta_all = [beta_s[r, :] for r in rows]
        q = [q_s[rows[ci], hsl[h]] for ci, h in probs]
        k = [k_s[rows[ci], hsl[h]] for ci, h in probs]
        v = [v_s[rows[ci], hsl[h]] for ci, h in probs]
        gc = [gam_all[ci][:, h:h + 1] for ci, h in probs]
        beta = [beta_all[ci][:, N_HEADS + h:N_HEADS + h + 1] for ci, h in probs]
        g_row = [_select_dot(onehots[h], gam_all[ci], _NT) for ci, h in probs]
        kb = [a * b for a, b in zip(k, beta)]
        kk = [_bdot(a, b, _NT) for a, b in zip(kb, k)]
        qk = [_bdot(a, b, _NT) for a, b in zip(q, k)]
        decay = [jnp.where(incl, jnp.exp(jnp.minimum(a - b, 0.0)), 0.0) for a, b in zip(gc, g_row)]
        lower = [jnp.where(strict, a * b, 0.0) for a, b in zip(kk, decay)]
        inv = [eye - a for a in lower]
        pw = [_bdot(a, a) for a in lower]
        for lvl in range(5):
            upd = [_bdot(a, b) for a, b in zip(inv, pw)]
            if lvl < 4:
                pw = [_bdot(a, a) for a in pw]
            inv = [a + b for a, b in zip(inv, upd)]
        e_gc = [jnp.exp(a) for a in gc]
        rhs = [jnp.concatenate([a * b, c * d], axis=1) for a, b, c, d in zip(v, beta, kb, e_gc)]
        sol = [_bdot(a, b) for a, b in zip(inv, rhs)]
        for p, (ci, h) in enumerate(probs):
            v_s[rows[ci], hsl[h]] = sol[p][:, :D_HEAD]
            w_s[rows[ci], hsl[h]] = sol[p][:, D_HEAD:]
            attn_s[rows[ci], asl[h]] = qk[p] * decay[p]
            q_s[rows[ci], hsl[h]] = q[p] * e_gc[p]
            k_s[rows[ci], hsl[h]] = k[p] * jnp.exp(gc[p][CHUNK - 1:CHUNK, :] - gc[p])
        return carry

    lax.fori_loop(0, n_chunks // LOCAL_CHUNKS, local_body, 0)

    def state_body(c, carry):
        rows = pl.ds(pl.multiple_of(c * CHUNK, CHUNK), CHUNK)
        g_last = gam_s[pl.ds(c * CHUNK + CHUNK - 1, 1), :]
        st = [state_s[h] for h in heads]
        ws = [_bdot(w_s[rows, hsl[h]], st[h]) for h in heads]
        qs = [_bdot(q_s[rows, hsl[h]], st[h]) for h in heads]
        u = [v_s[rows, hsl[h]] - ws[h] for h in heads]
        au = [_bdot(attn_s[rows, asl[h]], u[h]) for h in heads]
        ku = [_bdot(k_s[rows, hsl[h]], u[h], _TN) for h in heads]
        for h in heads:
            state_s[h] = st[h] * jnp.exp(g_last[:, h:h + 1]) + ku[h]
            o = qs[h] + au[h]
            o = o * lax.rsqrt(jnp.mean(o * o, axis=-1, keepdims=True) + NORM_EPS) * gnorm
            o_ref[0, rows, hsl[h]] = (o * _silu(z_ref[0, rows, hsl[h]])).astype(o_ref.dtype)
        return carry

    lax.fori_loop(0, n_chunks, state_body, 0)


def _gdn(big3, small3, conv_w, prm, gnorm, cs):
    b, s, _ = big3.shape
    return pl.pallas_call(
        functools.partial(_gdn_kernel, cs=cs),
        grid=(b, s // cs),
        in_specs=[pl.BlockSpec((1, cs, 3 * D_QK), lambda i, j: (i, j, 0)),
                  pl.BlockSpec((1, cs, D_QK), lambda i, j: (i, j, 3)),
                  pl.BlockSpec((1, cs, LANES), lambda i, j: (i, j, 0)),
                  pl.BlockSpec((CONV_WIDTH, 3 * D_QK), lambda i, j: (0, 0)),
                  pl.BlockSpec((8, LANES), lambda i, j: (0, 0)),
                  pl.BlockSpec((1, D_HEAD), lambda i, j: (0, 0))],
        out_specs=pl.BlockSpec((1, cs, D_QK), lambda i, j: (i, j, 0)),
        out_shape=jax.ShapeDtypeStruct((b, s, D_QK), BF16),
        scratch_shapes=[pltpu.VMEM((cs + 8, 3 * D_QK), F32),
                        pltpu.VMEM((cs, D_QK), F32), pltpu.VMEM((cs, D_QK), F32),
                        pltpu.VMEM((cs, D_QK), F32), pltpu.VMEM((cs, D_QK), F32),
                        pltpu.VMEM((cs, N_HEADS * CHUNK), F32),
                        pltpu.VMEM((cs, LANES), F32), pltpu.VMEM((cs, LANES), F32),
                        pltpu.VMEM((N_HEADS, D_HEAD, D_HEAD), F32)],
        compiler_params=pltpu.CompilerParams(dimension_semantics=("arbitrary", "arbitrary"),
                                             vmem_limit_bytes=VMEM_LIMIT),
        name="gdn",
    )(big3, big3, small3, conv_w, prm, gnorm)


def _mlstm_kernel(qk_ref, v_ref, og_ref, sm_ref, conv_ref, prm_ref, gn_ref, o_ref,
                  buf_ref, q_s, k_s, bcum_s, ipre_s, state_s, m_s, *, cs):
    n_chunks = cs // CHUNK

    @pl.when(pl.program_id(1) == 0)
    def _():
        state_s[...] = jnp.zeros(state_s.shape, F32)
        m_s[...] = jnp.zeros(m_s.shape, F32)

    for r0, c0, piece in _conv_silu_pieces(qk_ref, conv_ref, buf_ref, cs, D_QK):
        if c0 == 0:
            q_s[r0:r0 + CHUNK, :] = piece
        else:
            k_s[r0:r0 + CHUNK, :] = piece * (D_HEAD ** -0.5)

    pre = sm_ref[0] + prm_ref[0:1, :]
    capped = GATE_SOFTCAP * jnp.tanh(pre / GATE_SOFTCAP)
    logf = -_softplus(-capped)
    bcum_s[...] = _select_dot(_block_cumsum_matrix(cs), logf)
    ipre_s[...] = pltpu.roll(capped, N_HEADS, axis=1)

    _, incl = _chunk_masks()
    ones_aug = jnp.ones((CHUNK, D_HEAD), F32)

    heads = range(N_HEADS)
    hsl = [slice(h * D_HEAD, (h + 1) * D_HEAD) for h in heads]
    lanes = [3 * N_HEADS + h for h in heads]
    onehots = [_lane_onehot(lane) for lane in lanes]

    def chunk_body(c, carry):
        rows = pl.ds(pl.multiple_of(c * CHUNK, CHUNK), CHUNK)
        b_all = bcum_s[rows, :]
        comb_all = ipre_s[rows, :] - b_all
        q = [q_s[rows, hsl[h]] for h in heads]
        k = [k_s[rows, hsl[h]] for h in heads]
        v_aug = [jnp.concatenate([v_ref[0, rows, hsl[h]], ones_aug], axis=1) for h in heads]
        st = [state_s[h] for h in heads]
        m_st = [m_s[h] for h in heads]
        bc = [b_all[:, lane:lane + 1] for lane in lanes]
        comb_row = [_select_dot(onehots[h], comb_all, _NT) for h in heads]
        qk = [_bdot(q[h], k[h], _NT) for h in heads]
        inter = [_bdot(q[h], st[h]) for h in heads]
        b_last = [bc[h][CHUNK - 1:CHUNK, :] for h in heads]
        a_log = [b_last[h] + comb_all[:, lanes[h]:lanes[h] + 1] for h in heads]
        m_chunk = [jnp.max(a_log[h], axis=0, keepdims=True) for h in heads]
        wk = [k[h] * jnp.exp(a_log[h] - m_chunk[h]) for h in heads]
        d_state = [_bdot(wk[h], v_aug[h], _TN) for h in heads]
        d_log = [jnp.where(incl, bc[h] + comb_row[h], -jnp.inf) for h in heads]
        m_loc = [jnp.max(d_log[h], axis=-1, keepdims=True) for h in heads]
        p = [jnp.exp(d_log[h] - m_loc[h]) * qk[h] for h in heads]
        loc = [_bdot(p[h], v_aug[h]) for h in heads]
        for h in heads:
            m_inter = bc[h] + m_st[h]
            m_t = jnp.maximum(m_loc[h], m_inter)
            tot = jnp.exp(m_inter - m_t) * inter[h] + jnp.exp(m_loc[h] - m_t) * loc[h]
            hh = tot[:, :D_HEAD] / jnp.maximum(jnp.abs(tot[:, D_HEAD:]), jnp.exp(-m_t))
            m_new = jnp.maximum(b_last[h] + m_st[h], m_chunk[h])
            state_s[h] = (jnp.exp(b_last[h] + m_st[h] - m_new) * st[h]
                          + jnp.exp(m_chunk[h] - m_new) * d_state[h])
            m_s[h] = m_new
            hh = hh * lax.rsqrt(jnp.mean(hh * hh, axis=-1, keepdims=True) + NORM_EPS) * gn_ref[:, hsl[h]]
            o_ref[0, rows, hsl[h]] = (hh * _sigmoid(og_ref[0, rows, hsl[h]])).astype(o_ref.dtype)
        return carry

    lax.fori_loop(0, n_chunks, chunk_body, 0)


def _mlstm(big3, small3, conv_w, prm, gnorm, cs):
    b, s, _ = big3.shape
    return pl.pallas_call(
        functools.partial(_mlstm_kernel, cs=cs),
        grid=(b, s // cs),
        in_specs=[pl.BlockSpec((1, cs, 2 * D_QK), lambda i, j: (i, j, 2)),
                  pl.BlockSpec((1, cs, D_QK), lambda i, j: (i, j, 6)),
                  pl.BlockSpec((1, cs, D_QK), lambda i, j: (i, j, 7)),
                  pl.BlockSpec((1, cs, LANES), lambda i, j: (i, j, 0)),
                  pl.BlockSpec((CONV_WIDTH, 2 * D_QK), lambda i, j: (0, 0)),
                  pl.BlockSpec((8, LANES), lambda i, j: (0, 0)),
                  pl.BlockSpec((1, D_QK), lambda i, j: (0, 0))],
        out_specs=pl.BlockSpec((1, cs, D_QK), lambda i, j: (i, j, 0)),
        out_shape=jax.ShapeDtypeStruct((b, s, D_QK), BF16),
        scratch_shapes=[pltpu.VMEM((cs + 8, 2 * D_QK), F32),
                        pltpu.VMEM((cs, D_QK), F32), pltpu.VMEM((cs, D_QK), F32),
                        pltpu.VMEM((cs, LANES), F32), pltpu.VMEM((cs, LANES), F32),
                        pltpu.VMEM((N_HEADS, D_HEAD, 2 * D_HEAD), F32),
                        pltpu.VMEM((N_HEADS, 1, 1), F32)],
        compiler_params=pltpu.CompilerParams(dimension_semantics=("arbitrary", "arbitrary"),
                                             vmem_limit_bytes=VMEM_LIMIT),
        name="mlstm",
    )(big3, big3, big3, small3, conv_w, prm, gnorm)


N_SEG = D_MODEL // LANES
assert N_SEG == SUBLANES


def _store_token_tiles(ref, val, n):
    for s in range(N_SEG):
        ref[pl.ds(s, n, stride=N_SEG), :] = val[:, s * LANES:(s + 1) * LANES]


def _load_token_tiles(ref, n):
    return jnp.concatenate([ref[pl.ds(s, n, stride=N_SEG), :] for s in range(N_SEG)], axis=1)


def _merge_kernel(oa_ref, ob_ref, gates_ref, x_ref, wa_ref, wb_ref, wo_ref, g_ref, wr_ref, br_ref,
                  x1_ref, h2_ref, idx_ref, gate_ref, rank_ref, cnt_ref, carry_s, *, tm):
    @pl.when(pl.program_id(0) == 0)
    def _():
        carry_s[...] = jnp.zeros(carry_s.shape, F32)

    y_a = jnp.dot(oa_ref[...], wa_ref[...], preferred_element_type=F32)
    y_b = jnp.dot(ob_ref[...], wb_ref[...], preferred_element_type=F32)
    y = _sigmoid(gates_ref[:, :D_MODEL]) * y_a + _sigmoid(gates_ref[:, D_MODEL:]) * y_b
    x1 = x_ref[...] + jnp.dot(y.astype(BF16), wo_ref[...], preferred_element_type=F32)
    x1_ref[...] = x1
    h2 = _rms(x1, g_ref[...])
    _store_token_tiles(h2_ref, h2, tm)
    logits = jnp.dot(h2.astype(BF16), wr_ref[...], preferred_element_type=F32) + br_ref[...]

    lane = lax.broadcasted_iota(I32, (tm, LANES), 1).astype(F32)
    vals, sels = [], []
    idx_t = jnp.zeros((tm, LANES), F32)
    work = logits
    for k in range(TOP_K):
        m = jnp.max(work, axis=-1, keepdims=True)
        i = jnp.min(jnp.where(work == m, lane, float(LANES)), axis=-1, keepdims=True)
        sel = lane == i
        work = jnp.where(sel, -jnp.inf, work)
        idx_t = jnp.where(lane == float(k), i, idx_t)
        vals.append(m)
        sels.append(sel)
    es = [jnp.exp(v - vals[0]) for v in vals]
    denom = es[0] + es[1] + es[2] + es[3]
    gate_t = jnp.zeros((tm, LANES), F32)
    for k in range(TOP_K):
        gate_t = jnp.where(lane == float(k), es[k] / denom, gate_t)
    idx_ref[...] = idx_t[:, :TOP_K].astype(I32)
    gate_ref[...] = gate_t[:, :TOP_K]

    onehot = jnp.zeros((tm, LANES), F32)
    for sel in sels:
        onehot = onehot + jnp.where(sel, 1.0, 0.0)
    r = lax.broadcasted_iota(I32, (tm, tm), 0)
    c = lax.broadcasted_iota(I32, (tm, tm), 1)
    before = jnp.where(r > c, 1.0, 0.0).astype(BF16)
    cum = jnp.dot(before, onehot.astype(BF16), preferred_element_type=F32) + carry_s[...]
    rank_t = jnp.zeros((tm, LANES), F32)
    for k, sel in enumerate(sels):
        rk = jnp.sum(jnp.where(sel, cum, 0.0), axis=-1, keepdims=True)
        rank_t = jnp.where(lane == float(k), rk, rank_t)
    rank_ref[...] = rank_t[:, :TOP_K].astype(I32)
    total = carry_s[...] + jnp.sum(onehot, axis=0, keepdims=True)
    carry_s[...] = total
    cnt_ref[...] = total.astype(I32)


def _merge(oa, ob, big, x2, wa, wb, wo, g, wr, br, tm):
    t = x2.shape[0]
    const = lambda i: (0, 0)
    return pl.pallas_call(
        functools.partial(_merge_kernel, tm=tm),
        grid=(t // tm,),
        in_specs=[pl.BlockSpec((tm, D_QK), lambda i: (i, 0)),
                  pl.BlockSpec((tm, D_QK), lambda i: (i, 0)),
                  pl.BlockSpec((tm, 2 * D_MODEL), lambda i: (i, 2)),
                  pl.BlockSpec((tm, D_MODEL), lambda i: (i, 0)),
                  pl.BlockSpec((D_QK, D_MODEL), const),
                  pl.BlockSpec((D_QK, D_MODEL), const),
                  pl.BlockSpec((D_MODEL, D_MODEL), const),
                  pl.BlockSpec((1, D_MODEL), const),
                  pl.BlockSpec((D_MODEL, LANES), const),
                  pl.BlockSpec((1, LANES), const)],
        out_specs=[pl.BlockSpec((tm, D_MODEL), lambda i: (i, 0)),
                   pl.BlockSpec((tm * N_SEG, LANES), lambda i: (i, 0)),
                   pl.BlockSpec((tm, TOP_K), lambda i: (i, 0)),
                   pl.BlockSpec((tm, TOP_K), lambda i: (i, 0)),
                   pl.BlockSpec((tm, TOP_K), lambda i: (i, 0)),
                   pl.BlockSpec((1, LANES), const)],
        out_shape=[jax.ShapeDtypeStruct((t, D_MODEL), F32),
                   jax.ShapeDtypeStruct((t * N_SEG, LANES), F32),
                   jax.ShapeDtypeStruct((t, TOP_K), I32),
                   jax.ShapeDtypeStruct((t, TOP_K), F32),
                   jax.ShapeDtypeStruct((t, TOP_K), I32),
                   jax.ShapeDtypeStruct((1, LANES), I32)],
        scratch_shapes=[pltpu.VMEM((1, LANES), F32)],
        compiler_params=pltpu.CompilerParams(dimension_semantics=("arbitrary",),
                                             vmem_limit_bytes=VMEM_LIMIT),
        name="merge_router",
    )(oa, ob, big, x2, wa, wb, wo, g, wr, br)


def _expert_kernel(te_ref, nu_ref, nv_ref, tok_ref, tok_next_ref, slot_ref, slot_prev_ref, h2_hbm,
                   wgu_ref, wd_ref, perm_ref, bg_ref, bu_ref, bd_ref, y_hbm,
                   xbuf0, xbuf1, ybuf0, ybuf1, wg_s, wu_s, wd_s, in_sem, out_sem, *, bm):
    i = pl.program_id(0)
    n_used = nu_ref[0]
    new_expert = (i == 0) | (te_ref[i] != te_ref[jnp.maximum(i - 1, 0)])
    xbufs = (xbuf0, xbuf1)
    ybufs = (ybuf0, ybuf1)

    def token_tile(ref, r):
        return ref.at[pl.ds(pl.multiple_of(r * N_SEG, N_SEG), N_SEG)]

    def gather_row(ids_ref, s, r):
        pltpu.make_async_copy(token_tile(h2_hbm, ids_ref[r]), token_tile(xbufs[s], r),
                              in_sem.at[s]).start()

    def wait_gather(s):
        pltpu.make_async_copy(h2_hbm.at[pl.ds(0, bm * N_SEG)], xbufs[s], in_sem.at[s]).wait()

    def wait_scatter(s, n_rows):
        n = pl.multiple_of(n_rows * N_SEG, N_SEG)
        pltpu.make_async_copy(ybufs[s].at[pl.ds(0, n)], y_hbm.at[pl.ds(0, n)], out_sem.at[s]).wait()

    @pl.when(i == 0)
    def _():
        def body(r, carry):
            gather_row(tok_ref, 0, r)
            return carry
        lax.fori_loop(0, bm, body, 0, unroll=DMA_UNROLL)

    n_prev = jnp.where(i >= 1, nv_ref[jnp.maximum(i - 1, 0)], 0)

    def scatter_row(ids_ref, s, r):
        pltpu.make_async_copy(token_tile(ybufs[s], r), token_tile(y_hbm, ids_ref[r]),
                              out_sem.at[s]).start()

    def scatter_loop(ids_ref, s, n_rows):
        def one(r, carry):
            scatter_row(ids_ref, s, r)
            return carry

        def group(j, carry):
            for q in range(DMA_UNROLL):
                scatter_row(ids_ref, s, j * DMA_UNROLL + q)
            return carry
        n_groups = n_rows // DMA_UNROLL
        lax.fori_loop(0, n_groups, group, 0)
        lax.fori_loop(n_groups * DMA_UNROLL, n_rows, one, 0)

    def tile_body(cur, prev_full):
        nxt = 1 - cur
        wait_gather(cur)

        @pl.when(i >= 2)
        def _():
            wait_scatter(cur, nv_ref[i - 2])

        if not prev_full:
            scatter_loop(slot_prev_ref, nxt, n_prev)

        @pl.when(new_expert)
        def _():
            half = PERM_BLOCK // 2
            for j in range(2 * D_EXPERT // PERM_BLOCK):
                blk = wgu_ref[0, :, j * PERM_BLOCK:(j + 1) * PERM_BLOCK].astype(BF16)
                split = jnp.dot(blk, perm_ref[...], preferred_element_type=F32)
                wg_s[:, j * half:(j + 1) * half] = split[:, :half].astype(BF16)
                wu_s[:, j * half:(j + 1) * half] = split[:, half:].astype(BF16)
            wd_s[...] = wd_ref[0].astype(BF16)

        n_grp = bm // DMA_GROUPS

        def dma_group(gi):
            for r in range(gi * n_grp, (gi + 1) * n_grp):
                gather_row(tok_next_ref, nxt, r)
                if prev_full:
                    scatter_row(slot_prev_ref, nxt, r)

        dma_group(0)
        x = _load_token_tiles(xbufs[cur], bm).astype(BF16)
        g = jnp.dot(x, wg_s[...], preferred_element_type=F32) + bg_ref[0]
        dma_group(1)
        u = jnp.dot(x, wu_s[...], preferred_element_type=F32) + bu_ref[0]
        dma_group(2)
        gate = jnp.minimum(g, SWIGLU_LIMIT)
        up = jnp.clip(u, -SWIGLU_LIMIT, SWIGLU_LIMIT)
        act = gate * _sigmoid(SWIGLU_ALPHA * gate) * (up + 1.0)
        dma_group(3)
        y = jnp.dot(act.astype(BF16), wd_s[...], preferred_element_type=F32) + bd_ref[0]
        _store_token_tiles(ybufs[cur], y, bm)

        @pl.when(i == n_used - 1)
        def _():
            scatter_loop(slot_ref, cur, nv_ref[i])
            wait_gather(nxt)
            wait_scatter(cur, nv_ref[i])

            @pl.when(i >= 1)
            def _():
                wait_scatter(nxt, nv_ref[i - 1])

    for parity in range(2):
        for prev_full in (False, True):
            full = (n_prev == bm) if prev_full else (n_prev != bm)
            cond = (i < n_used) & (lax.rem(i, 2) == parity) & full
            pl.when(cond)(functools.partial(tile_body, parity, prev_full))


def _split_permutation():
    half = PERM_BLOCK // 2
    src = jnp.arange(PERM_BLOCK)[:, None]
    dst = jnp.arange(PERM_BLOCK)[None, :]
    return (src == jnp.where(dst < half, 2 * dst, 2 * (dst - half) + 1)).astype(BF16)


def _experts(tile_expert, n_used, tile_valid, row_tok, row_slot, h2, wgu, wd, bg, bu, bd, bm):
    n_rows = row_tok.shape[0]
    n_tiles = n_rows // bm
    n_slots = h2.shape[0] // N_SEG * TOP_K
    cur = lambda i, te, nu, nv: (jnp.minimum(i, nu[0] - 1),)
    nxt = lambda i, te, nu, nv: (jnp.minimum(i + 1, nu[0] - 1),)
    prv = lambda i, te, nu, nv: (jnp.clip(i - 1, 0, nu[0] - 1),)
    w_map = lambda i, te, nu, nv: (te[i], 0, 0)
    grid_spec = pltpu.PrefetchScalarGridSpec(
        num_scalar_prefetch=3,
        grid=(n_tiles,),
        in_specs=[pl.BlockSpec((bm,), cur, memory_space=pltpu.SMEM),
                  pl.BlockSpec((bm,), nxt, memory_space=pltpu.SMEM),
                  pl.BlockSpec((bm,), cur, memory_space=pltpu.SMEM),
                  pl.BlockSpec((bm,), prv, memory_space=pltpu.SMEM),
                  pl.BlockSpec(memory_space=pl.ANY),
                  pl.BlockSpec((1, D_MODEL, 2 * D_EXPERT), w_map),
                  pl.BlockSpec((1, D_EXPERT, D_MODEL), w_map),
                  pl.BlockSpec((PERM_BLOCK, PERM_BLOCK), lambda i, te, nu, nv: (0, 0)),
                  pl.BlockSpec((1, 1, D_EXPERT), w_map),
                  pl.BlockSpec((1, 1, D_EXPERT), w_map),
                  pl.BlockSpec((1, 1, D_MODEL), w_map)],
        out_specs=pl.BlockSpec(memory_space=pl.ANY),
        scratch_shapes=[pltpu.VMEM((bm * N_SEG, LANES), F32), pltpu.VMEM((bm * N_SEG, LANES), F32),
                        pltpu.VMEM((bm * N_SEG, LANES), F32), pltpu.VMEM((bm * N_SEG, LANES), F32),
                        pltpu.VMEM((D_MODEL, D_EXPERT), BF16), pltpu.VMEM((D_MODEL, D_EXPERT), BF16),
                        pltpu.VMEM((D_EXPERT, D_MODEL), BF16),
                        pltpu.SemaphoreType.DMA((2,)), pltpu.SemaphoreType.DMA((2,))],
    )
    return pl.pallas_call(
        functools.partial(_expert_kernel, bm=bm),
        grid_spec=grid_spec,
        out_shape=jax.ShapeDtypeStruct((n_slots * N_SEG, LANES), F32),
        compiler_params=pltpu.CompilerParams(dimension_semantics=("arbitrary",),
                                             vmem_limit_bytes=VMEM_LIMIT,
                                             has_side_effects=True),
        name="experts",
    )(tile_expert, n_used, tile_valid, row_tok, row_tok, row_slot, row_slot, h2, wgu, wd, _split_permutation(),
      bg, bu, bd)


def _combine_kernel(y0_ref, y1_ref, y2_ref, y3_ref, gate_ref, x1_ref, g_ref, o_ref, *, tm):
    acc = x1_ref[...]
    for k, y_ref in enumerate((y0_ref, y1_ref, y2_ref, y3_ref)):
        acc = acc + gate_ref[:, k:k + 1] * _load_token_tiles(y_ref, tm)
    o_ref[...] = _rms(acc, g_ref[...])


def _combine(y_slots, gates, x1, g, tm):
    t = x1.shape[0]
    nb = t // tm

    def slot_spec(k):
        return pl.BlockSpec((tm * N_SEG, LANES), lambda i: (k * nb + i, 0))

    return pl.pallas_call(
        functools.partial(_combine_kernel, tm=tm),
        grid=(nb,),
        in_specs=[slot_spec(0), slot_spec(1), slot_spec(2), slot_spec(3),
                  pl.BlockSpec((tm, TOP_K), lambda i: (i, 0)),
                  pl.BlockSpec((tm, D_MODEL), lambda i: (i, 0)),
                  pl.BlockSpec((1, D_MODEL), lambda i: (0, 0))],
        out_specs=pl.BlockSpec((tm, D_MODEL), lambda i: (i, 0)),
        out_shape=jax.ShapeDtypeStruct((t, D_MODEL), F32),
        compiler_params=pltpu.CompilerParams(dimension_semantics=("arbitrary",),
                                             vmem_limit_bytes=VMEM_LIMIT),
        name="combine",
    )(y_slots, y_slots, y_slots, y_slots, gates, x1, g)


def _lane_row(vec, lane0):
    n = vec.shape[0]
    return jnp.zeros((8, LANES), F32).at[0, lane0:lane0 + n].set(vec.astype(F32))


def _tile_size(n, pref):
    return pref if n % pref == 0 else n


def kernel(x, norm_mix, w_in, gdn_conv, gdn_a_log, gdn_dt_bias, gdn_norm, ml_conv, ml_b_i, ml_b_f,
           ml_norm, w_up_gdn, w_up_ml, w_out, norm_ffn, w_router, b_router, w_gate_up, b_gate_up,
           w_down, b_down, norm_final):
    assert norm_mix.shape[0] == 1, "single-layer stack"
    b, s, d = x.shape
    assert d == D_MODEL and s % CHUNK == 0
    t = b * s
    x2 = x.reshape(t, d)

    w = w_in[0]
    w_big = jnp.concatenate([_cols(w, n) for n in ("g_q", "g_k", "g_v", "g_z", "m_q", "m_k", "m_v",
                                                   "m_o", "gate_gdn", "gate_ml")], axis=1).astype(BF16)
    w_small = jnp.concatenate([_cols(w, n) for n in ("g_a", "g_b", "m_i", "m_f")], axis=1)
    w_small = jnp.pad(w_small, ((0, 0), (0, LANES - N_SMALL))).astype(BF16)

    tm = _tile_size(t, 512)
    big, small = _in_proj(x2, norm_mix[0][None, :], w_big, w_small, tm)
    big3 = big.reshape(b, s, N_BIG)
    small3 = small.reshape(b, s, LANES)

    cs = _tile_size(s, 512)
    gdn_prm = _lane_row(gdn_a_log[0], 0).at[1, 0:N_HEADS].set(gdn_dt_bias[0].astype(F32))
    oa = _gdn(big3, small3, gdn_conv[0].astype(F32), gdn_prm, gdn_norm[0][None, :].astype(F32), cs)
    ml_prm = _lane_row(ml_b_i[0], 2 * N_HEADS).at[0, 3 * N_HEADS:4 * N_HEADS].set(ml_b_f[0].astype(F32))
    ob = _mlstm(big3, small3, ml_conv[0].astype(F32), ml_prm, ml_norm[0][None, :].astype(F32), cs)

    w_r = jnp.pad(w_router[0], ((0, 0), (0, LANES - N_EXPERTS))).astype(BF16)
    b_r = jnp.full((1, LANES), -1e30, F32).at[0, :N_EXPERTS].set(b_router[0].astype(F32))
    x1, h2, idx, gates, rank, counts = _merge(
        oa.reshape(t, D_QK), ob.reshape(t, D_QK), big, x2, w_up_gdn[0].astype(BF16),
        w_up_ml[0].astype(BF16), w_out[0].astype(BF16), norm_ffn[0][None, :], w_r, b_r, tm)

    bm = EXPERT_TILE
    n_assign = t * TOP_K
    n_tiles = -(-n_assign // bm) + N_EXPERTS
    counts = counts[0, :N_EXPERTS]
    padded = (counts + bm - 1) // bm * bm
    pend = jnp.cumsum(padded)
    pstart = pend - padded
    dest = (pstart[idx] + rank).reshape(-1)
    n_used = (pend[-1] // bm).astype(I32)
    tile_ids = jnp.minimum(jnp.arange(n_tiles, dtype=I32), n_used - 1)
    tile_expert = jnp.minimum(jnp.sum((pend[None, :] <= (tile_ids * bm)[:, None]).astype(I32), axis=1),
                              N_EXPERTS - 1)
    tile_valid = jnp.clip(counts[tile_expert] - (tile_ids * bm - pstart[tile_expert]), 0, bm).astype(I32)
    assign = jnp.arange(n_assign, dtype=I32)
    pads = padded - counts
    cpad = jnp.cumsum(pads)
    j = jnp.arange(n_tiles * bm - n_assign, dtype=I32)
    pe = jnp.minimum(jnp.sum((cpad[None, :] <= j[:, None]).astype(I32), axis=1), N_EXPERTS - 1)
    pad_row = jnp.where(j < cpad[-1], pstart[pe] + counts[pe] + j - (cpad[pe] - pads[pe]),
                        pend[-1] + j - cpad[-1])
    _, row_slot = lax.sort((jnp.concatenate([dest, pad_row]).astype(I32),
                            jnp.concatenate([(assign % TOP_K) * t + assign // TOP_K, jnp.full_like(j, -1)])),
                           num_keys=1)
    row_tok = jnp.where(row_slot < 0, jnp.arange(n_tiles * bm, dtype=I32), row_slot) % t

    bgu = b_gate_up[0]
    y_slots = _experts(tile_expert.astype(I32), n_used.reshape(1), tile_valid, row_tok, row_slot, h2,
                       w_gate_up[0], w_down[0],
                       bgu[:, None, 0::2].astype(F32), bgu[:, None, 1::2].astype(F32),
                       b_down[0][:, None, :].astype(F32), bm)
    out = _combine(y_slots, gates, x1, norm_final[None, :], _tile_size(t, 256))
    return out.reshape(b, s, d)
```

```python
import functools

import jax
import jax.numpy as jnp
from jax import lax
from jax.experimental import pallas as pl
from jax.experimental.pallas import tpu as pltpu

F32 = jnp.float32
BF16 = jnp.bfloat16
I32 = jnp.int32

D_MODEL = 1024
N_HEADS = 4
D_HEAD = 128
D_QK = N_HEADS * D_HEAD
CONV_WIDTH = 4
CHUNK = 64
GATE_SOFTCAP = 15.0
N_EXPERTS = 32
TOP_K = 4
D_EXPERT = 1024
SWIGLU_LIMIT = 7.0
SWIGLU_ALPHA = 1.702
NORM_EPS = 1e-6

LANES = 128
SUBLANES = 8
N_SMALL = 16
N_BIG = 6 * D_MODEL
VMEM_LIMIT = 56 * 1024 * 1024
EXPERT_TILE = 512
DMA_UNROLL = 8
DMA_GROUPS = 4
RANK_SPAN = 1 << 18
PERM_BLOCK = 256
LOCAL_CHUNKS = 4

_OFF = {}
_o = 0
for _name, _w in (("g_q", D_QK), ("g_k", D_QK), ("g_v", D_QK), ("g_z", D_QK), ("g_a", N_HEADS),
                  ("g_b", N_HEADS), ("m_q", D_QK), ("m_k", D_QK), ("m_v", D_QK), ("m_o", D_QK),
                  ("m_i", N_HEADS), ("m_f", N_HEADS), ("gate_gdn", D_MODEL), ("gate_ml", D_MODEL)):
    _OFF[_name] = (_o, _w)
    _o += _w


def _cols(w, name):
    o, n = _OFF[name]
    return w[:, o:o + n]


_NN = (((1,), (0,)), ((), ()))
_NT = (((1,), (1,)), ((), ()))
_TN = (((0,), (0,)), ((), ()))


def _bdot(a, b, dims=_NN):
    return lax.dot_general(a.astype(BF16), b.astype(BF16), dims, preferred_element_type=F32)


def _split3(a):
    hi = a.astype(BF16)
    r = a - hi.astype(F32)
    mid = r.astype(BF16)
    return hi, mid, (r - mid.astype(F32)).astype(BF16)


def _select_dot(sel01, x, dims=_NN):
    s = sel01.astype(BF16)
    h, m, l = _split3(x)
    d = functools.partial(lax.dot_general, dimension_numbers=dims, preferred_element_type=F32)
    return d(s, h) + (d(s, m) + d(s, l))


def _sigmoid(x):
    return 1.0 / (1.0 + jnp.exp(-x))


def _silu(x):
    return x * _sigmoid(x)


def _softplus(x):
    return jnp.maximum(x, 0.0) + jnp.log(1.0 + jnp.exp(-jnp.abs(x)))


def _rms(x, g):
    return x * lax.rsqrt(jnp.mean(x * x, axis=-1, keepdims=True) + NORM_EPS) * g


def _in_proj_kernel(x_ref, g_ref, wbig_ref, wsmall_ref, big_ref, small_ref, *, n_chunk):
    h = _rms(x_ref[...], g_ref[...]).astype(BF16)
    for c in range(N_BIG // n_chunk):
        sl = slice(c * n_chunk, (c + 1) * n_chunk)
        big_ref[:, sl] = jnp.dot(h, wbig_ref[:, sl], preferred_element_type=F32)
    small_ref[...] = jnp.dot(h, wsmall_ref[...], preferred_element_type=F32)


def _in_proj(x2, g, w_big, w_small, tm):
    t = x2.shape[0]
    return pl.pallas_call(
        functools.partial(_in_proj_kernel, n_chunk=1024),
        grid=(t // tm,),
        in_specs=[pl.BlockSpec((tm, D_MODEL), lambda i: (i, 0)),
                  pl.BlockSpec((1, D_MODEL), lambda i: (0, 0)),
                  pl.BlockSpec((D_MODEL, N_BIG), lambda i: (0, 0)),
                  pl.BlockSpec((D_MODEL, LANES), lambda i: (0, 0))],
        out_specs=[pl.BlockSpec((tm, N_BIG), lambda i: (i, 0)),
                   pl.BlockSpec((tm, LANES), lambda i: (i, 0))],
        out_shape=[jax.ShapeDtypeStruct((t, N_BIG), F32),
                   jax.ShapeDtypeStruct((t, LANES), F32)],
        compiler_params=pltpu.CompilerParams(dimension_semantics=("arbitrary",),
                                             vmem_limit_bytes=VMEM_LIMIT),
        name="in_proj",
    )(x2, g, w_big, w_small)


def _chunk_masks():
    r = lax.broadcasted_iota(I32, (CHUNK, CHUNK), 0)
    c = lax.broadcasted_iota(I32, (CHUNK, CHUNK), 1)
    return r > c, r >= c


def _block_cumsum_matrix(cs):
    r = lax.broadcasted_iota(I32, (cs, cs), 0)
    c = lax.broadcasted_iota(I32, (cs, cs), 1)
    same = (r // CHUNK) == (c // CHUNK)
    return jnp.where(same & (r >= c), 1.0, 0.0).astype(F32)


def _lane_onehot(lane):
    return jnp.where(lax.broadcasted_iota(I32, (CHUNK, LANES), 1) == lane, 1.0, 0.0).astype(F32)


def _conv_silu_pieces(x_ref, w_ref, buf_ref, cs, width):
    @pl.when(pl.program_id(1) == 0)
    def _():
        buf_ref[0:8, :] = jnp.zeros((8, buf_ref.shape[1]), F32)

    buf_ref[8:8 + cs, :] = x_ref[0]
    for r0 in range(0, cs, CHUNK):
        for c0 in range(0, buf_ref.shape[1], width):
            cols = slice(c0, c0 + width)
            acc = w_ref[CONV_WIDTH - 1:CONV_WIDTH, cols] * buf_ref[8 + r0:8 + r0 + CHUNK, cols]
            for j in range(CONV_WIDTH - 1):
                s = 8 - (CONV_WIDTH - 1) + j + r0
                acc = acc + w_ref[j:j + 1, cols] * buf_ref[s:s + CHUNK, cols]
            yield r0, c0, _silu(acc)
    buf_ref[0:8, :] = buf_ref[cs:cs + 8, :]


def _gdn_kernel(qkv_ref, z_ref, sm_ref, conv_ref, prm_ref, gn_ref, o_ref,
                buf_ref, q_s, k_s, v_s, w_s, attn_s, gam_s, beta_s, state_s, *, cs):
    n_chunks = cs // CHUNK

    @pl.when(pl.program_id(1) == 0)
    def _():
        state_s[...] = jnp.zeros(state_s.shape, F32)

    for r0, c0, piece in _conv_silu_pieces(qkv_ref, conv_ref, buf_ref, cs, D_QK):
        rows = slice(r0, r0 + CHUNK)
        if c0 == 2 * D_QK:
            v_s[rows, :] = piece
            continue
        dst, scale = (q_s, D_HEAD ** -0.5) if c0 == 0 else (k_s, 1.0)
        for h in range(N_HEADS):
            sl = slice(h * D_HEAD, (h + 1) * D_HEAD)
            a = piece[:, sl]
            dst[rows, sl] = a * (lax.rsqrt(jnp.sum(a * a, axis=-1, keepdims=True) + NORM_EPS) * scale)

    sm = sm_ref[0]
    logdec = -jnp.exp(prm_ref[0:1, :]) * _softplus(sm + prm_ref[1:2, :])
    gam_s[...] = _select_dot(_block_cumsum_matrix(cs), logdec)
    beta_s[...] = _sigmoid(sm)

    strict, incl = _chunk_masks()
    eye = jnp.where(lax.broadcasted_iota(I32, (CHUNK, CHUNK), 0)
                    == lax.broadcasted_iota(I32, (CHUNK, CHUNK), 1), 1.0, 0.0).astype(F32)
    gnorm = gn_ref[...]

    heads = range(N_HEADS)
    hsl = [slice(h * D_HEAD, (h + 1) * D_HEAD) for h in heads]
    asl = [slice(h * CHUNK, (h + 1) * CHUNK) for h in heads]
    onehots = [_lane_onehot(h) for h in heads]

    def local_body(it, carry):
        rows = [pl.ds(pl.multiple_of((it * LOCAL_CHUNKS + ci) * CHUNK, CHUNK), CHUNK)
                for ci in range(LOCAL_CHUNKS)]
        probs = [(ci, h) for ci in range(LOCAL_CHUNKS) for h in heads]
        gam_all = [gam_s[r, :] for r in rows]
        beta_all = [beta_s[r, :] for r in rows]
        q = [q_s[rows[ci], hsl[h]] for ci, h in probs]
        k = [k_s[rows[ci], hsl[h]] for ci, h in probs]
        v = [v_s[rows[ci], hsl[h]] for ci, h in probs]
        gc = [gam_all[ci][:, h:h + 1] for ci, h in probs]
        beta = [beta_all[ci][:, N_HEADS + h:N_HEADS + h + 1] for ci, h in probs]
        g_row = [_select_dot(onehots[h], gam_all[ci], _NT) for ci, h in probs]
        kb = [a * b for a, b in zip(k, beta)]
        kk = [_bdot(a, b, _NT) for a, b in zip(kb, k)]
        qk = [_bdot(a, b, _NT) for a, b in zip(q, k)]
        decay = [jnp.where(incl, jnp.exp(jnp.minimum(a - b, 0.0)), 0.0) for a, b in zip(gc, g_row)]
        lower = [jnp.where(strict, a * b, 0.0) for a, b in zip(kk, decay)]
        inv = [eye - a for a in lower]
        pw = [_bdot(a, a) for a in lower]
        for lvl in range(5):
            upd = [_bdot(a, b) for a, b in zip(inv, pw)]
            if lvl < 4:
                pw = [_bdot(a, a) for a in pw]
            inv = [a + b for a, b in zip(inv, upd)]
        e_gc = [jnp.exp(a) for a in gc]
        rhs = [jnp.concatenate([a * b, c * d], axis=1) for a, b, c, d in zip(v, beta, kb, e_gc)]
        sol = [_bdot(a, b) for a, b in zip(inv, rhs)]
        for p, (ci, h) in enumerate(probs):
            v_s[rows[ci], hsl[h]] = sol[p][:, :D_HEAD]
            w_s[rows[ci], hsl[h]] = sol[p][:, D_HEAD:]
            attn_s[rows[ci], asl[h]] = qk[p] * decay[p]
            q_s[rows[ci], hsl[h]] = q[p] * e_gc[p]
            k_s[rows[ci], hsl[h]] = k[p] * jnp.exp(gc[p][CHUNK - 1:CHUNK, :] - gc[p])
        return carry

    lax.fori_loop(0, n_chunks // LOCAL_CHUNKS, local_body, 0)

    def state_body(c, carry):
        rows = pl.ds(pl.multiple_of(c * CHUNK, CHUNK), CHUNK)
        g_last = gam_s[pl.ds(c * CHUNK + CHUNK - 1, 1), :]
        st = [state_s[h] for h in heads]
        ws = [_bdot(w_s[rows, hsl[h]], st[h]) for h in heads]
        qs = [_bdot(q_s[rows, hsl[h]], st[h]) for h in heads]
        u = [v_s[rows, hsl[h]] - ws[h] for h in heads]
        au = [_bdot(attn_s[rows, asl[h]], u[h]) for h in heads]
        ku = [_bdot(k_s[rows, hsl[h]], u[h], _TN) for h in heads]
        for h in heads:
            state_s[h] = st[h] * jnp.exp(g_last[:, h:h + 1]) + ku[h]
            o = qs[h] + au[h]
            o = o * lax.rsqrt(jnp.mean(o * o, axis=-1, keepdims=True) + NORM_EPS) * gnorm
            o_ref[0, rows, hsl[h]] = (o * _silu(z_ref[0, rows, hsl[h]])).astype(o_ref.dtype)
        return carry

    lax.fori_loop(0, n_chunks, state_body, 0)


def _gdn(big3, small3, conv_w, prm, gnorm, cs):
    b, s, _ = big3.shape
    return pl.pallas_call(
        functools.partial(_gdn_kernel, cs=cs),
        grid=(b, s // cs),
        in_specs=[pl.BlockSpec((1, cs, 3 * D_QK), lambda i, j: (i, j, 0)),
                  pl.BlockSpec((1, cs, D_QK), lambda i, j: (i, j, 3)),
                  pl.BlockSpec((1, cs, LANES), lambda i, j: (i, j, 0)),
                  pl.BlockSpec((CONV_WIDTH, 3 * D_QK), lambda i, j: (0, 0)),
                  pl.BlockSpec((8, LANES), lambda i, j: (0, 0)),
                  pl.BlockSpec((1, D_HEAD), lambda i, j: (0, 0))],
        out_specs=pl.BlockSpec((1, cs, D_QK), lambda i, j: (i, j, 0)),
        out_shape=jax.ShapeDtypeStruct((b, s, D_QK), BF16),
        scratch_shapes=[pltpu.VMEM((cs + 8, 3 * D_QK), F32),
                        pltpu.VMEM((cs, D_QK), F32), pltpu.VMEM((cs, D_QK), F32),
                        pltpu.VMEM((cs, D_QK), F32), pltpu.VMEM((cs, D_QK), F32),
                        pltpu.VMEM((cs, N_HEADS * CHUNK), F32),
                        pltpu.VMEM((cs, LANES), F32), pltpu.VMEM((cs, LANES), F32),
                        pltpu.VMEM((N_HEADS, D_HEAD, D_HEAD), F32)],
        compiler_params=pltpu.CompilerParams(dimension_semantics=("arbitrary", "arbitrary"),
                                             vmem_limit_bytes=VMEM_LIMIT),
        name="gdn",
    )(big3, big3, small3, conv_w, prm, gnorm)


def _mlstm_kernel(qk_ref, v_ref, og_ref, sm_ref, conv_ref, prm_ref, gn_ref, o_ref,
                  buf_ref, q_s, k_s, bcum_s, ipre_s, state_s, m_s, *, cs):
    n_chunks = cs // CHUNK

    @pl.when(pl.program_id(1) == 0)
    def _():
        state_s[...] = jnp.zeros(state_s.shape, F32)
        m_s[...] = jnp.zeros(m_s.shape, F32)

    for r0, c0, piece in _conv_silu_pieces(qk_ref, conv_ref, buf_ref, cs, D_QK):
        if c0 == 0:
            q_s[r0:r0 + CHUNK, :] = piece
        else:
            k_s[r0:r0 + CHUNK, :] = piece * (D_HEAD ** -0.5)

    pre = sm_ref[0] + prm_ref[0:1, :]
    capped = GATE_SOFTCAP * jnp.tanh(pre / GATE_SOFTCAP)
    logf = -_softplus(-capped)
    bcum_s[...] = _select_dot(_block_cumsum_matrix(cs), logf)
    ipre_s[...] = pltpu.roll(capped, N_HEADS, axis=1)

    _, incl = _chunk_masks()
    ones_aug = jnp.ones((CHUNK, D_HEAD), F32)

    heads = range(N_HEADS)
    hsl = [slice(h * D_HEAD, (h + 1) * D_HEAD) for h in heads]
    lanes = [3 * N_HEADS + h for h in heads]
    onehots = [_lane_onehot(lane) for lane in lanes]

    def chunk_body(c, carry):
        rows = pl.ds(pl.multiple_of(c * CHUNK, CHUNK), CHUNK)
        b_all = bcum_s[rows, :]
        comb_all = ipre_s[rows, :] - b_all
        q = [q_s[rows, hsl[h]] for h in heads]
        k = [k_s[rows, hsl[h]] for h in heads]
        v_aug = [jnp.concatenate([v_ref[0, rows, hsl[h]], ones_aug], axis=1) for h in heads]
        st = [state_s[h] for h in heads]
        m_st = [m_s[h] for h in heads]
        bc = [b_all[:, lane:lane + 1] for lane in lanes]
        comb_row = [_select_dot(onehots[h], comb_all, _NT) for h in heads]
        qk = [_bdot(q[h], k[h], _NT) for h in heads]
        inter = [_bdot(q[h], st[h]) for h in heads]
        b_last = [bc[h][CHUNK - 1:CHUNK, :] for h in heads]
        a_log = [b_last[h] + comb_all[:, lanes[h]:lanes[h] + 1] for h in heads]
        m_chunk = [jnp.max(a_log[h], axis=0, keepdims=True) for h in heads]
        wk = [k[h] * jnp.exp(a_log[h] - m_chunk[h]) for h in heads]
        d_state = [_bdot(wk[h], v_aug[h], _TN) for h in heads]
        d_log = [jnp.where(incl, bc[h] + comb_row[h], -jnp.inf) for h in heads]
        m_loc = [jnp.max(d_log[h], axis=-1, keepdims=True) for h in heads]
        p = [jnp.exp(d_log[h] - m_loc[h]) * qk[h] for h in heads]
        loc = [_bdot(p[h], v_aug[h]) for h in heads]
        for h in heads:
            m_inter = bc[h] + m_st[h]
            m_t = jnp.maximum(m_loc[h], m_inter)
            tot = jnp.exp(m_inter - m_t) * inter[h] + jnp.exp(m_loc[h] - m_t) * loc[h]
            hh = tot[:, :D_HEAD] / jnp.maximum(jnp.abs(tot[:, D_HEAD:]), jnp.exp(-m_t))
            m_new = jnp.maximum(b_last[h] + m_st[h], m_chunk[h])
            state_s[h] = (jnp.exp(b_last[h] + m_st[h] - m_new) * st[h]
                          + jnp.exp(m_chunk[h] - m_new) * d_state[h])
            m_s[h] = m_new
            hh = hh * lax.rsqrt(jnp.mean(hh * hh, axis=-1, keepdims=True) + NORM_EPS) * gn_ref[:, hsl[h]]
            o_ref[0, rows, hsl[h]] = (hh * _sigmoid(og_ref[0, rows, hsl[h]])).astype(o_ref.dtype)
        return carry

    lax.fori_loop(0, n_chunks, chunk_body, 0)


def _mlstm(big3, small3, conv_w, prm, gnorm, cs):
    b, s, _ = big3.shape
    return pl.pallas_call(
        functools.partial(_mlstm_kernel, cs=cs),
        grid=(b, s // cs),
        in_specs=[pl.BlockSpec((1, cs, 2 * D_QK), lambda i, j: (i, j, 2)),
                  pl.BlockSpec((1, cs, D_QK), lambda i, j: (i, j, 6)),
                  pl.BlockSpec((1, cs, D_QK), lambda i, j: (i, j, 7)),
                  pl.BlockSpec((1, cs, LANES), lambda i, j: (i, j, 0)),
                  pl.BlockSpec((CONV_WIDTH, 2 * D_QK), lambda i, j: (0, 0)),
                  pl.BlockSpec((8, LANES), lambda i, j: (0, 0)),
                  pl.BlockSpec((1, D_QK), lambda i, j: (0, 0))],
        out_specs=pl.BlockSpec((1, cs, D_QK), lambda i, j: (i, j, 0)),
        out_shape=jax.ShapeDtypeStruct((b, s, D_QK), BF16),
        scratch_shapes=[pltpu.VMEM((cs + 8, 2 * D_QK), F32),
                        pltpu.VMEM((cs, D_QK), F32), pltpu.VMEM((cs, D_QK), F32),
                        pltpu.VMEM((cs, LANES), F32), pltpu.VMEM((cs, LANES), F32),
                        pltpu.VMEM((N_HEADS, D_HEAD, 2 * D_HEAD), F32),
                        pltpu.VMEM((N_HEADS, 1, 1), F32)],
        compiler_params=pltpu.CompilerParams(dimension_semantics=("arbitrary", "arbitrary"),
                                             vmem_limit_bytes=VMEM_LIMIT),
        name="mlstm",
    )(big3, big3, big3, small3, conv_w, prm, gnorm)


N_SEG = D_MODEL // LANES
assert N_SEG == SUBLANES


def _store_token_tiles(ref, val, n):
    for s in range(N_SEG):
        ref[pl.ds(s, n, stride=N_SEG), :] = val[:, s * LANES:(s + 1) * LANES]


def _load_token_tiles(ref, n):
    return jnp.concatenate([ref[pl.ds(s, n, stride=N_SEG), :] for s in range(N_SEG)], axis=1)


def _merge_kernel(oa_ref, ob_ref, gates_ref, x_ref, wa_ref, wb_ref, wo_ref, g_ref, wr_ref, br_ref,
                  x1_ref, h2_ref, key_ref, gate_ref, cnt_ref, carry_s, *, tm):
    @pl.when(pl.program_id(0) == 0)
    def _():
        carry_s[...] = jnp.zeros(carry_s.shape, F32)

    y_a = jnp.dot(oa_ref[...], wa_ref[...], preferred_element_type=F32)
    y_b = jnp.dot(ob_ref[...], wb_ref[...], preferred_element_type=F32)
    y = _sigmoid(gates_ref[:, :D_MODEL]) * y_a + _sigmoid(gates_ref[:, D_MODEL:]) * y_b
    x1 = x_ref[...] + jnp.dot(y.astype(BF16), wo_ref[...], preferred_element_type=F32)
    x1_ref[...] = x1
    h2 = _rms(x1, g_ref[...])
    _store_token_tiles(h2_ref, h2, tm)
    logits = jnp.dot(h2.astype(BF16), wr_ref[...], preferred_element_type=F32) + br_ref[...]

    lane = lax.broadcasted_iota(I32, (tm, LANES), 1).astype(F32)
    vals, sels = [], []
    idx_t = jnp.zeros((tm, LANES), F32)
    work = logits
    for k in range(TOP_K):
        m = jnp.max(work, axis=-1, keepdims=True)
        i = jnp.min(jnp.where(work == m, lane, float(LANES)), axis=-1, keepdims=True)
        sel = lane == i
        work = jnp.where(sel, -jnp.inf, work)
        idx_t = jnp.where(lane == float(k), i, idx_t)
        vals.append(m)
        sels.append(sel)
    es = [jnp.exp(v - vals[0]) for v in vals]
    denom = es[0] + es[1] + es[2] + es[3]
    gate_t = jnp.zeros((tm, LANES), F32)
    for k in range(TOP_K):
        gate_t = jnp.where(lane == float(k), es[k] / denom, gate_t)
    gate_ref[...] = gate_t[:, :TOP_K]

    onehot = jnp.zeros((tm, LANES), F32)
    for sel in sels:
        onehot = onehot + jnp.where(sel, 1.0, 0.0)
    r = lax.broadcasted_iota(I32, (tm, tm), 0)
    c = lax.broadcasted_iota(I32, (tm, tm), 1)
    before = jnp.where(r > c, 1.0, 0.0).astype(BF16)
    cum = jnp.dot(before, onehot.astype(BF16), preferred_element_type=F32) + carry_s[...]
    rank_t = jnp.zeros((tm, LANES), F32)
    for k, sel in enumerate(sels):
        rk = jnp.sum(jnp.where(sel, cum, 0.0), axis=-1, keepdims=True)
        rank_t = jnp.where(lane == float(k), rk, rank_t)
    key_ref[...] = (idx_t * float(RANK_SPAN) + rank_t)[:, :TOP_K].astype(I32)
    total = carry_s[...] + jnp.sum(onehot, axis=0, keepdims=True)
    carry_s[...] = total
    cnt_ref[...] = total.astype(I32)


def _merge(oa, ob, big, x2, wa, wb, wo, g, wr, br, tm):
    t = x2.shape[0]
    const = lambda i: (0, 0)
    return pl.pallas_call(
        functools.partial(_merge_kernel, tm=tm),
        grid=(t // tm,),
        in_specs=[pl.BlockSpec((tm, D_QK), lambda i: (i, 0)),
                  pl.BlockSpec((tm, D_QK), lambda i: (i, 0)),
                  pl.BlockSpec((tm, 2 * D_MODEL), lambda i: (i, 2)),
                  pl.BlockSpec((tm, D_MODEL), lambda i: (i, 0)),
                  pl.BlockSpec((D_QK, D_MODEL), const),
                  pl.BlockSpec((D_QK, D_MODEL), const),
                  pl.BlockSpec((D_MODEL, D_MODEL), const),
                  pl.BlockSpec((1, D_MODEL), const),
                  pl.BlockSpec((D_MODEL, LANES), const),
                  pl.BlockSpec((1, LANES), const)],
        out_specs=[pl.BlockSpec((tm, D_MODEL), lambda i: (i, 0)),
                   pl.BlockSpec((tm * N_SEG, LANES), lambda i: (i, 0)),
                   pl.BlockSpec((tm, TOP_K), lambda i: (i, 0)),
                   pl.BlockSpec((tm, TOP_K), lambda i: (i, 0)),
                   pl.BlockSpec((1, LANES), const)],
        out_shape=[jax.ShapeDtypeStruct((t, D_MODEL), F32),
                   jax.ShapeDtypeStruct((t * N_SEG, LANES), F32),
                   jax.ShapeDtypeStruct((t, TOP_K), I32),
                   jax.ShapeDtypeStruct((t, TOP_K), F32),
                   jax.ShapeDtypeStruct((1, LANES), I32)],
        scratch_shapes=[pltpu.VMEM((1, LANES), F32)],
        compiler_params=pltpu.CompilerParams(dimension_semantics=("arbitrary",),
                                             vmem_limit_bytes=VMEM_LIMIT),
        name="merge_router",
    )(oa, ob, big, x2, wa, wb, wo, g, wr, br)


def _expert_kernel(te_ref, nu_ref, nv_ref, tok_ref, tok_next_ref, slot_ref, slot_prev_ref, h2_hbm,
                   wgu_ref, wd_ref, perm_ref, bg_ref, bu_ref, bd_ref, y_hbm,
                   xbuf0, xbuf1, ybuf0, ybuf1, wg_s, wu_s, wd_s, in_sem, out_sem, *, bm):
    i = pl.program_id(0)
    n_used = nu_ref[0]
    new_expert = (i == 0) | (te_ref[i] != te_ref[jnp.maximum(i - 1, 0)])
    xbufs = (xbuf0, xbuf1)
    ybufs = (ybuf0, ybuf1)

    def token_tile(ref, r):
        return ref.at[pl.ds(pl.multiple_of(r * N_SEG, N_SEG), N_SEG)]

    def gather_row(ids_ref, s, r):
        pltpu.make_async_copy(token_tile(h2_hbm, ids_ref[r]), token_tile(xbufs[s], r),
                              in_sem.at[s]).start()

    def wait_gather(s):
        pltpu.make_async_copy(h2_hbm.at[pl.ds(0, bm * N_SEG)], xbufs[s], in_sem.at[s]).wait()

    def wait_scatter(s, n_rows):
        n = pl.multiple_of(n_rows * N_SEG, N_SEG)
        pltpu.make_async_copy(ybufs[s].at[pl.ds(0, n)], y_hbm.at[pl.ds(0, n)], out_sem.at[s]).wait()

    @pl.when(i == 0)
    def _():
        def body(r, carry):
            gather_row(tok_ref, 0, r)
            return carry
        lax.fori_loop(0, bm, body, 0, unroll=DMA_UNROLL)

    n_prev = jnp.where(i >= 1, nv_ref[jnp.maximum(i - 1, 0)], 0)

    def scatter_row(ids_ref, s, r):
        pltpu.make_async_copy(token_tile(ybufs[s], r), token_tile(y_hbm, ids_ref[r]),
                              out_sem.at[s]).start()

    def scatter_loop(ids_ref, s, n_rows):
        def one(r, carry):
            scatter_row(ids_ref, s, r)
            return carry

        def group(j, carry):
            for q in range(DMA_UNROLL):
                scatter_row(ids_ref, s, j * DMA_UNROLL + q)
            return carry
        n_groups = n_rows // DMA_UNROLL
        lax.fori_loop(0, n_groups, group, 0)
        lax.fori_loop(n_groups * DMA_UNROLL, n_rows, one, 0)

    def tile_body(cur, prev_full):
        nxt = 1 - cur
        wait_gather(cur)

        @pl.when(i >= 2)
        def _():
            wait_scatter(cur, nv_ref[i - 2])

        if not prev_full:
            scatter_loop(slot_prev_ref, nxt, n_prev)

        @pl.when(new_expert)
        def _():
            half = PERM_BLOCK // 2
            for j in range(2 * D_EXPERT // PERM_BLOCK):
                blk = wgu_ref[0, :, j * PERM_BLOCK:(j + 1) * PERM_BLOCK].astype(BF16)
                split = jnp.dot(blk, perm_ref[...], preferred_element_type=F32)
                wg_s[:, j * half:(j + 1) * half] = split[:, :half].astype(BF16)
                wu_s[:, j * half:(j + 1) * half] = split[:, half:].astype(BF16)
            wd_s[...] = wd_ref[0].astype(BF16)

        n_grp = bm // DMA_GROUPS

        def dma_group(gi):
            for r in range(gi * n_grp, (gi + 1) * n_grp):
                gather_row(tok_next_ref, nxt, r)
                if prev_full:
                    scatter_row(slot_prev_ref, nxt, r)

        dma_group(0)
        x = _load_token_tiles(xbufs[cur], bm).astype(BF16)
        g = jnp.dot(x, wg_s[...], preferred_element_type=F32) + bg_ref[0]
        dma_group(1)
        u = jnp.dot(x, wu_s[...], preferred_element_type=F32) + bu_ref[0]
        dma_group(2)
        gate = jnp.minimum(g, SWIGLU_LIMIT)
        up = jnp.clip(u, -SWIGLU_LIMIT, SWIGLU_LIMIT)
        act = gate * _sigmoid(SWIGLU_ALPHA * gate) * (up + 1.0)
        dma_group(3)
        y = jnp.dot(act.astype(BF16), wd_s[...], preferred_element_type=F32) + bd_ref[0]
        _store_token_tiles(ybufs[cur], y, bm)

        @pl.when(i == n_used - 1)
        def _():
            scatter_loop(slot_ref, cur, nv_ref[i])
            wait_gather(nxt)
            wait_scatter(cur, nv_ref[i])

            @pl.when(i >= 1)
            def _():
                wait_scatter(nxt, nv_ref[i - 1])

    for parity in range(2):
        for prev_full in (False, True):
            full = (n_prev == bm) if prev_full else (n_prev != bm)
            cond = (i < n_used) & (lax.rem(i, 2) == parity) & full
            pl.when(cond)(functools.partial(tile_body, parity, prev_full))


def _split_permutation():
    half = PERM_BLOCK // 2
    src = jnp.arange(PERM_BLOCK)[:, None]
    dst = jnp.arange(PERM_BLOCK)[None, :]
    return (src == jnp.where(dst < half, 2 * dst, 2 * (dst - half) + 1)).astype(BF16)


def _experts(tile_expert, n_used, tile_valid, row_tok, row_slot, h2, wgu, wd, bg, bu, bd, bm):
    n_rows = row_tok.shape[0]
    n_tiles = n_rows // bm
    n_slots = h2.shape[0] // N_SEG * TOP_K
    cur = lambda i, te, nu, nv: (jnp.minimum(i, nu[0] - 1),)
    nxt = lambda i, te, nu, nv: (jnp.minimum(i + 1, nu[0] - 1),)
    prv = lambda i, te, nu, nv: (jnp.clip(i - 1, 0, nu[0] - 1),)
    w_map = lambda i, te, nu, nv: (te[i], 0, 0)
    grid_spec = pltpu.PrefetchScalarGridSpec(
        num_scalar_prefetch=3,
        grid=(n_tiles,),
        in_specs=[pl.BlockSpec((bm,), cur, memory_space=pltpu.SMEM),
                  pl.BlockSpec((bm,), nxt, memory_space=pltpu.SMEM),
                  pl.BlockSpec((bm,), cur, memory_space=pltpu.SMEM),
                  pl.BlockSpec((bm,), prv, memory_space=pltpu.SMEM),
                  pl.BlockSpec(memory_space=pl.ANY),
                  pl.BlockSpec((1, D_MODEL, 2 * D_EXPERT), w_map),
                  pl.BlockSpec((1, D_EXPERT, D_MODEL), w_map),
                  pl.BlockSpec((PERM_BLOCK, PERM_BLOCK), lambda i, te, nu, nv: (0, 0)),
                  pl.BlockSpec((1, 1, D_EXPERT), w_map),
                  pl.BlockSpec((1, 1, D_EXPERT), w_map),
                  pl.BlockSpec((1, 1, D_MODEL), w_map)],
        out_specs=pl.BlockSpec(memory_space=pl.ANY),
        scratch_shapes=[pltpu.VMEM((bm * N_SEG, LANES), F32), pltpu.VMEM((bm * N_SEG, LANES), F32),
                        pltpu.VMEM((bm * N_SEG, LANES), F32), pltpu.VMEM((bm * N_SEG, LANES), F32),
                        pltpu.VMEM((D_MODEL, D_EXPERT), BF16), pltpu.VMEM((D_MODEL, D_EXPERT), BF16),
                        pltpu.VMEM((D_EXPERT, D_MODEL), BF16),
                        pltpu.SemaphoreType.DMA((2,)), pltpu.SemaphoreType.DMA((2,))],
    )
    return pl.pallas_call(
        functools.partial(_expert_kernel, bm=bm),
        grid_spec=grid_spec,
        out_shape=jax.ShapeDtypeStruct((n_slots * N_SEG, LANES), F32),
        compiler_params=pltpu.CompilerParams(dimension_semantics=("arbitrary",),
                                             vmem_limit_bytes=VMEM_LIMIT,
                                             has_side_effects=True),
        name="experts",
    )(tile_expert, n_used, tile_valid, row_tok, row_tok, row_slot, row_slot, h2, wgu, wd, _split_permutation(),
      bg, bu, bd)


def _combine_kernel(y0_ref, y1_ref, y2_ref, y3_ref, gate_ref, x1_ref, g_ref, o_ref, *, tm):
    acc = x1_ref[...]
    for k, y_ref in enumerate((y0_ref, y1_ref, y2_ref, y3_ref)):
        acc = acc + gate_ref[:, k:k + 1] * _load_token_tiles(y_ref, tm)
    o_ref[...] = _rms(acc, g_ref[...])


def _combine(y_slots, gates, x1, g, tm):
    t = x1.shape[0]
    nb = t // tm

    def slot_spec(k):
        return pl.BlockSpec((tm * N_SEG, LANES), lambda i: (k * nb + i, 0))

    return pl.pallas_call(
        functools.partial(_combine_kernel, tm=tm),
        grid=(nb,),
        in_specs=[slot_spec(0), slot_spec(1), slot_spec(2), slot_spec(3),
                  pl.BlockSpec((tm, TOP_K), lambda i: (i, 0)),
                  pl.BlockSpec((tm, D_MODEL), lambda i: (i, 0)),
                  pl.BlockSpec((1, D_MODEL), lambda i: (0, 0))],
        out_specs=pl.BlockSpec((tm, D_MODEL), lambda i: (i, 0)),
        out_shape=jax.ShapeDtypeStruct((t, D_MODEL), F32),
        compiler_params=pltpu.CompilerParams(dimension_semantics=("arbitrary",),
                                             vmem_limit_bytes=VMEM_LIMIT),
        name="combine",
    )(y_slots, y_slots, y_slots, y_slots, gates, x1, g)


def _lane_row(vec, lane0):
    n = vec.shape[0]
    return jnp.zeros((8, LANES), F32).at[0, lane0:lane0 + n].set(vec.astype(F32))


def _tile_size(n, pref):
    return pref if n % pref == 0 else n


def kernel(x, norm_mix, w_in, gdn_conv, gdn_a_log, gdn_dt_bias, gdn_norm, ml_conv, ml_b_i, ml_b_f,
           ml_norm, w_up_gdn, w_up_ml, w_out, norm_ffn, w_router, b_router, w_gate_up, b_gate_up,
           w_down, b_down, norm_final):
    assert norm_mix.shape[0] == 1, "single-layer stack"
    b, s, d = x.shape
    assert d == D_MODEL and s % CHUNK == 0
    t = b * s
    x2 = x.reshape(t, d)

    w = w_in[0]
    w_big = jnp.concatenate([_cols(w, n) for n in ("g_q", "g_k", "g_v", "g_z", "m_q", "m_k", "m_v",
                                                   "m_o", "gate_gdn", "gate_ml")], axis=1).astype(BF16)
    w_small = jnp.concatenate([_cols(w, n) for n in ("g_a", "g_b", "m_i", "m_f")], axis=1)
    w_small = jnp.pad(w_small, ((0, 0), (0, LANES - N_SMALL))).astype(BF16)

    tm = _tile_size(t, 512)
    big, small = _in_proj(x2, norm_mix[0][None, :], w_big, w_small, tm)
    big3 = big.reshape(b, s, N_BIG)
    small3 = small.reshape(b, s, LANES)

    cs = _tile_size(s, 512)
    gdn_prm = _lane_row(gdn_a_log[0], 0).at[1, 0:N_HEADS].set(gdn_dt_bias[0].astype(F32))
    oa = _gdn(big3, small3, gdn_conv[0].astype(F32), gdn_prm, gdn_norm[0][None, :].astype(F32), cs)
    ml_prm = _lane_row(ml_b_i[0], 2 * N_HEADS).at[0, 3 * N_HEADS:4 * N_HEADS].set(ml_b_f[0].astype(F32))
    ob = _mlstm(big3, small3, ml_conv[0].astype(F32), ml_prm, ml_norm[0][None, :].astype(F32), cs)

    w_r = jnp.pad(w_router[0], ((0, 0), (0, LANES - N_EXPERTS))).astype(BF16)
    b_r = jnp.full((1, LANES), -1e30, F32).at[0, :N_EXPERTS].set(b_router[0].astype(F32))
    x1, h2, key, gates, counts = _merge(
        oa.reshape(t, D_QK), ob.reshape(t, D_QK), big, x2, w_up_gdn[0].astype(BF16),
        w_up_ml[0].astype(BF16), w_out[0].astype(BF16), norm_ffn[0][None, :], w_r, b_r, tm)

    bm = EXPERT_TILE
    n_assign = t * TOP_K
    n_tiles = -(-n_assign // bm) + N_EXPERTS
    counts = counts[0, :N_EXPERTS]
    padded = (counts + bm - 1) // bm * bm
    pend = jnp.cumsum(padded)
    pstart = pend - padded
    n_used = (pend[-1] // bm).astype(I32)
    tile_ids = jnp.minimum(jnp.arange(n_tiles, dtype=I32), n_used - 1)
    tile_expert = jnp.minimum(jnp.sum((pend[None, :] <= (tile_ids * bm)[:, None]).astype(I32), axis=1),
                              N_EXPERTS - 1)
    tile_valid = jnp.clip(counts[tile_expert] - (tile_ids * bm - pstart[tile_expert]), 0, bm).astype(I32)
    assert t <= RANK_SPAN and bm <= RANK_SPAN
    assign = jnp.arange(n_assign, dtype=I32)
    pad_e = jnp.arange(N_EXPERTS, dtype=I32)[:, None]
    pad_p = jnp.arange(bm, dtype=I32)[None, :]
    pad_key = jnp.where(pad_p < (padded - counts)[:, None],
                        pad_e * RANK_SPAN + counts[:, None] + pad_p,
                        N_EXPERTS * RANK_SPAN + pad_e * bm + pad_p)
    _, row_slot = lax.sort((jnp.concatenate([key.reshape(-1), pad_key.reshape(-1)]),
                            jnp.concatenate([(assign % TOP_K) * t + assign // TOP_K,
                                             jnp.full((N_EXPERTS * bm,), -1, I32)])),
                           num_keys=1)
    row_tok = jnp.where(row_slot < 0, jnp.arange(n_tiles * bm, dtype=I32), row_slot) % t

    bgu = b_gate_up[0]
    y_slots = _experts(tile_expert.astype(I32), n_used.reshape(1), tile_valid, row_tok, row_slot, h2,
                       w_gate_up[0], w_down[0],
                       bgu[:, None, 0::2].astype(F32), bgu[:, None, 1::2].astype(F32),
                       b_down[0][:, None, :].astype(F32), bm)
    out = _combine(y_slots, gates, x1, norm_final[None, :], _tile_size(t, 256))
    return out.reshape(b, s, d)
```

```python
import functools

import jax
import jax.numpy as jnp
from jax import lax
from jax.experimental import pallas as pl
from jax.experimental.pallas import tpu as pltpu

F32 = jnp.float32
BF16 = jnp.bfloat16
I32 = jnp.int32

D_MODEL = 1024
N_HEADS = 4
D_HEAD = 128
D_QK = N_HEADS * D_HEAD
CONV_WIDTH = 4
CHUNK = 64
GATE_SOFTCAP = 15.0
N_EXPERTS = 32
TOP_K = 4
D_EXPERT = 1024
SWIGLU_LIMIT = 7.0
SWIGLU_ALPHA = 1.702
NORM_EPS = 1e-6

LANES = 128
SUBLANES = 8
CONV_CARRY = SUBLANES
PAD_LOGIT = -1e30
N_SMALL = 16
N_BIG = 6 * D_MODEL
VMEM_LIMIT = 56 * 1024 * 1024
EXPERT_TILE = 512
DMA_UNROLL = 8
DMA_GROUPS = 4
RANK_SPAN = 1 << 18
PERM_BLOCK = 256
LOCAL_CHUNKS = 4

_OFF = {}
_o = 0
for _name, _w in (("g_q", D_QK), ("g_k", D_QK), ("g_v", D_QK), ("g_z", D_QK), ("g_a", N_HEADS),
                  ("g_b", N_HEADS), ("m_q", D_QK), ("m_k", D_QK), ("m_v", D_QK), ("m_o", D_QK),
                  ("m_i", N_HEADS), ("m_f", N_HEADS), ("gate_gdn", D_MODEL), ("gate_ml", D_MODEL)):
    _OFF[_name] = (_o, _w)
    _o += _w


def _cols(w, name):
    o, n = _OFF[name]
    return w[:, o:o + n]


_NN = (((1,), (0,)), ((), ()))
_NT = (((1,), (1,)), ((), ()))
_TN = (((0,), (0,)), ((), ()))


def _bdot(a, b, dims=_NN):
    return lax.dot_general(a.astype(BF16), b.astype(BF16), dims, preferred_element_type=F32)


def _split3(a):
    hi = a.astype(BF16)
    r = a - hi.astype(F32)
    mid = r.astype(BF16)
    return hi, mid, (r - mid.astype(F32)).astype(BF16)


def _select_dot(sel01, x, dims=_NN):
    s = sel01.astype(BF16)
    h, m, l = _split3(x)
    d = functools.partial(lax.dot_general, dimension_numbers=dims, preferred_element_type=F32)
    return d(s, h) + (d(s, m) + d(s, l))


def _sigmoid(x):
    return 1.0 / (1.0 + jnp.exp(-x))


def _silu(x):
    return x * _sigmoid(x)


def _softplus(x):
    return jnp.maximum(x, 0.0) + jnp.log(1.0 + jnp.exp(-jnp.abs(x)))


def _rms(x, g):
    return x * lax.rsqrt(jnp.mean(x * x, axis=-1, keepdims=True) + NORM_EPS) * g


def _in_proj_kernel(x_ref, g_ref, wbig_ref, wsmall_ref, big_ref, small_ref, *, n_chunk):
    h = _rms(x_ref[...], g_ref[...]).astype(BF16)
    for c in range(N_BIG // n_chunk):
        sl = slice(c * n_chunk, (c + 1) * n_chunk)
        big_ref[:, sl] = jnp.dot(h, wbig_ref[:, sl], preferred_element_type=F32)
    small_ref[...] = jnp.dot(h, wsmall_ref[...], preferred_element_type=F32)


def _in_proj(x2, g, w_big, w_small, tm):
    t = x2.shape[0]
    return pl.pallas_call(
        functools.partial(_in_proj_kernel, n_chunk=1024),
        grid=(t // tm,),
        in_specs=[pl.BlockSpec((tm, D_MODEL), lambda i: (i, 0)),
                  pl.BlockSpec((1, D_MODEL), lambda i: (0, 0)),
                  pl.BlockSpec((D_MODEL, N_BIG), lambda i: (0, 0)),
                  pl.BlockSpec((D_MODEL, LANES), lambda i: (0, 0))],
        out_specs=[pl.BlockSpec((tm, N_BIG), lambda i: (i, 0)),
                   pl.BlockSpec((tm, LANES), lambda i: (i, 0))],
        out_shape=[jax.ShapeDtypeStruct((t, N_BIG), F32),
                   jax.ShapeDtypeStruct((t, LANES), F32)],
        compiler_params=pltpu.CompilerParams(dimension_semantics=("arbitrary",),
                                             vmem_limit_bytes=VMEM_LIMIT),
        name="in_proj",
    )(x2, g, w_big, w_small)


def _chunk_masks():
    r = lax.broadcasted_iota(I32, (CHUNK, CHUNK), 0)
    c = lax.broadcasted_iota(I32, (CHUNK, CHUNK), 1)
    return r > c, r >= c


def _block_cumsum_matrix(cs):
    r = lax.broadcasted_iota(I32, (cs, cs), 0)
    c = lax.broadcasted_iota(I32, (cs, cs), 1)
    same = (r // CHUNK) == (c // CHUNK)
    return jnp.where(same & (r >= c), 1.0, 0.0).astype(F32)


def _lane_onehot(lane):
    return jnp.where(lax.broadcasted_iota(I32, (CHUNK, LANES), 1) == lane, 1.0, 0.0).astype(F32)


def _conv_silu_pieces(x_ref, w_ref, buf_ref, cs, width):
    assert CONV_WIDTH - 1 <= CONV_CARRY

    @pl.when(pl.program_id(1) == 0)
    def _():
        buf_ref[0:CONV_CARRY, :] = jnp.zeros((CONV_CARRY, buf_ref.shape[1]), F32)

    buf_ref[CONV_CARRY:CONV_CARRY + cs, :] = x_ref[0]
    for r0 in range(0, cs, CHUNK):
        for c0 in range(0, buf_ref.shape[1], width):
            cols = slice(c0, c0 + width)
            top = CONV_CARRY + r0
            acc = w_ref[CONV_WIDTH - 1:CONV_WIDTH, cols] * buf_ref[top:top + CHUNK, cols]
            for j in range(CONV_WIDTH - 1):
                s = top - (CONV_WIDTH - 1) + j
                acc = acc + w_ref[j:j + 1, cols] * buf_ref[s:s + CHUNK, cols]
            yield r0, c0, _silu(acc)
    buf_ref[0:CONV_CARRY, :] = buf_ref[cs:cs + CONV_CARRY, :]


def _gdn_kernel(qkv_ref, z_ref, sm_ref, conv_ref, prm_ref, gn_ref, o_ref,
                buf_ref, q_s, k_s, v_s, w_s, attn_s, gam_s, beta_s, state_s, *, cs):
    n_chunks = cs // CHUNK

    @pl.when(pl.program_id(1) == 0)
    def _():
        state_s[...] = jnp.zeros(state_s.shape, F32)

    for r0, c0, piece in _conv_silu_pieces(qkv_ref, conv_ref, buf_ref, cs, D_QK):
        rows = slice(r0, r0 + CHUNK)
        if c0 == 2 * D_QK:
            v_s[rows, :] = piece
            continue
        dst, scale = (q_s, D_HEAD ** -0.5) if c0 == 0 else (k_s, 1.0)
        for h in range(N_HEADS):
            sl = slice(h * D_HEAD, (h + 1) * D_HEAD)
            a = piece[:, sl]
            dst[rows, sl] = a * (lax.rsqrt(jnp.sum(a * a, axis=-1, keepdims=True) + NORM_EPS) * scale)

    sm = sm_ref[0]
    logdec = -jnp.exp(prm_ref[0:1, :]) * _softplus(sm + prm_ref[1:2, :])
    gam_s[...] = _select_dot(_block_cumsum_matrix(cs), logdec)
    beta_s[...] = _sigmoid(sm)

    strict, incl = _chunk_masks()
    eye = jnp.where(lax.broadcasted_iota(I32, (CHUNK, CHUNK), 0)
                    == lax.broadcasted_iota(I32, (CHUNK, CHUNK), 1), 1.0, 0.0).astype(F32)
    gnorm = gn_ref[...]

    heads = range(N_HEADS)
    hsl = [slice(h * D_HEAD, (h + 1) * D_HEAD) for h in heads]
    asl = [slice(h * CHUNK, (h + 1) * CHUNK) for h in heads]
    onehots = [_lane_onehot(h) for h in heads]

    def local_body(it, carry):
        rows = [pl.ds(pl.multiple_of((it * LOCAL_CHUNKS + ci) * CHUNK, CHUNK), CHUNK)
                for ci in range(LOCAL_CHUNKS)]
        probs = [(ci, h) for ci in range(LOCAL_CHUNKS) for h in heads]
        gam_all = [gam_s[r, :] for r in rows]
        beta_all = [beta_s[r, :] for r in rows]
        q = [q_s[rows[ci], hsl[h]] for ci, h in probs]
        k = [k_s[rows[ci], hsl[h]] for ci, h in probs]
        v = [v_s[rows[ci], hsl[h]] for ci, h in probs]
        gc = [gam_all[ci][:, h:h + 1] for ci, h in probs]
        beta = [beta_all[ci][:, N_HEADS + h:N_HEADS + h + 1] for ci, h in probs]
        g_row = [_select_dot(onehots[h], gam_all[ci], _NT) for ci, h in probs]
        kb = [a * b for a, b in zip(k, beta)]
        kk = [_bdot(a, b, _NT) for a, b in zip(kb, k)]
        qk = [_bdot(a, b, _NT) for a, b in zip(q, k)]
        decay = [jnp.where(incl, jnp.exp(jnp.minimum(a - b, 0.0)), 0.0) for a, b in zip(gc, g_row)]
        lower = [jnp.where(strict, a * b, 0.0) for a, b in zip(kk, decay)]
        inv = [eye - a for a in lower]
        pw = [_bdot(a, a) for a in lower]
        for lvl in range(5):
            upd = [_bdot(a, b) for a, b in zip(inv, pw)]
            if lvl < 4:
                pw = [_bdot(a, a) for a in pw]
            inv = [a + b for a, b in zip(inv, upd)]
        e_gc = [jnp.exp(a) for a in gc]
        rhs = [jnp.concatenate([a * b, c * d], axis=1) for a, b, c, d in zip(v, beta, kb, e_gc)]
        sol = [_bdot(a, b) for a, b in zip(inv, rhs)]
        for p, (ci, h) in enumerate(probs):
            v_s[rows[ci], hsl[h]] = sol[p][:, :D_HEAD]
            w_s[rows[ci], hsl[h]] = sol[p][:, D_HEAD:]
            attn_s[rows[ci], asl[h]] = qk[p] * decay[p]
            q_s[rows[ci], hsl[h]] = q[p] * e_gc[p]
            k_s[rows[ci], hsl[h]] = k[p] * jnp.exp(gc[p][CHUNK - 1:CHUNK, :] - gc[p])
        return carry

    lax.fori_loop(0, n_chunks // LOCAL_CHUNKS, local_body, 0)

    def state_body(c, carry):
        rows = pl.ds(pl.multiple_of(c * CHUNK, CHUNK), CHUNK)
        g_last = gam_s[pl.ds(c * CHUNK + CHUNK - 1, 1), :]
        st = [state_s[h] for h in heads]
        ws = [_bdot(w_s[rows, hsl[h]], st[h]) for h in heads]
        qs = [_bdot(q_s[rows, hsl[h]], st[h]) for h in heads]
        u = [v_s[rows, hsl[h]] - ws[h] for h in heads]
        au = [_bdot(attn_s[rows, asl[h]], u[h]) for h in heads]
        ku = [_bdot(k_s[rows, hsl[h]], u[h], _TN) for h in heads]
        for h in heads:
            state_s[h] = st[h] * jnp.exp(g_last[:, h:h + 1]) + ku[h]
            o = qs[h] + au[h]
            o = o * lax.rsqrt(jnp.mean(o * o, axis=-1, keepdims=True) + NORM_EPS) * gnorm
            o_ref[0, rows, hsl[h]] = (o * _silu(z_ref[0, rows, hsl[h]])).astype(o_ref.dtype)
        return carry

    lax.fori_loop(0, n_chunks, state_body, 0)


def _gdn(big3, small3, conv_w, prm, gnorm, cs):
    b, s, _ = big3.shape
    return pl.pallas_call(
        functools.partial(_gdn_kernel, cs=cs),
        grid=(b, s // cs),
        in_specs=[pl.BlockSpec((1, cs, 3 * D_QK), lambda i, j: (i, j, 0)),
                  pl.BlockSpec((1, cs, D_QK), lambda i, j: (i, j, 3)),
                  pl.BlockSpec((1, cs, LANES), lambda i, j: (i, j, 0)),
                  pl.BlockSpec((CONV_WIDTH, 3 * D_QK), lambda i, j: (0, 0)),
                  pl.BlockSpec((SUBLANES, LANES), lambda i, j: (0, 0)),
                  pl.BlockSpec((1, D_HEAD), lambda i, j: (0, 0))],
        out_specs=pl.BlockSpec((1, cs, D_QK), lambda i, j: (i, j, 0)),
        out_shape=jax.ShapeDtypeStruct((b, s, D_QK), BF16),
        scratch_shapes=[pltpu.VMEM((cs + CONV_CARRY, 3 * D_QK), F32),
                        pltpu.VMEM((cs, D_QK), F32), pltpu.VMEM((cs, D_QK), F32),
                        pltpu.VMEM((cs, D_QK), F32), pltpu.VMEM((cs, D_QK), F32),
                        pltpu.VMEM((cs, N_HEADS * CHUNK), F32),
                        pltpu.VMEM((cs, LANES), F32), pltpu.VMEM((cs, LANES), F32),
                        pltpu.VMEM((N_HEADS, D_HEAD, D_HEAD), F32)],
        compiler_params=pltpu.CompilerParams(dimension_semantics=("arbitrary", "arbitrary"),
                                             vmem_limit_bytes=VMEM_LIMIT),
        name="gdn",
    )(big3, big3, small3, conv_w, prm, gnorm)


def _mlstm_kernel(qk_ref, v_ref, og_ref, sm_ref, conv_ref, prm_ref, gn_ref, o_ref,
                  buf_ref, q_s, k_s, bcum_s, ipre_s, state_s, m_s, *, cs):
    n_chunks = cs // CHUNK

    @pl.when(pl.program_id(1) == 0)
    def _():
        state_s[...] = jnp.zeros(state_s.shape, F32)
        m_s[...] = jnp.zeros(m_s.shape, F32)

    for r0, c0, piece in _conv_silu_pieces(qk_ref, conv_ref, buf_ref, cs, D_QK):
        if c0 == 0:
            q_s[r0:r0 + CHUNK, :] = piece
        else:
            k_s[r0:r0 + CHUNK, :] = piece * (D_HEAD ** -0.5)

    pre = sm_ref[0] + prm_ref[0:1, :]
    capped = GATE_SOFTCAP * jnp.tanh(pre / GATE_SOFTCAP)
    logf = -_softplus(-capped)
    bcum_s[...] = _select_dot(_block_cumsum_matrix(cs), logf)
    ipre_s[...] = pltpu.roll(capped, N_HEADS, axis=1)

    _, incl = _chunk_masks()
    ones_aug = jnp.ones((CHUNK, D_HEAD), F32)

    heads = range(N_HEADS)
    hsl = [slice(h * D_HEAD, (h + 1) * D_HEAD) for h in heads]
    lanes = [3 * N_HEADS + h for h in heads]
    onehots = [_lane_onehot(lane) for lane in lanes]

    def chunk_body(c, carry):
        rows = pl.ds(pl.multiple_of(c * CHUNK, CHUNK), CHUNK)
        b_all = bcum_s[rows, :]
        comb_all = ipre_s[rows, :] - b_all
        q = [q_s[rows, hsl[h]] for h in heads]
        k = [k_s[rows, hsl[h]] for h in heads]
        v_aug = [jnp.concatenate([v_ref[0, rows, hsl[h]], ones_aug], axis=1) for h in heads]
        st = [state_s[h] for h in heads]
        m_st = [m_s[h] for h in heads]
        bc = [b_all[:, lane:lane + 1] for lane in lanes]
        comb_row = [_select_dot(onehots[h], comb_all, _NT) for h in heads]
        qk = [_bdot(q[h], k[h], _NT) for h in heads]
        inter = [_bdot(q[h], st[h]) for h in heads]
        b_last = [bc[h][CHUNK - 1:CHUNK, :] for h in heads]
        a_log = [b_last[h] + comb_all[:, lanes[h]:lanes[h] + 1] for h in heads]
        m_chunk = [jnp.max(a_log[h], axis=0, keepdims=True) for h in heads]
        wk = [k[h] * jnp.exp(a_log[h] - m_chunk[h]) for h in heads]
        d_state = [_bdot(wk[h], v_aug[h], _TN) for h in heads]
        d_log = [jnp.where(incl, bc[h] + comb_row[h], -jnp.inf) for h in heads]
        m_loc = [jnp.max(d_log[h], axis=-1, keepdims=True) for h in heads]
        p = [jnp.exp(d_log[h] - m_loc[h]) * qk[h] for h in heads]
        loc = [_bdot(p[h], v_aug[h]) for h in heads]
        for h in heads:
            m_inter = bc[h] + m_st[h]
            m_t = jnp.maximum(m_loc[h], m_inter)
            tot = jnp.exp(m_inter - m_t) * inter[h] + jnp.exp(m_loc[h] - m_t) * loc[h]
            hh = tot[:, :D_HEAD] / jnp.maximum(jnp.abs(tot[:, D_HEAD:]), jnp.exp(-m_t))
            m_new = jnp.maximum(b_last[h] + m_st[h], m_chunk[h])
            state_s[h] = (jnp.exp(b_last[h] + m_st[h] - m_new) * st[h]
                          + jnp.exp(m_chunk[h] - m_new) * d_state[h])
            m_s[h] = m_new
            hh = hh * lax.rsqrt(jnp.mean(hh * hh, axis=-1, keepdims=True) + NORM_EPS) * gn_ref[:, hsl[h]]
            o_ref[0, rows, hsl[h]] = (hh * _sigmoid(og_ref[0, rows, hsl[h]])).astype(o_ref.dtype)
        return carry

    lax.fori_loop(0, n_chunks, chunk_body, 0)


def _mlstm(big3, small3, conv_w, prm, gnorm, cs):
    b, s, _ = big3.shape
    return pl.pallas_call(
        functools.partial(_mlstm_kernel, cs=cs),
        grid=(b, s // cs),
        in_specs=[pl.BlockSpec((1, cs, 2 * D_QK), lambda i, j: (i, j, 2)),
                  pl.BlockSpec((1, cs, D_QK), lambda i, j: (i, j, 6)),
                  pl.BlockSpec((1, cs, D_QK), lambda i, j: (i, j, 7)),
                  pl.BlockSpec((1, cs, LANES), lambda i, j: (i, j, 0)),
                  pl.BlockSpec((CONV_WIDTH, 2 * D_QK), lambda i, j: (0, 0)),
                  pl.BlockSpec((SUBLANES, LANES), lambda i, j: (0, 0)),
                  pl.BlockSpec((1, D_QK), lambda i, j: (0, 0))],
        out_specs=pl.BlockSpec((1, cs, D_QK), lambda i, j: (i, j, 0)),
        out_shape=jax.ShapeDtypeStruct((b, s, D_QK), BF16),
        scratch_shapes=[pltpu.VMEM((cs + CONV_CARRY, 2 * D_QK), F32),
                        pltpu.VMEM((cs, D_QK), F32), pltpu.VMEM((cs, D_QK), F32),
                        pltpu.VMEM((cs, LANES), F32), pltpu.VMEM((cs, LANES), F32),
                        pltpu.VMEM((N_HEADS, D_HEAD, 2 * D_HEAD), F32),
                        pltpu.VMEM((N_HEADS, 1, 1), F32)],
        compiler_params=pltpu.CompilerParams(dimension_semantics=("arbitrary", "arbitrary"),
                                             vmem_limit_bytes=VMEM_LIMIT),
        name="mlstm",
    )(big3, big3, big3, small3, conv_w, prm, gnorm)


N_SEG = D_MODEL // LANES
assert N_SEG == SUBLANES


def _store_token_tiles(ref, val, n):
    for s in range(N_SEG):
        ref[pl.ds(s, n, stride=N_SEG), :] = val[:, s * LANES:(s + 1) * LANES]


def _load_token_tiles(ref, n):
    return jnp.concatenate([ref[pl.ds(s, n, stride=N_SEG), :] for s in range(N_SEG)], axis=1)


def _merge_kernel(oa_ref, ob_ref, gates_ref, x_ref, wa_ref, wb_ref, wo_ref, g_ref, wr_ref, br_ref,
                  x1_ref, h2_ref, key_ref, gate_ref, cnt_ref, carry_s, *, tm):
    @pl.when(pl.program_id(0) == 0)
    def _():
        carry_s[...] = jnp.zeros(carry_s.shape, F32)

    y_a = jnp.dot(oa_ref[...], wa_ref[...], preferred_element_type=F32)
    y_b = jnp.dot(ob_ref[...], wb_ref[...], preferred_element_type=F32)
    y = _sigmoid(gates_ref[:, :D_MODEL]) * y_a + _sigmoid(gates_ref[:, D_MODEL:]) * y_b
    x1 = x_ref[...] + jnp.dot(y.astype(BF16), wo_ref[...], preferred_element_type=F32)
    x1_ref[...] = x1
    h2 = _rms(x1, g_ref[...])
    _store_token_tiles(h2_ref, h2, tm)
    logits = jnp.dot(h2.astype(BF16), wr_ref[...], preferred_element_type=F32) + br_ref[...]

    lane = lax.broadcasted_iota(I32, (tm, LANES), 1).astype(F32)
    vals, sels = [], []
    idx_t = jnp.zeros((tm, LANES), F32)
    work = logits
    for k in range(TOP_K):
        m = jnp.max(work, axis=-1, keepdims=True)
        i = jnp.min(jnp.where(work == m, lane, float(LANES)), axis=-1, keepdims=True)
        sel = lane == i
        work = jnp.where(sel, -jnp.inf, work)
        idx_t = jnp.where(lane == float(k), i, idx_t)
        vals.append(m)
        sels.append(sel)
    es = [jnp.exp(v - vals[0]) for v in vals]
    denom = es[0] + es[1] + es[2] + es[3]
    gate_t = jnp.zeros((tm, LANES), F32)
    for k in range(TOP_K):
        gate_t = jnp.where(lane == float(k), es[k] / denom, gate_t)
    gate_ref[...] = gate_t[:, :TOP_K]

    onehot = jnp.zeros((tm, LANES), F32)
    for sel in sels:
        onehot = onehot + jnp.where(sel, 1.0, 0.0)
    r = lax.broadcasted_iota(I32, (tm, tm), 0)
    c = lax.broadcasted_iota(I32, (tm, tm), 1)
    before = jnp.where(r > c, 1.0, 0.0).astype(BF16)
    cum = jnp.dot(before, onehot.astype(BF16), preferred_element_type=F32) + carry_s[...]
    rank_t = jnp.zeros((tm, LANES), F32)
    for k, sel in enumerate(sels):
        rk = jnp.sum(jnp.where(sel, cum, 0.0), axis=-1, keepdims=True)
        rank_t = jnp.where(lane == float(k), rk, rank_t)
    key_ref[...] = (idx_t * float(RANK_SPAN) + rank_t)[:, :TOP_K].astype(I32)
    total = carry_s[...] + jnp.sum(onehot, axis=0, keepdims=True)
    carry_s[...] = total
    cnt_ref[...] = total.astype(I32)


def _merge(oa, ob, big, x2, wa, wb, wo, g, wr, br, tm):
    t = x2.shape[0]
    const = lambda i: (0, 0)
    return pl.pallas_call(
        functools.partial(_merge_kernel, tm=tm),
        grid=(t // tm,),
        in_specs=[pl.BlockSpec((tm, D_QK), lambda i: (i, 0)),
                  pl.BlockSpec((tm, D_QK), lambda i: (i, 0)),
                  pl.BlockSpec((tm, 2 * D_MODEL), lambda i: (i, 2)),
                  pl.BlockSpec((tm, D_MODEL), lambda i: (i, 0)),
                  pl.BlockSpec((D_QK, D_MODEL), const),
                  pl.BlockSpec((D_QK, D_MODEL), const),
                  pl.BlockSpec((D_MODEL, D_MODEL), const),
                  pl.BlockSpec((1, D_MODEL), const),
                  pl.BlockSpec((D_MODEL, LANES), const),
                  pl.BlockSpec((1, LANES), const)],
        out_specs=[pl.BlockSpec((tm, D_MODEL), lambda i: (i, 0)),
                   pl.BlockSpec((tm * N_SEG, LANES), lambda i: (i, 0)),
                   pl.BlockSpec((tm, TOP_K), lambda i: (i, 0)),
                   pl.BlockSpec((tm, TOP_K), lambda i: (i, 0)),
                   pl.BlockSpec((1, LANES), const)],
        out_shape=[jax.ShapeDtypeStruct((t, D_MODEL), F32),
                   jax.ShapeDtypeStruct((t * N_SEG, LANES), F32),
                   jax.ShapeDtypeStruct((t, TOP_K), I32),
                   jax.ShapeDtypeStruct((t, TOP_K), F32),
                   jax.ShapeDtypeStruct((1, LANES), I32)],
        scratch_shapes=[pltpu.VMEM((1, LANES), F32)],
        compiler_params=pltpu.CompilerParams(dimension_semantics=("arbitrary",),
                                             vmem_limit_bytes=VMEM_LIMIT),
        name="merge_router",
    )(oa, ob, big, x2, wa, wb, wo, g, wr, br)


def _expert_kernel(te_ref, nu_ref, nv_ref, tok_ref, tok_next_ref, slot_ref, slot_prev_ref, h2_hbm,
                   wgu_ref, wd_ref, perm_ref, bg_ref, bu_ref, bd_ref, y_hbm,
                   xbuf0, xbuf1, ybuf0, ybuf1, wg_s, wu_s, wd_s, in_sem, out_sem, *, bm):
    i = pl.program_id(0)
    n_used = nu_ref[0]
    new_expert = (i == 0) | (te_ref[i] != te_ref[jnp.maximum(i - 1, 0)])
    xbufs = (xbuf0, xbuf1)
    ybufs = (ybuf0, ybuf1)

    def token_tile(ref, r):
        return ref.at[pl.ds(pl.multiple_of(r * N_SEG, N_SEG), N_SEG)]

    def gather_row(ids_ref, s, r):
        pltpu.make_async_copy(token_tile(h2_hbm, ids_ref[r]), token_tile(xbufs[s], r),
                              in_sem.at[s]).start()

    def wait_gather(s):
        pltpu.make_async_copy(h2_hbm.at[pl.ds(0, bm * N_SEG)], xbufs[s], in_sem.at[s]).wait()

    def wait_scatter(s, n_rows):
        n = pl.multiple_of(n_rows * N_SEG, N_SEG)
        pltpu.make_async_copy(ybufs[s].at[pl.ds(0, n)], y_hbm.at[pl.ds(0, n)], out_sem.at[s]).wait()

    @pl.when(i == 0)
    def _():
        def body(r, carry):
            gather_row(tok_ref, 0, r)
            return carry
        lax.fori_loop(0, bm, body, 0, unroll=DMA_UNROLL)

    n_prev = jnp.where(i >= 1, nv_ref[jnp.maximum(i - 1, 0)], 0)

    def scatter_row(ids_ref, s, r):
        pltpu.make_async_copy(token_tile(ybufs[s], r), token_tile(y_hbm, ids_ref[r]),
                              out_sem.at[s]).start()

    def scatter_loop(ids_ref, s, n_rows):
        def one(r, carry):
            scatter_row(ids_ref, s, r)
            return carry

        def group(j, carry):
            for q in range(DMA_UNROLL):
                scatter_row(ids_ref, s, j * DMA_UNROLL + q)
            return carry
        n_groups = n_rows // DMA_UNROLL
        lax.fori_loop(0, n_groups, group, 0)
        lax.fori_loop(n_groups * DMA_UNROLL, n_rows, one, 0)

    def tile_body(cur, prev_full):
        nxt = 1 - cur
        wait_gather(cur)

        @pl.when(i >= 2)
        def _():
            wait_scatter(cur, nv_ref[i - 2])

        if not prev_full:
            scatter_loop(slot_prev_ref, nxt, n_prev)

        @pl.when(new_expert)
        def _():
            half = PERM_BLOCK // 2
            for j in range(2 * D_EXPERT // PERM_BLOCK):
                blk = wgu_ref[0, :, j * PERM_BLOCK:(j + 1) * PERM_BLOCK].astype(BF16)
                split = jnp.dot(blk, perm_ref[...], preferred_element_type=F32)
                wg_s[:, j * half:(j + 1) * half] = split[:, :half].astype(BF16)
                wu_s[:, j * half:(j + 1) * half] = split[:, half:].astype(BF16)
            wd_s[...] = wd_ref[0].astype(BF16)

        n_grp = bm // DMA_GROUPS

        def dma_group(gi):
            for r in range(gi * n_grp, (gi + 1) * n_grp):
                gather_row(tok_next_ref, nxt, r)
                if prev_full:
                    scatter_row(slot_prev_ref, nxt, r)

        dma_group(0)
        x = _load_token_tiles(xbufs[cur], bm).astype(BF16)
        g = jnp.dot(x, wg_s[...], preferred_element_type=F32) + bg_ref[0]
        dma_group(1)
        u = jnp.dot(x, wu_s[...], preferred_element_type=F32) + bu_ref[0]
        dma_group(2)
        gate = jnp.minimum(g, SWIGLU_LIMIT)
        up = jnp.clip(u, -SWIGLU_LIMIT, SWIGLU_LIMIT)
        act = gate * _sigmoid(SWIGLU_ALPHA * gate) * (up + 1.0)
        dma_group(3)
        y = jnp.dot(act.astype(BF16), wd_s[...], preferred_element_type=F32) + bd_ref[0]
        _store_token_tiles(ybufs[cur], y, bm)

        @pl.when(i == n_used - 1)
        def _():
            scatter_loop(slot_ref, cur, nv_ref[i])
            wait_gather(nxt)
            wait_scatter(cur, nv_ref[i])

            @pl.when(i >= 1)
            def _():
                wait_scatter(nxt, nv_ref[i - 1])

    for parity in range(2):
        for prev_full in (False, True):
            full = (n_prev == bm) if prev_full else (n_prev != bm)
            cond = (i < n_used) & (lax.rem(i, 2) == parity) & full
            pl.when(cond)(functools.partial(tile_body, parity, prev_full))


def _split_permutation():
    half = PERM_BLOCK // 2
    src = jnp.arange(PERM_BLOCK)[:, None]
    dst = jnp.arange(PERM_BLOCK)[None, :]
    return (src == jnp.where(dst < half, 2 * dst, 2 * (dst - half) + 1)).astype(BF16)


def _experts(tile_expert, n_used, tile_valid, row_tok, row_slot, h2, wgu, wd, bg, bu, bd, bm):
    n_rows = row_tok.shape[0]
    n_tiles = n_rows // bm
    n_slots = h2.shape[0] // N_SEG * TOP_K
    cur = lambda i, te, nu, nv: (jnp.minimum(i, nu[0] - 1),)
    nxt = lambda i, te, nu, nv: (jnp.minimum(i + 1, nu[0] - 1),)
    prv = lambda i, te, nu, nv: (jnp.clip(i - 1, 0, nu[0] - 1),)
    w_map = lambda i, te, nu, nv: (te[i], 0, 0)
    grid_spec = pltpu.PrefetchScalarGridSpec(
        num_scalar_prefetch=3,
        grid=(n_tiles,),
        in_specs=[pl.BlockSpec((bm,), cur, memory_space=pltpu.SMEM),
                  pl.BlockSpec((bm,), nxt, memory_space=pltpu.SMEM),
                  pl.BlockSpec((bm,), cur, memory_space=pltpu.SMEM),
                  pl.BlockSpec((bm,), prv, memory_space=pltpu.SMEM),
                  pl.BlockSpec(memory_space=pl.ANY),
                  pl.BlockSpec((1, D_MODEL, 2 * D_EXPERT), w_map),
                  pl.BlockSpec((1, D_EXPERT, D_MODEL), w_map),
                  pl.BlockSpec((PERM_BLOCK, PERM_BLOCK), lambda i, te, nu, nv: (0, 0)),
                  pl.BlockSpec((1, 1, D_EXPERT), w_map),
                  pl.BlockSpec((1, 1, D_EXPERT), w_map),
                  pl.BlockSpec((1, 1, D_MODEL), w_map)],
        out_specs=pl.BlockSpec(memory_space=pl.ANY),
        scratch_shapes=[pltpu.VMEM((bm * N_SEG, LANES), F32), pltpu.VMEM((bm * N_SEG, LANES), F32),
                        pltpu.VMEM((bm * N_SEG, LANES), F32), pltpu.VMEM((bm * N_SEG, LANES), F32),
                        pltpu.VMEM((D_MODEL, D_EXPERT), BF16), pltpu.VMEM((D_MODEL, D_EXPERT), BF16),
                        pltpu.VMEM((D_EXPERT, D_MODEL), BF16),
                        pltpu.SemaphoreType.DMA((2,)), pltpu.SemaphoreType.DMA((2,))],
    )
    return pl.pallas_call(
        functools.partial(_expert_kernel, bm=bm),
        grid_spec=grid_spec,
        out_shape=jax.ShapeDtypeStruct((n_slots * N_SEG, LANES), F32),
        compiler_params=pltpu.CompilerParams(dimension_semantics=("arbitrary",),
                                             vmem_limit_bytes=VMEM_LIMIT,
                                             has_side_effects=True),
        name="experts",
    )(tile_expert, n_used, tile_valid, row_tok, row_tok, row_slot, row_slot, h2, wgu, wd, _split_permutation(),
      bg, bu, bd)


def _combine_kernel(y0_ref, y1_ref, y2_ref, y3_ref, gate_ref, x1_ref, g_ref, o_ref, *, tm):
    acc = x1_ref[...]
    for k, y_ref in enumerate((y0_ref, y1_ref, y2_ref, y3_ref)):
        acc = acc + gate_ref[:, k:k + 1] * _load_token_tiles(y_ref, tm)
    o_ref[...] = _rms(acc, g_ref[...])


def _combine(y_slots, gates, x1, g, tm):
    t = x1.shape[0]
    nb = t // tm

    def slot_spec(k):
        return pl.BlockSpec((tm * N_SEG, LANES), lambda i: (k * nb + i, 0))

    return pl.pallas_call(
        functools.partial(_combine_kernel, tm=tm),
        grid=(nb,),
        in_specs=[slot_spec(0), slot_spec(1), slot_spec(2), slot_spec(3),
                  pl.BlockSpec((tm, TOP_K), lambda i: (i, 0)),
                  pl.BlockSpec((tm, D_MODEL), lambda i: (i, 0)),
                  pl.BlockSpec((1, D_MODEL), lambda i: (0, 0))],
        out_specs=pl.BlockSpec((tm, D_MODEL), lambda i: (i, 0)),
        out_shape=jax.ShapeDtypeStruct((t, D_MODEL), F32),
        compiler_params=pltpu.CompilerParams(dimension_semantics=("arbitrary",),
                                             vmem_limit_bytes=VMEM_LIMIT),
        name="combine",
    )(y_slots, y_slots, y_slots, y_slots, gates, x1, g)


def _lane_row(vec, lane0):
    n = vec.shape[0]
    return jnp.zeros((SUBLANES, LANES), F32).at[0, lane0:lane0 + n].set(vec.astype(F32))


def _tile_size(n, pref):
    return pref if n % pref == 0 else n


def kernel(x, norm_mix, w_in, gdn_conv, gdn_a_log, gdn_dt_bias, gdn_norm, ml_conv, ml_b_i, ml_b_f,
           ml_norm, w_up_gdn, w_up_ml, w_out, norm_ffn, w_router, b_router, w_gate_up, b_gate_up,
           w_down, b_down, norm_final):
    assert norm_mix.shape[0] == 1, "single-layer stack"
    b, s, d = x.shape
    assert d == D_MODEL and s % CHUNK == 0
    t = b * s
    x2 = x.reshape(t, d)

    w = w_in[0]
    w_big = jnp.concatenate([_cols(w, n) for n in ("g_q", "g_k", "g_v", "g_z", "m_q", "m_k", "m_v",
                                                   "m_o", "gate_gdn", "gate_ml")], axis=1).astype(BF16)
    w_small = jnp.concatenate([_cols(w, n) for n in ("g_a", "g_b", "m_i", "m_f")], axis=1)
    w_small = jnp.pad(w_small, ((0, 0), (0, LANES - N_SMALL))).astype(BF16)

    tm = _tile_size(t, 512)
    big, small = _in_proj(x2, norm_mix[0][None, :], w_big, w_small, tm)
    big3 = big.reshape(b, s, N_BIG)
    small3 = small.reshape(b, s, LANES)

    cs = _tile_size(s, 512)
    gdn_prm = _lane_row(gdn_a_log[0], 0).at[1, 0:N_HEADS].set(gdn_dt_bias[0].astype(F32))
    oa = _gdn(big3, small3, gdn_conv[0].astype(F32), gdn_prm, gdn_norm[0][None, :].astype(F32), cs)
    ml_prm = _lane_row(ml_b_i[0], 2 * N_HEADS).at[0, 3 * N_HEADS:4 * N_HEADS].set(ml_b_f[0].astype(F32))
    ob = _mlstm(big3, small3, ml_conv[0].astype(F32), ml_prm, ml_norm[0][None, :].astype(F32), cs)

    w_r = jnp.pad(w_router[0], ((0, 0), (0, LANES - N_EXPERTS))).astype(BF16)
    b_r = jnp.full((1, LANES), PAD_LOGIT, F32).at[0, :N_EXPERTS].set(b_router[0].astype(F32))
    x1, h2, key, gates, counts = _merge(
        oa.reshape(t, D_QK), ob.reshape(t, D_QK), big, x2, w_up_gdn[0].astype(BF16),
        w_up_ml[0].astype(BF16), w_out[0].astype(BF16), norm_ffn[0][None, :], w_r, b_r, tm)

    bm = EXPERT_TILE
    n_assign = t * TOP_K
    n_tiles = -(-n_assign // bm) + N_EXPERTS
    counts = counts[0, :N_EXPERTS]
    padded = (counts + bm - 1) // bm * bm
    pend = jnp.cumsum(padded)
    pstart = pend - padded
    n_used = (pend[-1] // bm).astype(I32)
    tile_ids = jnp.minimum(jnp.arange(n_tiles, dtype=I32), n_used - 1)
    tile_expert = jnp.minimum(jnp.sum((pend[None, :] <= (tile_ids * bm)[:, None]).astype(I32), axis=1),
                              N_EXPERTS - 1)
    tile_valid = jnp.clip(counts[tile_expert] - (tile_ids * bm - pstart[tile_expert]), 0, bm).astype(I32)
    assert t <= RANK_SPAN and bm <= RANK_SPAN
    assign = jnp.arange(n_assign, dtype=I32)
    pad_e = jnp.arange(N_EXPERTS, dtype=I32)[:, None]
    pad_p = jnp.arange(bm, dtype=I32)[None, :]
    pad_key = jnp.where(pad_p < (padded - counts)[:, None],
                        pad_e * RANK_SPAN + counts[:, None] + pad_p,
                        N_EXPERTS * RANK_SPAN + pad_e * bm + pad_p)
    _, row_slot = lax.sort((jnp.concatenate([key.reshape(-1), pad_key.reshape(-1)]),
                            jnp.concatenate([(assign % TOP_K) * t + assign // TOP_K,
                                             jnp.full((N_EXPERTS * bm,), -1, I32)])),
                           num_keys=1)
    row_tok = jnp.where(row_slot < 0, jnp.arange(n_tiles * bm, dtype=I32), row_slot) % t

    bgu = b_gate_up[0]
    y_slots = _experts(tile_expert.astype(I32), n_used.reshape(1), tile_valid, row_tok, row_slot, h2,
                       w_gate_up[0], w_down[0],
                       bgu[:, None, 0::2].astype(F32), bgu[:, None, 1::2].astype(F32),
                       b_down[0][:, None, :].astype(F32), bm)
    out = _combine(y_slots, gates, x1, norm_final[None, :], tm)
    return out.reshape(b, s, d)
```

```python
import functools

import jax
import jax.numpy as jnp
from jax import lax
from jax.experimental import pallas as pl
from jax.experimental.pallas import tpu as pltpu

F32 = jnp.float32
BF16 = jnp.bfloat16
I32 = jnp.int32

D_MODEL = 1024
N_HEADS = 4
D_HEAD = 128
D_QK = N_HEADS * D_HEAD
CONV_WIDTH = 4
CHUNK = 64
GATE_SOFTCAP = 15.0
N_EXPERTS = 32
TOP_K = 4
D_EXPERT = 1024
SWIGLU_LIMIT = 7.0
SWIGLU_ALPHA = 1.702
NORM_EPS = 1e-6

LANES = 128
SUBLANES = 8
CONV_CARRY = SUBLANES
PAD_LOGIT = -1e30
N_SMALL = 16
N_BIG = 6 * D_MODEL
VMEM_LIMIT = 56 * 1024 * 1024
EXPERT_TILE = 512
DMA_UNROLL = 8
DMA_GROUPS = 4
RANK_SPAN = 1 << 18
PERM_BLOCK = 256
LOCAL_CHUNKS = 8

_OFF = {}
_o = 0
for _name, _w in (("g_q", D_QK), ("g_k", D_QK), ("g_v", D_QK), ("g_z", D_QK), ("g_a", N_HEADS),
                  ("g_b", N_HEADS), ("m_q", D_QK), ("m_k", D_QK), ("m_v", D_QK), ("m_o", D_QK),
                  ("m_i", N_HEADS), ("m_f", N_HEADS), ("gate_gdn", D_MODEL), ("gate_ml", D_MODEL)):
    _OFF[_name] = (_o, _w)
    _o += _w


def _cols(w, name):
    o, n = _OFF[name]
    return w[:, o:o + n]


_NN = (((1,), (0,)), ((), ()))
_NT = (((1,), (1,)), ((), ()))
_TN = (((0,), (0,)), ((), ()))


def _bdot(a, b, dims=_NN):
    return lax.dot_general(a.astype(BF16), b.astype(BF16), dims, preferred_element_type=F32)


def _split3(a):
    hi = a.astype(BF16)
    r = a - hi.astype(F32)
    mid = r.astype(BF16)
    return hi, mid, (r - mid.astype(F32)).astype(BF16)


def _select_dot(sel01, x, dims=_NN):
    s = sel01.astype(BF16)
    h, m, l = _split3(x)
    d = functools.partial(lax.dot_general, dimension_numbers=dims, preferred_element_type=F32)
    return d(s, h) + (d(s, m) + d(s, l))


def _sigmoid(x):
    return 1.0 / (1.0 + jnp.exp(-x))


def _silu(x):
    return x * _sigmoid(x)


def _softplus(x):
    return jnp.maximum(x, 0.0) + jnp.log(1.0 + jnp.exp(-jnp.abs(x)))


def _rms(x, g):
    return x * lax.rsqrt(jnp.mean(x * x, axis=-1, keepdims=True) + NORM_EPS) * g


def _in_proj_kernel(x_ref, g_ref, wbig_ref, wsmall_ref, big_ref, small_ref, *, n_chunk):
    h = _rms(x_ref[...], g_ref[...]).astype(BF16)
    for c in range(N_BIG // n_chunk):
        sl = slice(c * n_chunk, (c + 1) * n_chunk)
        big_ref[:, sl] = jnp.dot(h, wbig_ref[:, sl], preferred_element_type=F32)
    small_ref[...] = jnp.dot(h, wsmall_ref[...], preferred_element_type=F32)


def _in_proj(x2, g, w_big, w_small, tm):
    t = x2.shape[0]
    return pl.pallas_call(
        functools.partial(_in_proj_kernel, n_chunk=1024),
        grid=(t // tm,),
        in_specs=[pl.BlockSpec((tm, D_MODEL), lambda i: (i, 0)),
                  pl.BlockSpec((1, D_MODEL), lambda i: (0, 0)),
                  pl.BlockSpec((D_MODEL, N_BIG), lambda i: (0, 0)),
                  pl.BlockSpec((D_MODEL, LANES), lambda i: (0, 0))],
        out_specs=[pl.BlockSpec((tm, N_BIG), lambda i: (i, 0)),
                   pl.BlockSpec((tm, LANES), lambda i: (i, 0))],
        out_shape=[jax.ShapeDtypeStruct((t, N_BIG), F32),
                   jax.ShapeDtypeStruct((t, LANES), F32)],
        compiler_params=pltpu.CompilerParams(dimension_semantics=("arbitrary",),
                                             vmem_limit_bytes=VMEM_LIMIT),
        name="in_proj",
    )(x2, g, w_big, w_small)


def _chunk_masks():
    r = lax.broadcasted_iota(I32, (CHUNK, CHUNK), 0)
    c = lax.broadcasted_iota(I32, (CHUNK, CHUNK), 1)
    return r > c, r >= c


def _block_cumsum_matrix(cs):
    r = lax.broadcasted_iota(I32, (cs, cs), 0)
    c = lax.broadcasted_iota(I32, (cs, cs), 1)
    same = (r // CHUNK) == (c // CHUNK)
    return jnp.where(same & (r >= c), 1.0, 0.0).astype(F32)


def _lane_onehot(lane):
    return jnp.where(lax.broadcasted_iota(I32, (CHUNK, LANES), 1) == lane, 1.0, 0.0).astype(F32)


def _conv_silu_pieces(x_ref, w_ref, buf_ref, cs, width):
    assert CONV_WIDTH - 1 <= CONV_CARRY

    @pl.when(pl.program_id(1) == 0)
    def _():
        buf_ref[0:CONV_CARRY, :] = jnp.zeros((CONV_CARRY, buf_ref.shape[1]), F32)

    buf_ref[CONV_CARRY:CONV_CARRY + cs, :] = x_ref[0]
    for r0 in range(0, cs, CHUNK):
        for c0 in range(0, buf_ref.shape[1], width):
            cols = slice(c0, c0 + width)
            top = CONV_CARRY + r0
            acc = w_ref[CONV_WIDTH - 1:CONV_WIDTH, cols] * buf_ref[top:top + CHUNK, cols]
            for j in range(CONV_WIDTH - 1):
                s = top - (CONV_WIDTH - 1) + j
                acc = acc + w_ref[j:j + 1, cols] * buf_ref[s:s + CHUNK, cols]
            yield r0, c0, _silu(acc)
    buf_ref[0:CONV_CARRY, :] = buf_ref[cs:cs + CONV_CARRY, :]


def _gdn_kernel(qkv_ref, z_ref, sm_ref, conv_ref, prm_ref, gn_ref, o_ref,
                buf_ref, q_s, k_s, v_s, w_s, attn_s, gam_s, beta_s, state_s, *, cs):
    n_chunks = cs // CHUNK

    @pl.when(pl.program_id(1) == 0)
    def _():
        state_s[...] = jnp.zeros(state_s.shape, F32)

    for r0, c0, piece in _conv_silu_pieces(qkv_ref, conv_ref, buf_ref, cs, D_QK):
        rows = slice(r0, r0 + CHUNK)
        if c0 == 2 * D_QK:
            v_s[rows, :] = piece
            continue
        dst, scale = (q_s, D_HEAD ** -0.5) if c0 == 0 else (k_s, 1.0)
        for h in range(N_HEADS):
            sl = slice(h * D_HEAD, (h + 1) * D_HEAD)
            a = piece[:, sl]
            dst[rows, sl] = a * (lax.rsqrt(jnp.sum(a * a, axis=-1, keepdims=True) + NORM_EPS) * scale)

    sm = sm_ref[0]
    logdec = -jnp.exp(prm_ref[0:1, :]) * _softplus(sm + prm_ref[1:2, :])
    gam_s[...] = _select_dot(_block_cumsum_matrix(cs), logdec)
    beta_s[...] = _sigmoid(sm)

    strict, incl = _chunk_masks()
    eye = jnp.where(lax.broadcasted_iota(I32, (CHUNK, CHUNK), 0)
                    == lax.broadcasted_iota(I32, (CHUNK, CHUNK), 1), 1.0, 0.0).astype(F32)
    gnorm = gn_ref[...]

    heads = range(N_HEADS)
    hsl = [slice(h * D_HEAD, (h + 1) * D_HEAD) for h in heads]
    asl = [slice(h * CHUNK, (h + 1) * CHUNK) for h in heads]
    onehots = [_lane_onehot(h) for h in heads]

    def local_body(it, carry):
        rows = [pl.ds(pl.multiple_of((it * LOCAL_CHUNKS + ci) * CHUNK, CHUNK), CHUNK)
                for ci in range(LOCAL_CHUNKS)]
        probs = [(ci, h) for ci in range(LOCAL_CHUNKS) for h in heads]
        gam_all = [gam_s[r, :] for r in rows]
        beta_all = [beta_s[r, :] for r in rows]
        q = [q_s[rows[ci], hsl[h]] for ci, h in probs]
        k = [k_s[rows[ci], hsl[h]] for ci, h in probs]
        v = [v_s[rows[ci], hsl[h]] for ci, h in probs]
        gc = [gam_all[ci][:, h:h + 1] for ci, h in probs]
        beta = [beta_all[ci][:, N_HEADS + h:N_HEADS + h + 1] for ci, h in probs]
        g_row = [_select_dot(onehots[h], gam_all[ci], _NT) for ci, h in probs]
        kb = [a * b for a, b in zip(k, beta)]
        kk = [_bdot(a, b, _NT) for a, b in zip(kb, k)]
        qk = [_bdot(a, b, _NT) for a, b in zip(q, k)]
        decay = [jnp.where(incl, jnp.exp(jnp.minimum(a - b, 0.0)), 0.0) for a, b in zip(gc, g_row)]
        lower = [jnp.where(strict, a * b, 0.0) for a, b in zip(kk, decay)]
        inv = [eye - a for a in lower]
        pw = [_bdot(a, a) for a in lower]
        for lvl in range(5):
            upd = [_bdot(a, b) for a, b in zip(inv, pw)]
            if lvl < 4:
                pw = [_bdot(a, a) for a in pw]
            inv = [a + b for a, b in zip(inv, upd)]
        e_gc = [jnp.exp(a) for a in gc]
        rhs = [jnp.concatenate([a * b, c * d], axis=1) for a, b, c, d in zip(v, beta, kb, e_gc)]
        sol = [_bdot(a, b) for a, b in zip(inv, rhs)]
        for p, (ci, h) in enumerate(probs):
            v_s[rows[ci], hsl[h]] = sol[p][:, :D_HEAD]
            w_s[rows[ci], hsl[h]] = sol[p][:, D_HEAD:]
            attn_s[rows[ci], asl[h]] = qk[p] * decay[p]
            q_s[rows[ci], hsl[h]] = q[p] * e_gc[p]
            k_s[rows[ci], hsl[h]] = k[p] * jnp.exp(gc[p][CHUNK - 1:CHUNK, :] - gc[p])
        return carry

    lax.fori_loop(0, n_chunks // LOCAL_CHUNKS, local_body, 0)

    def state_body(c, carry):
        rows = pl.ds(pl.multiple_of(c * CHUNK, CHUNK), CHUNK)
        g_last = gam_s[pl.ds(c * CHUNK + CHUNK - 1, 1), :]
        st = [state_s[h] for h in heads]
        ws = [_bdot(w_s[rows, hsl[h]], st[h]) for h in heads]
        qs = [_bdot(q_s[rows, hsl[h]], st[h]) for h in heads]
        u = [v_s[rows, hsl[h]] - ws[h] for h in heads]
        au = [_bdot(attn_s[rows, asl[h]], u[h]) for h in heads]
        ku = [_bdot(k_s[rows, hsl[h]], u[h], _TN) for h in heads]
        for h in heads:
            state_s[h] = st[h] * jnp.exp(g_last[:, h:h + 1]) + ku[h]
            o = qs[h] + au[h]
            o = o * lax.rsqrt(jnp.mean(o * o, axis=-1, keepdims=True) + NORM_EPS) * gnorm
            o_ref[0, rows, hsl[h]] = (o * _silu(z_ref[0, rows, hsl[h]])).astype(o_ref.dtype)
        return carry

    lax.fori_loop(0, n_chunks, state_body, 0)


def _gdn(big3, small3, conv_w, prm, gnorm, cs):
    b, s, _ = big3.shape
    return pl.pallas_call(
        functools.partial(_gdn_kernel, cs=cs),
        grid=(b, s // cs),
        in_specs=[pl.BlockSpec((1, cs, 3 * D_QK), lambda i, j: (i, j, 0)),
                  pl.BlockSpec((1, cs, D_QK), lambda i, j: (i, j, 3)),
                  pl.BlockSpec((1, cs, LANES), lambda i, j: (i, j, 0)),
                  pl.BlockSpec((CONV_WIDTH, 3 * D_QK), lambda i, j: (0, 0)),
                  pl.BlockSpec((SUBLANES, LANES), lambda i, j: (0, 0)),
                  pl.BlockSpec((1, D_HEAD), lambda i, j: (0, 0))],
        out_specs=pl.BlockSpec((1, cs, D_QK), lambda i, j: (i, j, 0)),
        out_shape=jax.ShapeDtypeStruct((b, s, D_QK), BF16),
        scratch_shapes=[pltpu.VMEM((cs + CONV_CARRY, 3 * D_QK), F32),
                        pltpu.VMEM((cs, D_QK), F32), pltpu.VMEM((cs, D_QK), F32),
                        pltpu.VMEM((cs, D_QK), F32), pltpu.VMEM((cs, D_QK), F32),
                        pltpu.VMEM((cs, N_HEADS * CHUNK), F32),
                        pltpu.VMEM((cs, LANES), F32), pltpu.VMEM((cs, LANES), F32),
                        pltpu.VMEM((N_HEADS, D_HEAD, D_HEAD), F32)],
        compiler_params=pltpu.CompilerParams(dimension_semantics=("arbitrary", "arbitrary"),
                                             vmem_limit_bytes=VMEM_LIMIT),
        name="gdn",
    )(big3, big3, small3, conv_w, prm, gnorm)


def _mlstm_kernel(qk_ref, v_ref, og_ref, sm_ref, conv_ref, prm_ref, gn_ref, o_ref,
                  buf_ref, q_s, k_s, bcum_s, ipre_s, state_s, m_s, *, cs):
    n_chunks = cs // CHUNK

    @pl.when(pl.program_id(1) == 0)
    def _():
        state_s[...] = jnp.zeros(state_s.shape, F32)
        m_s[...] = jnp.zeros(m_s.shape, F32)

    for r0, c0, piece in _conv_silu_pieces(qk_ref, conv_ref, buf_ref, cs, D_QK):
        if c0 == 0:
            q_s[r0:r0 + CHUNK, :] = piece
        else:
            k_s[r0:r0 + CHUNK, :] = piece * (D_HEAD ** -0.5)

    pre = sm_ref[0] + prm_ref[0:1, :]
    capped = GATE_SOFTCAP * jnp.tanh(pre / GATE_SOFTCAP)
    logf = -_softplus(-capped)
    bcum_s[...] = _select_dot(_block_cumsum_matrix(cs), logf)
    ipre_s[...] = pltpu.roll(capped, N_HEADS, axis=1)

    _, incl = _chunk_masks()
    ones_aug = jnp.ones((CHUNK, D_HEAD), F32)

    heads = range(N_HEADS)
    hsl = [slice(h * D_HEAD, (h + 1) * D_HEAD) for h in heads]
    lanes = [3 * N_HEADS + h for h in heads]
    onehots = [_lane_onehot(lane) for lane in lanes]

    def chunk_body(c, carry):
        rows = pl.ds(pl.multiple_of(c * CHUNK, CHUNK), CHUNK)
        b_all = bcum_s[rows, :]
        comb_all = ipre_s[rows, :] - b_all
        q = [q_s[rows, hsl[h]] for h in heads]
        k = [k_s[rows, hsl[h]] for h in heads]
        v_aug = [jnp.concatenate([v_ref[0, rows, hsl[h]], ones_aug], axis=1) for h in heads]
        st = [state_s[h] for h in heads]
        m_st = [m_s[h] for h in heads]
        bc = [b_all[:, lane:lane + 1] for lane in lanes]
        comb_row = [_select_dot(onehots[h], comb_all, _NT) for h in heads]
        qk = [_bdot(q[h], k[h], _NT) for h in heads]
        inter = [_bdot(q[h], st[h]) for h in heads]
        b_last = [bc[h][CHUNK - 1:CHUNK, :] for h in heads]
        a_log = [b_last[h] + comb_all[:, lanes[h]:lanes[h] + 1] for h in heads]
        m_chunk = [jnp.max(a_log[h], axis=0, keepdims=True) for h in heads]
        wk = [k[h] * jnp.exp(a_log[h] - m_chunk[h]) for h in heads]
        d_state = [_bdot(wk[h], v_aug[h], _TN) for h in heads]
        d_log = [jnp.where(incl, bc[h] + comb_row[h], -jnp.inf) for h in heads]
        m_loc = [jnp.max(d_log[h], axis=-1, keepdims=True) for h in heads]
        p = [jnp.exp(d_log[h] - m_loc[h]) * qk[h] for h in heads]
        loc = [_bdot(p[h], v_aug[h]) for h in heads]
        for h in heads:
            m_inter = bc[h] + m_st[h]
            m_t = jnp.maximum(m_loc[h], m_inter)
            tot = jnp.exp(m_inter - m_t) * inter[h] + jnp.exp(m_loc[h] - m_t) * loc[h]
            hh = tot[:, :D_HEAD] / jnp.maximum(jnp.abs(tot[:, D_HEAD:]), jnp.exp(-m_t))
            m_new = jnp.maximum(b_last[h] + m_st[h], m_chunk[h])
            state_s[h] = (jnp.exp(b_last[h] + m_st[h] - m_new) * st[h]
                          + jnp.exp(m_chunk[h] - m_new) * d_state[h])
            m_s[h] = m_new
            hh = hh * lax.rsqrt(jnp.mean(hh * hh, axis=-1, keepdims=True) + NORM_EPS) * gn_ref[:, hsl[h]]
            o_ref[0, rows, hsl[h]] = (hh * _sigmoid(og_ref[0, rows, hsl[h]])).astype(o_ref.dtype)
        return carry

    lax.fori_loop(0, n_chunks, chunk_body, 0)


def _mlstm(big3, small3, conv_w, prm, gnorm, cs):
    b, s, _ = big3.shape
    return pl.pallas_call(
        functools.partial(_mlstm_kernel, cs=cs),
        grid=(b, s // cs),
        in_specs=[pl.BlockSpec((1, cs, 2 * D_QK), lambda i, j: (i, j, 2)),
                  pl.BlockSpec((1, cs, D_QK), lambda i, j: (i, j, 6)),
                  pl.BlockSpec((1, cs, D_QK), lambda i, j: (i, j, 7)),
                  pl.BlockSpec((1, cs, LANES), lambda i, j: (i, j, 0)),
                  pl.BlockSpec((CONV_WIDTH, 2 * D_QK), lambda i, j: (0, 0)),
                  pl.BlockSpec((SUBLANES, LANES), lambda i, j: (0, 0)),
                  pl.BlockSpec((1, D_QK), lambda i, j: (0, 0))],
        out_specs=pl.BlockSpec((1, cs, D_QK), lambda i, j: (i, j, 0)),
        out_shape=jax.ShapeDtypeStruct((b, s, D_QK), BF16),
        scratch_shapes=[pltpu.VMEM((cs + CONV_CARRY, 2 * D_QK), F32),
                        pltpu.VMEM((cs, D_QK), F32), pltpu.VMEM((cs, D_QK), F32),
                        pltpu.VMEM((cs, LANES), F32), pltpu.VMEM((cs, LANES), F32),
                        pltpu.VMEM((N_HEADS, D_HEAD, 2 * D_HEAD), F32),
                        pltpu.VMEM((N_HEADS, 1, 1), F32)],
        compiler_params=pltpu.CompilerParams(dimension_semantics=("arbitrary", "arbitrary"),
                                             vmem_limit_bytes=VMEM_LIMIT),
        name="mlstm",
    )(big3, big3, big3, small3, conv_w, prm, gnorm)


N_SEG = D_MODEL // LANES
assert N_SEG == SUBLANES


def _store_token_tiles(ref, val, n):
    for s in range(N_SEG):
        ref[pl.ds(s, n, stride=N_SEG), :] = val[:, s * LANES:(s + 1) * LANES]


def _load_token_tiles(ref, n):
    return jnp.concatenate([ref[pl.ds(s, n, stride=N_SEG), :] for s in range(N_SEG)], axis=1)


def _merge_kernel(oa_ref, ob_ref, gates_ref, x_ref, wa_ref, wb_ref, wo_ref, g_ref, wr_ref, br_ref,
                  x1_ref, h2_ref, key_ref, gate_ref, cnt_ref, carry_s, *, tm):
    @pl.when(pl.program_id(0) == 0)
    def _():
        carry_s[...] = jnp.zeros(carry_s.shape, F32)

    y_a = jnp.dot(oa_ref[...], wa_ref[...], preferred_element_type=F32)
    y_b = jnp.dot(ob_ref[...], wb_ref[...], preferred_element_type=F32)
    y = _sigmoid(gates_ref[:, :D_MODEL]) * y_a + _sigmoid(gates_ref[:, D_MODEL:]) * y_b
    x1 = x_ref[...] + jnp.dot(y.astype(BF16), wo_ref[...], preferred_element_type=F32)
    x1_ref[...] = x1
    h2 = _rms(x1, g_ref[...])
    _store_token_tiles(h2_ref, h2, tm)
    logits = jnp.dot(h2.astype(BF16), wr_ref[...], preferred_element_type=F32) + br_ref[...]

    lane = lax.broadcasted_iota(I32, (tm, LANES), 1).astype(F32)
    vals, sels = [], []
    idx_t = jnp.zeros((tm, LANES), F32)
    work = logits
    for k in range(TOP_K):
        m = jnp.max(work, axis=-1, keepdims=True)
        i = jnp.min(jnp.where(work == m, lane, float(LANES)), axis=-1, keepdims=True)
        sel = lane == i
        work = jnp.where(sel, -jnp.inf, work)
        idx_t = jnp.where(lane == float(k), i, idx_t)
        vals.append(m)
        sels.append(sel)
    es = [jnp.exp(v - vals[0]) for v in vals]
    denom = es[0] + es[1] + es[2] + es[3]
    gate_t = jnp.zeros((tm, LANES), F32)
    for k in range(TOP_K):
        gate_t = jnp.where(lane == float(k), es[k] / denom, gate_t)
    gate_ref[...] = gate_t[:, :TOP_K]

    onehot = jnp.zeros((tm, LANES), F32)
    for sel in sels:
        onehot = onehot + jnp.where(sel, 1.0, 0.0)
    r = lax.broadcasted_iota(I32, (tm, tm), 0)
    c = lax.broadcasted_iota(I32, (tm, tm), 1)
    before = jnp.where(r > c, 1.0, 0.0).astype(BF16)
    cum = jnp.dot(before, onehot.astype(BF16), preferred_element_type=F32) + carry_s[...]
    rank_t = jnp.zeros((tm, LANES), F32)
    for k, sel in enumerate(sels):
        rk = jnp.sum(jnp.where(sel, cum, 0.0), axis=-1, keepdims=True)
        rank_t = jnp.where(lane == float(k), rk, rank_t)
    key_ref[...] = (idx_t * float(RANK_SPAN) + rank_t)[:, :TOP_K].astype(I32)
    total = carry_s[...] + jnp.sum(onehot, axis=0, keepdims=True)
    carry_s[...] = total
    cnt_ref[...] = total.astype(I32)


def _merge(oa, ob, big, x2, wa, wb, wo, g, wr, br, tm):
    t = x2.shape[0]
    const = lambda i: (0, 0)
    return pl.pallas_call(
        functools.partial(_merge_kernel, tm=tm),
        grid=(t // tm,),
        in_specs=[pl.BlockSpec((tm, D_QK), lambda i: (i, 0)),
                  pl.BlockSpec((tm, D_QK), lambda i: (i, 0)),
                  pl.BlockSpec((tm, 2 * D_MODEL), lambda i: (i, 2)),
                  pl.BlockSpec((tm, D_MODEL), lambda i: (i, 0)),
                  pl.BlockSpec((D_QK, D_MODEL), const),
                  pl.BlockSpec((D_QK, D_MODEL), const),
                  pl.BlockSpec((D_MODEL, D_MODEL), const),
                  pl.BlockSpec((1, D_MODEL), const),
                  pl.BlockSpec((D_MODEL, LANES), const),
                  pl.BlockSpec((1, LANES), const)],
        out_specs=[pl.BlockSpec((tm, D_MODEL), lambda i: (i, 0)),
                   pl.BlockSpec((tm * N_SEG, LANES), lambda i: (i, 0)),
                   pl.BlockSpec((tm, TOP_K), lambda i: (i, 0)),
                   pl.BlockSpec((tm, TOP_K), lambda i: (i, 0)),
                   pl.BlockSpec((1, LANES), const)],
        out_shape=[jax.ShapeDtypeStruct((t, D_MODEL), F32),
                   jax.ShapeDtypeStruct((t * N_SEG, LANES), F32),
                   jax.ShapeDtypeStruct((t, TOP_K), I32),
                   jax.ShapeDtypeStruct((t, TOP_K), F32),
                   jax.ShapeDtypeStruct((1, LANES), I32)],
        scratch_shapes=[pltpu.VMEM((1, LANES), F32)],
        compiler_params=pltpu.CompilerParams(dimension_semantics=("arbitrary",),
                                             vmem_limit_bytes=VMEM_LIMIT),
        name="merge_router",
    )(oa, ob, big, x2, wa, wb, wo, g, wr, br)


def _expert_kernel(te_ref, nu_ref, nv_ref, tok_ref, tok_next_ref, slot_ref, slot_prev_ref, h2_hbm,
                   wgu_ref, wd_ref, perm_ref, bg_ref, bu_ref, bd_ref, y_hbm,
                   xbuf0, xbuf1, ybuf0, ybuf1, wg_s, wu_s, wd_s, in_sem, out_sem, *, bm):
    i = pl.program_id(0)
    n_used = nu_ref[0]
    new_expert = (i == 0) | (te_ref[i] != te_ref[jnp.maximum(i - 1, 0)])
    xbufs = (xbuf0, xbuf1)
    ybufs = (ybuf0, ybuf1)

    def token_tile(ref, r):
        return ref.at[pl.ds(pl.multiple_of(r * N_SEG, N_SEG), N_SEG)]

    def gather_row(ids_ref, s, r):
        pltpu.make_async_copy(token_tile(h2_hbm, ids_ref[r]), token_tile(xbufs[s], r),
                              in_sem.at[s]).start()

    def wait_gather(s):
        pltpu.make_async_copy(h2_hbm.at[pl.ds(0, bm * N_SEG)], xbufs[s], in_sem.at[s]).wait()

    def wait_scatter(s, n_rows):
        n = pl.multiple_of(n_rows * N_SEG, N_SEG)
        pltpu.make_async_copy(ybufs[s].at[pl.ds(0, n)], y_hbm.at[pl.ds(0, n)], out_sem.at[s]).wait()

    @pl.when(i == 0)
    def _():
        def body(r, carry):
            gather_row(tok_ref, 0, r)
            return carry
        lax.fori_loop(0, bm, body, 0, unroll=DMA_UNROLL)

    n_prev = jnp.where(i >= 1, nv_ref[jnp.maximum(i - 1, 0)], 0)

    def scatter_row(ids_ref, s, r):
        pltpu.make_async_copy(token_tile(ybufs[s], r), token_tile(y_hbm, ids_ref[r]),
                              out_sem.at[s]).start()

    def scatter_loop(ids_ref, s, n_rows):
        def one(r, carry):
            scatter_row(ids_ref, s, r)
            return carry

        def group(j, carry):
            for q in range(DMA_UNROLL):
                scatter_row(ids_ref, s, j * DMA_UNROLL + q)
            return carry
        n_groups = n_rows // DMA_UNROLL
        lax.fori_loop(0, n_groups, group, 0)
        lax.fori_loop(n_groups * DMA_UNROLL, n_rows, one, 0)

    def tile_body(cur, prev_full):
        nxt = 1 - cur
        wait_gather(cur)

        @pl.when(i >= 2)
        def _():
            wait_scatter(cur, nv_ref[i - 2])

        if not prev_full:
            scatter_loop(slot_prev_ref, nxt, n_prev)

        @pl.when(new_expert)
        def _():
            half = PERM_BLOCK // 2
            for j in range(2 * D_EXPERT // PERM_BLOCK):
                blk = wgu_ref[0, :, j * PERM_BLOCK:(j + 1) * PERM_BLOCK].astype(BF16)
                split = jnp.dot(blk, perm_ref[...], preferred_element_type=F32)
                wg_s[:, j * half:(j + 1) * half] = split[:, :half].astype(BF16)
                wu_s[:, j * half:(j + 1) * half] = split[:, half:].astype(BF16)
            wd_s[...] = wd_ref[0].astype(BF16)

        n_grp = bm // DMA_GROUPS

        def dma_group(gi):
            for r in range(gi * n_grp, (gi + 1) * n_grp):
                gather_row(tok_next_ref, nxt, r)
                if prev_full:
                    scatter_row(slot_prev_ref, nxt, r)

        dma_group(0)
        x = _load_token_tiles(xbufs[cur], bm).astype(BF16)
        g = jnp.dot(x, wg_s[...], preferred_element_type=F32) + bg_ref[0]
        dma_group(1)
        u = jnp.dot(x, wu_s[...], preferred_element_type=F32) + bu_ref[0]
        dma_group(2)
        gate = jnp.minimum(g, SWIGLU_LIMIT)
        up = jnp.clip(u, -SWIGLU_LIMIT, SWIGLU_LIMIT)
        act = gate * _sigmoid(SWIGLU_ALPHA * gate) * (up + 1.0)
        dma_group(3)
        y = jnp.dot(act.astype(BF16), wd_s[...], preferred_element_type=F32) + bd_ref[0]
        _store_token_tiles(ybufs[cur], y, bm)

        @pl.when(i == n_used - 1)
        def _():
            scatter_loop(slot_ref, cur, nv_ref[i])
            wait_gather(nxt)
            wait_scatter(cur, nv_ref[i])

            @pl.when(i >= 1)
            def _():
                wait_scatter(nxt, nv_ref[i - 1])

    for parity in range(2):
        for prev_full in (False, True):
            full = (n_prev == bm) if prev_full else (n_prev != bm)
            cond = (i < n_used) & (lax.rem(i, 2) == parity) & full
            pl.when(cond)(functools.partial(tile_body, parity, prev_full))


def _split_permutation():
    half = PERM_BLOCK // 2
    src = jnp.arange(PERM_BLOCK)[:, None]
    dst = jnp.arange(PERM_BLOCK)[None, :]
    return (src == jnp.where(dst < half, 2 * dst, 2 * (dst - half) + 1)).astype(BF16)


def _experts(tile_expert, n_used, tile_valid, row_tok, row_slot, h2, wgu, wd, bg, bu, bd, bm):
    n_rows = row_tok.shape[0]
    n_tiles = n_rows // bm
    n_slots = h2.shape[0] // N_SEG * TOP_K
    cur = lambda i, te, nu, nv: (jnp.minimum(i, nu[0] - 1),)
    nxt = lambda i, te, nu, nv: (jnp.minimum(i + 1, nu[0] - 1),)
    prv = lambda i, te, nu, nv: (jnp.clip(i - 1, 0, nu[0] - 1),)
    w_map = lambda i, te, nu, nv: (te[i], 0, 0)
    grid_spec = pltpu.PrefetchScalarGridSpec(
        num_scalar_prefetch=3,
        grid=(n_tiles,),
        in_specs=[pl.BlockSpec((bm,), cur, memory_space=pltpu.SMEM),
                  pl.BlockSpec((bm,), nxt, memory_space=pltpu.SMEM),
                  pl.BlockSpec((bm,), cur, memory_space=pltpu.SMEM),
                  pl.BlockSpec((bm,), prv, memory_space=pltpu.SMEM),
                  pl.BlockSpec(memory_space=pl.ANY),
                  pl.BlockSpec((1, D_MODEL, 2 * D_EXPERT), w_map),
                  pl.BlockSpec((1, D_EXPERT, D_MODEL), w_map),
                  pl.BlockSpec((PERM_BLOCK, PERM_BLOCK), lambda i, te, nu, nv: (0, 0)),
                  pl.BlockSpec((1, 1, D_EXPERT), w_map),
                  pl.BlockSpec((1, 1, D_EXPERT), w_map),
                  pl.BlockSpec((1, 1, D_MODEL), w_map)],
        out_specs=pl.BlockSpec(memory_space=pl.ANY),
        scratch_shapes=[pltpu.VMEM((bm * N_SEG, LANES), F32), pltpu.VMEM((bm * N_SEG, LANES), F32),
                        pltpu.VMEM((bm * N_SEG, LANES), F32), pltpu.VMEM((bm * N_SEG, LANES), F32),
                        pltpu.VMEM((D_MODEL, D_EXPERT), BF16), pltpu.VMEM((D_MODEL, D_EXPERT), BF16),
                        pltpu.VMEM((D_EXPERT, D_MODEL), BF16),
                        pltpu.SemaphoreType.DMA((2,)), pltpu.SemaphoreType.DMA((2,))],
    )
    return pl.pallas_call(
        functools.partial(_expert_kernel, bm=bm),
        grid_spec=grid_spec,
        out_shape=jax.ShapeDtypeStruct((n_slots * N_SEG, LANES), F32),
        compiler_params=pltpu.CompilerParams(dimension_semantics=("arbitrary",),
                                             vmem_limit_bytes=VMEM_LIMIT,
                                             has_side_effects=True),
        name="experts",
    )(tile_expert, n_used, tile_valid, row_tok, row_tok, row_slot, row_slot, h2, wgu, wd, _split_permutation(),
      bg, bu, bd)


def _combine_kernel(y0_ref, y1_ref, y2_ref, y3_ref, gate_ref, x1_ref, g_ref, o_ref, *, tm):
    acc = x1_ref[...]
    for k, y_ref in enumerate((y0_ref, y1_ref, y2_ref, y3_ref)):
        acc = acc + gate_ref[:, k:k + 1] * _load_token_tiles(y_ref, tm)
    o_ref[...] = _rms(acc, g_ref[...])


def _combine(y_slots, gates, x1, g, tm):
    t = x1.shape[0]
    nb = t // tm

    def slot_spec(k):
        return pl.BlockSpec((tm * N_SEG, LANES), lambda i: (k * nb + i, 0))

    return pl.pallas_call(
        functools.partial(_combine_kernel, tm=tm),
        grid=(nb,),
        in_specs=[slot_spec(0), slot_spec(1), slot_spec(2), slot_spec(3),
                  pl.BlockSpec((tm, TOP_K), lambda i: (i, 0)),
                  pl.BlockSpec((tm, D_MODEL), lambda i: (i, 0)),
                  pl.BlockSpec((1, D_MODEL), lambda i: (0, 0))],
        out_specs=pl.BlockSpec((tm, D_MODEL), lambda i: (i, 0)),
        out_shape=jax.ShapeDtypeStruct((t, D_MODEL), F32),
        compiler_params=pltpu.CompilerParams(dimension_semantics=("arbitrary",),
                                             vmem_limit_bytes=VMEM_LIMIT),
        name="combine",
    )(y_slots, y_slots, y_slots, y_slots, gates, x1, g)


def _lane_row(vec, lane0):
    n = vec.shape[0]
    return jnp.zeros((SUBLANES, LANES), F32).at[0, lane0:lane0 + n].set(vec.astype(F32))


def _tile_size(n, pref):
    return pref if n % pref == 0 else n


def kernel(x, norm_mix, w_in, gdn_conv, gdn_a_log, gdn_dt_bias, gdn_norm, ml_conv, ml_b_i, ml_b_f,
           ml_norm, w_up_gdn, w_up_ml, w_out, norm_ffn, w_router, b_router, w_gate_up, b_gate_up,
           w_down, b_down, norm_final):
    assert norm_mix.shape[0] == 1, "single-layer stack"
    b, s, d = x.shape
    assert d == D_MODEL and s % CHUNK == 0
    t = b * s
    x2 = x.reshape(t, d)

    w = w_in[0]
    w_big = jnp.concatenate([_cols(w, n) for n in ("g_q", "g_k", "g_v", "g_z", "m_q", "m_k", "m_v",
                                                   "m_o", "gate_gdn", "gate_ml")], axis=1).astype(BF16)
    w_small = jnp.concatenate([_cols(w, n) for n in ("g_a", "g_b", "m_i", "m_f")], axis=1)
    w_small = jnp.pad(w_small, ((0, 0), (0, LANES - N_SMALL))).astype(BF16)

    tm = _tile_size(t, 512)
    big, small = _in_proj(x2, norm_mix[0][None, :], w_big, w_small, tm)
    big3 = big.reshape(b, s, N_BIG)
    small3 = small.reshape(b, s, LANES)

    cs = _tile_size(s, 512)
    gdn_prm = _lane_row(gdn_a_log[0], 0).at[1, 0:N_HEADS].set(gdn_dt_bias[0].astype(F32))
    oa = _gdn(big3, small3, gdn_conv[0].astype(F32), gdn_prm, gdn_norm[0][None, :].astype(F32), cs)
    ml_prm = _lane_row(ml_b_i[0], 2 * N_HEADS).at[0, 3 * N_HEADS:4 * N_HEADS].set(ml_b_f[0].astype(F32))
    ob = _mlstm(big3, small3, ml_conv[0].astype(F32), ml_prm, ml_norm[0][None, :].astype(F32), cs)

    w_r = jnp.pad(w_router[0], ((0, 0), (0, LANES - N_EXPERTS))).astype(BF16)
    b_r = jnp.full((1, LANES), PAD_LOGIT, F32).at[0, :N_EXPERTS].set(b_router[0].astype(F32))
    x1, h2, key, gates, counts = _merge(
        oa.reshape(t, D_QK), ob.reshape(t, D_QK), big, x2, w_up_gdn[0].astype(BF16),
        w_up_ml[0].astype(BF16), w_out[0].astype(BF16), norm_ffn[0][None, :], w_r, b_r, tm)

    bm = EXPERT_TILE
    n_assign = t * TOP_K
    n_tiles = -(-n_assign // bm) + N_EXPERTS
    counts = counts[0, :N_EXPERTS]
    padded = (counts + bm - 1) // bm * bm
    pend = jnp.cumsum(padded)
    pstart = pend - padded
    n_used = (pend[-1] // bm).astype(I32)
    tile_ids = jnp.minimum(jnp.arange(n_tiles, dtype=I32), n_used - 1)
    tile_expert = jnp.minimum(jnp.sum((pend[None, :] <= (tile_ids * bm)[:, None]).astype(I32), axis=1),
                              N_EXPERTS - 1)
    tile_valid = jnp.clip(counts[tile_expert] - (tile_ids * bm - pstart[tile_expert]), 0, bm).astype(I32)
    assert t <= RANK_SPAN and bm <= RANK_SPAN
    assign = jnp.arange(n_assign, dtype=I32)
    pad_e = jnp.arange(N_EXPERTS, dtype=I32)[:, None]
    pad_p = jnp.arange(bm, dtype=I32)[None, :]
    pad_key = jnp.where(pad_p < (padded - counts)[:, None],
                        pad_e * RANK_SPAN + counts[:, None] + pad_p,
                        N_EXPERTS * RANK_SPAN + pad_e * bm + pad_p)
    _, row_slot = lax.sort((jnp.concatenate([key.reshape(-1), pad_key.reshape(-1)]),
                            jnp.concatenate([(assign % TOP_K) * t + assign // TOP_K,
                                             jnp.full((N_EXPERTS * bm,), -1, I32)])),
                           num_keys=1)
    row_tok = jnp.where(row_slot < 0, jnp.arange(n_tiles * bm, dtype=I32), row_slot) % t

    bgu = b_gate_up[0]
    y_slots = _experts(tile_expert.astype(I32), n_used.reshape(1), tile_valid, row_tok, row_slot, h2,
                       w_gate_up[0], w_down[0],
                       bgu[:, None, 0::2].astype(F32), bgu[:, None, 1::2].astype(F32),
                       b_down[0][:, None, :].astype(F32), bm)
    out = _combine(y_slots, gates, x1, norm_final[None, :], tm)
    return out.reshape(b, s, d)
```

```python
import functools

import jax
import jax.numpy as jnp
from jax import lax
from jax.experimental import pallas as pl
from jax.experimental.pallas import tpu as pltpu

F32 = jnp.float32
BF16 = jnp.bfloat16
I32 = jnp.int32

D_MODEL = 1024
N_HEADS = 4
D_HEAD = 128
D_QK = N_HEADS * D_HEAD
CONV_WIDTH = 4
CHUNK = 64
GATE_SOFTCAP = 15.0
N_EXPERTS = 32
TOP_K = 4
D_EXPERT = 1024
SWIGLU_LIMIT = 7.0
SWIGLU_ALPHA = 1.702
NORM_EPS = 1e-6

LANES = 128
SUBLANES = 8
CONV_CARRY = SUBLANES
PAD_LOGIT = -1e30
N_SMALL = 16
N_BIG = 6 * D_MODEL
VMEM_LIMIT = 56 * 1024 * 1024
EXPERT_TILE = 512
DMA_UNROLL = 8
DMA_GROUPS = 4
RANK_SPAN = 1 << 18
PERM_BLOCK = 256
LOCAL_CHUNKS = 8

_OFF = {}
_o = 0
for _name, _w in (("g_q", D_QK), ("g_k", D_QK), ("g_v", D_QK), ("g_z", D_QK), ("g_a", N_HEADS),
                  ("g_b", N_HEADS), ("m_q", D_QK), ("m_k", D_QK), ("m_v", D_QK), ("m_o", D_QK),
                  ("m_i", N_HEADS), ("m_f", N_HEADS), ("gate_gdn", D_MODEL), ("gate_ml", D_MODEL)):
    _OFF[_name] = (_o, _w)
    _o += _w


def _cols(w, name):
    o, n = _OFF[name]
    return w[:, o:o + n]


_NN = (((1,), (0,)), ((), ()))
_NT = (((1,), (1,)), ((), ()))
_TN = (((0,), (0,)), ((), ()))


def _bdot(a, b, dims=_NN):
    return lax.dot_general(a.astype(BF16), b.astype(BF16), dims, preferred_element_type=F32)


def _split3(a):
    hi = a.astype(BF16)
    r = a - hi.astype(F32)
    mid = r.astype(BF16)
    return hi, mid, (r - mid.astype(F32)).astype(BF16)


def _select_dot(sel01, x, dims=_NN):
    s = sel01.astype(BF16)
    h, m, l = _split3(x)
    d = functools.partial(lax.dot_general, dimension_numbers=dims, preferred_element_type=F32)
    return d(s, h) + (d(s, m) + d(s, l))


def _sigmoid(x):
    return 1.0 / (1.0 + jnp.exp(-x))


def _silu(x):
    return x * _sigmoid(x)


def _softplus(x):
    return jnp.maximum(x, 0.0) + jnp.log(1.0 + jnp.exp(-jnp.abs(x)))


def _rms(x, g):
    return x * lax.rsqrt(jnp.mean(x * x, axis=-1, keepdims=True) + NORM_EPS) * g


def _in_proj_kernel(x_ref, g_ref, wbig_ref, wsmall_ref, big_ref, small_ref, *, n_chunk):
    h = _rms(x_ref[...], g_ref[...]).astype(BF16)
    for c in range(N_BIG // n_chunk):
        sl = slice(c * n_chunk, (c + 1) * n_chunk)
        big_ref[:, sl] = jnp.dot(h, wbig_ref[:, sl], preferred_element_type=F32)
    small_ref[...] = jnp.dot(h, wsmall_ref[...], preferred_element_type=F32)


def _in_proj(x2, g, w_big, w_small, tm):
    t = x2.shape[0]
    return pl.pallas_call(
        functools.partial(_in_proj_kernel, n_chunk=1024),
        grid=(t // tm,),
        in_specs=[pl.BlockSpec((tm, D_MODEL), lambda i: (i, 0)),
                  pl.BlockSpec((1, D_MODEL), lambda i: (0, 0)),
                  pl.BlockSpec((D_MODEL, N_BIG), lambda i: (0, 0)),
                  pl.BlockSpec((D_MODEL, LANES), lambda i: (0, 0))],
        out_specs=[pl.BlockSpec((tm, N_BIG), lambda i: (i, 0)),
                   pl.BlockSpec((tm, LANES), lambda i: (i, 0))],
        out_shape=[jax.ShapeDtypeStruct((t, N_BIG), F32),
                   jax.ShapeDtypeStruct((t, LANES), F32)],
        compiler_params=pltpu.CompilerParams(dimension_semantics=("arbitrary",),
                                             vmem_limit_bytes=VMEM_LIMIT),
        name="in_proj",
    )(x2, g, w_big, w_small)


def _chunk_masks():
    r = lax.broadcasted_iota(I32, (CHUNK, CHUNK), 0)
    c = lax.broadcasted_iota(I32, (CHUNK, CHUNK), 1)
    return r > c, r >= c


def _block_cumsum_matrix(cs):
    r = lax.broadcasted_iota(I32, (cs, cs), 0)
    c = lax.broadcasted_iota(I32, (cs, cs), 1)
    same = (r // CHUNK) == (c // CHUNK)
    return jnp.where(same & (r >= c), 1.0, 0.0).astype(F32)


def _lane_onehot(lane):
    return jnp.where(lax.broadcasted_iota(I32, (CHUNK, LANES), 1) == lane, 1.0, 0.0).astype(F32)


def _conv_silu_pieces(x_ref, w_ref, buf_ref, cs, width):
    assert CONV_WIDTH - 1 <= CONV_CARRY

    @pl.when(pl.program_id(1) == 0)
    def _():
        buf_ref[0:CONV_CARRY, :] = jnp.zeros((CONV_CARRY, buf_ref.shape[1]), F32)

    buf_ref[CONV_CARRY:CONV_CARRY + cs, :] = x_ref[0]
    for r0 in range(0, cs, CHUNK):
        for c0 in range(0, buf_ref.shape[1], width):
            cols = slice(c0, c0 + width)
            top = CONV_CARRY + r0
            acc = w_ref[CONV_WIDTH - 1:CONV_WIDTH, cols] * buf_ref[top:top + CHUNK, cols]
            for j in range(CONV_WIDTH - 1):
                s = top - (CONV_WIDTH - 1) + j
                acc = acc + w_ref[j:j + 1, cols] * buf_ref[s:s + CHUNK, cols]
            yield r0, c0, _silu(acc)
    buf_ref[0:CONV_CARRY, :] = buf_ref[cs:cs + CONV_CARRY, :]


def _gdn_kernel(qkv_ref, z_ref, sm_ref, conv_ref, prm_ref, gn_ref, o_ref,
                buf_ref, q_s, k_s, v_s, w_s, attn_s, gam_s, beta_s, state_s, *, cs):
    n_chunks = cs // CHUNK

    @pl.when(pl.program_id(1) == 0)
    def _():
        state_s[...] = jnp.zeros(state_s.shape, F32)

    for r0, c0, piece in _conv_silu_pieces(qkv_ref, conv_ref, buf_ref, cs, D_QK):
        rows = slice(r0, r0 + CHUNK)
        if c0 == 2 * D_QK:
            v_s[rows, :] = piece
            continue
        dst, scale = (q_s, D_HEAD ** -0.5) if c0 == 0 else (k_s, 1.0)
        for h in range(N_HEADS):
            sl = slice(h * D_HEAD, (h + 1) * D_HEAD)
            a = piece[:, sl]
            dst[rows, sl] = a * (lax.rsqrt(jnp.sum(a * a, axis=-1, keepdims=True) + NORM_EPS) * scale)

    sm = sm_ref[0]
    logdec = -jnp.exp(prm_ref[0:1, :]) * _softplus(sm + prm_ref[1:2, :])
    gam_s[...] = _select_dot(_block_cumsum_matrix(cs), logdec)
    beta_s[...] = _sigmoid(sm)

    strict, incl = _chunk_masks()
    eye = jnp.where(lax.broadcasted_iota(I32, (CHUNK, CHUNK), 0)
                    == lax.broadcasted_iota(I32, (CHUNK, CHUNK), 1), 1.0, 0.0).astype(F32)
    gnorm = gn_ref[...]

    heads = range(N_HEADS)
    hsl = [slice(h * D_HEAD, (h + 1) * D_HEAD) for h in heads]
    asl = [slice(h * CHUNK, (h + 1) * CHUNK) for h in heads]
    onehots = [_lane_onehot(h) for h in heads]

    def local_body(it, carry):
        rows = [pl.ds(pl.multiple_of((it * LOCAL_CHUNKS + ci) * CHUNK, CHUNK), CHUNK)
                for ci in range(LOCAL_CHUNKS)]
        probs = [(ci, h) for ci in range(LOCAL_CHUNKS) for h in heads]
        gam_all = [gam_s[r, :] for r in rows]
        beta_all = [beta_s[r, :] for r in rows]
        q = [q_s[rows[ci], hsl[h]] for ci, h in probs]
        k = [k_s[rows[ci], hsl[h]] for ci, h in probs]
        v = [v_s[rows[ci], hsl[h]] for ci, h in probs]
        gc = [gam_all[ci][:, h:h + 1] for ci, h in probs]
        beta = [beta_all[ci][:, N_HEADS + h:N_HEADS + h + 1] for ci, h in probs]
        g_row = [_select_dot(onehots[h], gam_all[ci], _NT) for ci, h in probs]
        kb = [a * b for a, b in zip(k, beta)]
        kk = [_bdot(a, b, _NT) for a, b in zip(kb, k)]
        qk = [_bdot(a, b, _NT) for a, b in zip(q, k)]
        decay = [jnp.where(incl, jnp.exp(jnp.minimum(a - b, 0.0)), 0.0) for a, b in zip(gc, g_row)]
        lower = [jnp.where(strict, a * b, 0.0) for a, b in zip(kk, decay)]
        inv = [eye - a for a in lower]
        pw = [_bdot(a, a) for a in lower]
        for lvl in range(5):
            upd = [_bdot(a, b) for a, b in zip(inv, pw)]
            if lvl < 4:
                pw = [_bdot(a, a) for a in pw]
            inv = [a + b for a, b in zip(inv, upd)]
        e_gc = [jnp.exp(a) for a in gc]
        rhs = [jnp.concatenate([a * b, c * d], axis=1) for a, b, c, d in zip(v, beta, kb, e_gc)]
        sol = [_bdot(a, b) for a, b in zip(inv, rhs)]
        for p, (ci, h) in enumerate(probs):
            v_s[rows[ci], hsl[h]] = sol[p][:, :D_HEAD]
            w_s[rows[ci], hsl[h]] = sol[p][:, D_HEAD:]
            attn_s[rows[ci], asl[h]] = qk[p] * decay[p]
            q_s[rows[ci], hsl[h]] = q[p] * e_gc[p]
            k_s[rows[ci], hsl[h]] = k[p] * jnp.exp(gc[p][CHUNK - 1:CHUNK, :] - gc[p])
        return carry

    lax.fori_loop(0, n_chunks // LOCAL_CHUNKS, local_body, 0)

    def state_body(c, carry):
        rows = pl.ds(pl.multiple_of(c * CHUNK, CHUNK), CHUNK)
        g_last = gam_s[pl.ds(c * CHUNK + CHUNK - 1, 1), :]
        st = [state_s[h] for h in heads]
        ws = [_bdot(w_s[rows, hsl[h]], st[h]) for h in heads]
        qs = [_bdot(q_s[rows, hsl[h]], st[h]) for h in heads]
        u = [v_s[rows, hsl[h]] - ws[h] for h in heads]
        au = [_bdot(attn_s[rows, asl[h]], u[h]) for h in heads]
        ku = [_bdot(k_s[rows, hsl[h]], u[h], _TN) for h in heads]
        for h in heads:
            state_s[h] = st[h] * jnp.exp(g_last[:, h:h + 1]) + ku[h]
            o = qs[h] + au[h]
            o = o * lax.rsqrt(jnp.mean(o * o, axis=-1, keepdims=True) + NORM_EPS) * gnorm
            o_ref[0, rows, hsl[h]] = (o * _silu(z_ref[0, rows, hsl[h]])).astype(o_ref.dtype)
        return carry

    lax.fori_loop(0, n_chunks, state_body, 0, unroll=2)


def _gdn(big3, small3, conv_w, prm, gnorm, cs):
    b, s, _ = big3.shape
    return pl.pallas_call(
        functools.partial(_gdn_kernel, cs=cs),
        grid=(b, s // cs),
        in_specs=[pl.BlockSpec((1, cs, 3 * D_QK), lambda i, j: (i, j, 0)),
                  pl.BlockSpec((1, cs, D_QK), lambda i, j: (i, j, 3)),
                  pl.BlockSpec((1, cs, LANES), lambda i, j: (i, j, 0)),
                  pl.BlockSpec((CONV_WIDTH, 3 * D_QK), lambda i, j: (0, 0)),
                  pl.BlockSpec((SUBLANES, LANES), lambda i, j: (0, 0)),
                  pl.BlockSpec((1, D_HEAD), lambda i, j: (0, 0))],
        out_specs=pl.BlockSpec((1, cs, D_QK), lambda i, j: (i, j, 0)),
        out_shape=jax.ShapeDtypeStruct((b, s, D_QK), BF16),
        scratch_shapes=[pltpu.VMEM((cs + CONV_CARRY, 3 * D_QK), F32),
                        pltpu.VMEM((cs, D_QK), F32), pltpu.VMEM((cs, D_QK), F32),
                        pltpu.VMEM((cs, D_QK), F32), pltpu.VMEM((cs, D_QK), F32),
                        pltpu.VMEM((cs, N_HEADS * CHUNK), F32),
                        pltpu.VMEM((cs, LANES), F32), pltpu.VMEM((cs, LANES), F32),
                        pltpu.VMEM((N_HEADS, D_HEAD, D_HEAD), F32)],
        compiler_params=pltpu.CompilerParams(dimension_semantics=("arbitrary", "arbitrary"),
                                             vmem_limit_bytes=VMEM_LIMIT),
        name="gdn",
    )(big3, big3, small3, conv_w, prm, gnorm)


def _mlstm_kernel(qk_ref, v_ref, og_ref, sm_ref, conv_ref, prm_ref, gn_ref, o_ref,
                  buf_ref, q_s, k_s, bcum_s, ipre_s, state_s, m_s, *, cs):
    n_chunks = cs // CHUNK

    @pl.when(pl.program_id(1) == 0)
    def _():
        state_s[...] = jnp.zeros(state_s.shape, F32)
        m_s[...] = jnp.zeros(m_s.shape, F32)

    for r0, c0, piece in _conv_silu_pieces(qk_ref, conv_ref, buf_ref, cs, D_QK):
        if c0 == 0:
            q_s[r0:r0 + CHUNK, :] = piece
        else:
            k_s[r0:r0 + CHUNK, :] = piece * (D_HEAD ** -0.5)

    pre = sm_ref[0] + prm_ref[0:1, :]
    capped = GATE_SOFTCAP * jnp.tanh(pre / GATE_SOFTCAP)
    logf = -_softplus(-capped)
    bcum_s[...] = _select_dot(_block_cumsum_matrix(cs), logf)
    ipre_s[...] = pltpu.roll(capped, N_HEADS, axis=1)

    _, incl = _chunk_masks()
    ones_aug = jnp.ones((CHUNK, D_HEAD), F32)

    heads = range(N_HEADS)
    hsl = [slice(h * D_HEAD, (h + 1) * D_HEAD) for h in heads]
    lanes = [3 * N_HEADS + h for h in heads]
    onehots = [_lane_onehot(lane) for lane in lanes]

    def chunk_body(c, carry):
        rows = pl.ds(pl.multiple_of(c * CHUNK, CHUNK), CHUNK)
        b_all = bcum_s[rows, :]
        comb_all = ipre_s[rows, :] - b_all
        q = [q_s[rows, hsl[h]] for h in heads]
        k = [k_s[rows, hsl[h]] for h in heads]
        v_aug = [jnp.concatenate([v_ref[0, rows, hsl[h]], ones_aug], axis=1) for h in heads]
        st = [state_s[h] for h in heads]
        m_st = [m_s[h] for h in heads]
        bc = [b_all[:, lane:lane + 1] for lane in lanes]
        comb_row = [_select_dot(onehots[h], comb_all, _NT) for h in heads]
        qk = [_bdot(q[h], k[h], _NT) for h in heads]
        inter = [_bdot(q[h], st[h]) for h in heads]
        b_last = [bc[h][CHUNK - 1:CHUNK, :] for h in heads]
        a_log = [b_last[h] + comb_all[:, lanes[h]:lanes[h] + 1] for h in heads]
        m_chunk = [jnp.max(a_log[h], axis=0, keepdims=True) for h in heads]
        wk = [k[h] * jnp.exp(a_log[h] - m_chunk[h]) for h in heads]
        d_state = [_bdot(wk[h], v_aug[h], _TN) for h in heads]
        d_log = [jnp.where(incl, bc[h] + comb_row[h], -jnp.inf) for h in heads]
        m_loc = [jnp.max(d_log[h], axis=-1, keepdims=True) for h in heads]
        p = [jnp.exp(d_log[h] - m_loc[h]) * qk[h] for h in heads]
        loc = [_bdot(p[h], v_aug[h]) for h in heads]
        for h in heads:
            m_inter = bc[h] + m_st[h]
            m_t = jnp.maximum(m_loc[h], m_inter)
            tot = jnp.exp(m_inter - m_t) * inter[h] + jnp.exp(m_loc[h] - m_t) * loc[h]
            hh = tot[:, :D_HEAD] / jnp.maximum(jnp.abs(tot[:, D_HEAD:]), jnp.exp(-m_t))
            m_new = jnp.maximum(b_last[h] + m_st[h], m_chunk[h])
            state_s[h] = (jnp.exp(b_last[h] + m_st[h] - m_new) * st[h]
                          + jnp.exp(m_chunk[h] - m_new) * d_state[h])
            m_s[h] = m_new
            hh = hh * lax.rsqrt(jnp.mean(hh * hh, axis=-1, keepdims=True) + NORM_EPS) * gn_ref[:, hsl[h]]
            o_ref[0, rows, hsl[h]] = (hh * _sigmoid(og_ref[0, rows, hsl[h]])).astype(o_ref.dtype)
        return carry

    lax.fori_loop(0, n_chunks, chunk_body, 0, unroll=2)


def _mlstm(big3, small3, conv_w, prm, gnorm, cs):
    b, s, _ = big3.shape
    return pl.pallas_call(
        functools.partial(_mlstm_kernel, cs=cs),
        grid=(b, s // cs),
        in_specs=[pl.BlockSpec((1, cs, 2 * D_QK), lambda i, j: (i, j, 2)),
                  pl.BlockSpec((1, cs, D_QK), lambda i, j: (i, j, 6)),
                  pl.BlockSpec((1, cs, D_QK), lambda i, j: (i, j, 7)),
                  pl.BlockSpec((1, cs, LANES), lambda i, j: (i, j, 0)),
                  pl.BlockSpec((CONV_WIDTH, 2 * D_QK), lambda i, j: (0, 0)),
                  pl.BlockSpec((SUBLANES, LANES), lambda i, j: (0, 0)),
                  pl.BlockSpec((1, D_QK), lambda i, j: (0, 0))],
        out_specs=pl.BlockSpec((1, cs, D_QK), lambda i, j: (i, j, 0)),
        out_shape=jax.ShapeDtypeStruct((b, s, D_QK), BF16),
        scratch_shapes=[pltpu.VMEM((cs + CONV_CARRY, 2 * D_QK), F32),
                        pltpu.VMEM((cs, D_QK), F32), pltpu.VMEM((cs, D_QK), F32),
                        pltpu.VMEM((cs, LANES), F32), pltpu.VMEM((cs, LANES), F32),
                        pltpu.VMEM((N_HEADS, D_HEAD, 2 * D_HEAD), F32),
                        pltpu.VMEM((N_HEADS, 1, 1), F32)],
        compiler_params=pltpu.CompilerParams(dimension_semantics=("arbitrary", "arbitrary"),
                                             vmem_limit_bytes=VMEM_LIMIT),
        name="mlstm",
    )(big3, big3, big3, small3, conv_w, prm, gnorm)


N_SEG = D_MODEL // LANES
assert N_SEG == SUBLANES


def _store_token_tiles(ref, val, n):
    for s in range(N_SEG):
        ref[pl.ds(s, n, stride=N_SEG), :] = val[:, s * LANES:(s + 1) * LANES]


def _load_token_tiles(ref, n):
    return jnp.concatenate([ref[pl.ds(s, n, stride=N_SEG), :] for s in range(N_SEG)], axis=1)


def _merge_kernel(oa_ref, ob_ref, gates_ref, x_ref, wa_ref, wb_ref, wo_ref, g_ref, wr_ref, br_ref,
                  x1_ref, h2_ref, key_ref, gate_ref, cnt_ref, carry_s, *, tm):
    @pl.when(pl.program_id(0) == 0)
    def _():
        carry_s[...] = jnp.zeros(carry_s.shape, F32)

    y_a = jnp.dot(oa_ref[...], wa_ref[...], preferred_element_type=F32)
    y_b = jnp.dot(ob_ref[...], wb_ref[...], preferred_element_type=F32)
    y = _sigmoid(gates_ref[:, :D_MODEL]) * y_a + _sigmoid(gates_ref[:, D_MODEL:]) * y_b
    x1 = x_ref[...] + jnp.dot(y.astype(BF16), wo_ref[...], preferred_element_type=F32)
    x1_ref[...] = x1
    h2 = _rms(x1, g_ref[...])
    _store_token_tiles(h2_ref, h2, tm)
    logits = jnp.dot(h2.astype(BF16), wr_ref[...], preferred_element_type=F32) + br_ref[...]

    lane = lax.broadcasted_iota(I32, (tm, LANES), 1).astype(F32)
    vals, sels = [], []
    idx_t = jnp.zeros((tm, LANES), F32)
    work = logits
    for k in range(TOP_K):
        m = jnp.max(work, axis=-1, keepdims=True)
        i = jnp.min(jnp.where(work == m, lane, float(LANES)), axis=-1, keepdims=True)
        sel = lane == i
        work = jnp.where(sel, -jnp.inf, work)
        idx_t = jnp.where(lane == float(k), i, idx_t)
        vals.append(m)
        sels.append(sel)
    es = [jnp.exp(v - vals[0]) for v in vals]
    denom = es[0] + es[1] + es[2] + es[3]
    gate_t = jnp.zeros((tm, LANES), F32)
    for k in range(TOP_K):
        gate_t = jnp.where(lane == float(k), es[k] / denom, gate_t)
    gate_ref[...] = gate_t[:, :TOP_K]

    onehot = jnp.zeros((tm, LANES), F32)
    for sel in sels:
        onehot = onehot + jnp.where(sel, 1.0, 0.0)
    r = lax.broadcasted_iota(I32, (tm, tm), 0)
    c = lax.broadcasted_iota(I32, (tm, tm), 1)
    before = jnp.where(r > c, 1.0, 0.0).astype(BF16)
    cum = jnp.dot(before, onehot.astype(BF16), preferred_element_type=F32) + carry_s[...]
    rank_t = jnp.zeros((tm, LANES), F32)
    for k, sel in enumerate(sels):
        rk = jnp.sum(jnp.where(sel, cum, 0.0), axis=-1, keepdims=True)
        rank_t = jnp.where(lane == float(k), rk, rank_t)
    key_ref[...] = (idx_t * float(RANK_SPAN) + rank_t)[:, :TOP_K].astype(I32)
    total = carry_s[...] + jnp.sum(onehot, axis=0, keepdims=True)
    carry_s[...] = total
    cnt_ref[...] = total.astype(I32)


def _merge(oa, ob, big, x2, wa, wb, wo, g, wr, br, tm):
    t = x2.shape[0]
    const = lambda i: (0, 0)
    return pl.pallas_call(
        functools.partial(_merge_kernel, tm=tm),
        grid=(t // tm,),
        in_specs=[pl.BlockSpec((tm, D_QK), lambda i: (i, 0)),
                  pl.BlockSpec((tm, D_QK), lambda i: (i, 0)),
                  pl.BlockSpec((tm, 2 * D_MODEL), lambda i: (i, 2)),
                  pl.BlockSpec((tm, D_MODEL), lambda i: (i, 0)),
                  pl.BlockSpec((D_QK, D_MODEL), const),
                  pl.BlockSpec((D_QK, D_MODEL), const),
                  pl.BlockSpec((D_MODEL, D_MODEL), const),
                  pl.BlockSpec((1, D_MODEL), const),
                  pl.BlockSpec((D_MODEL, LANES), const),
                  pl.BlockSpec((1, LANES), const)],
        out_specs=[pl.BlockSpec((tm, D_MODEL), lambda i: (i, 0)),
                   pl.BlockSpec((tm * N_SEG, LANES), lambda i: (i, 0)),
                   pl.BlockSpec((tm, TOP_K), lambda i: (i, 0)),
                   pl.BlockSpec((tm, TOP_K), lambda i: (i, 0)),
                   pl.BlockSpec((1, LANES), const)],
        out_shape=[jax.ShapeDtypeStruct((t, D_MODEL), F32),
                   jax.ShapeDtypeStruct((t * N_SEG, LANES), F32),
                   jax.ShapeDtypeStruct((t, TOP_K), I32),
                   jax.ShapeDtypeStruct((t, TOP_K), F32),
                   jax.ShapeDtypeStruct((1, LANES), I32)],
        scratch_shapes=[pltpu.VMEM((1, LANES), F32)],
        compiler_params=pltpu.CompilerParams(dimension_semantics=("arbitrary",),
                                             vmem_limit_bytes=VMEM_LIMIT),
        name="merge_router",
    )(oa, ob, big, x2, wa, wb, wo, g, wr, br)


def _expert_kernel(te_ref, nu_ref, nv_ref, tok_ref, tok_next_ref, slot_ref, slot_prev_ref, h2_hbm,
                   wgu_ref, wd_ref, perm_ref, bg_ref, bu_ref, bd_ref, y_hbm,
                   xbuf0, xbuf1, ybuf0, ybuf1, wg_s, wu_s, wd_s, in_sem, out_sem, *, bm):
    i = pl.program_id(0)
    n_used = nu_ref[0]
    new_expert = (i == 0) | (te_ref[i] != te_ref[jnp.maximum(i - 1, 0)])
    xbufs = (xbuf0, xbuf1)
    ybufs = (ybuf0, ybuf1)

    def token_tile(ref, r):
        return ref.at[pl.ds(pl.multiple_of(r * N_SEG, N_SEG), N_SEG)]

    def gather_row(ids_ref, s, r):
        pltpu.make_async_copy(token_tile(h2_hbm, ids_ref[r]), token_tile(xbufs[s], r),
                              in_sem.at[s]).start()

    def wait_gather(s):
        pltpu.make_async_copy(h2_hbm.at[pl.ds(0, bm * N_SEG)], xbufs[s], in_sem.at[s]).wait()

    def wait_scatter(s, n_rows):
        n = pl.multiple_of(n_rows * N_SEG, N_SEG)
        pltpu.make_async_copy(ybufs[s].at[pl.ds(0, n)], y_hbm.at[pl.ds(0, n)], out_sem.at[s]).wait()

    @pl.when(i == 0)
    def _():
        def body(r, carry):
            gather_row(tok_ref, 0, r)
            return carry
        lax.fori_loop(0, bm, body, 0, unroll=DMA_UNROLL)

    n_prev = jnp.where(i >= 1, nv_ref[jnp.maximum(i - 1, 0)], 0)

    def scatter_row(ids_ref, s, r):
        pltpu.make_async_copy(token_tile(ybufs[s], r), token_tile(y_hbm, ids_ref[r]),
                              out_sem.at[s]).start()

    def scatter_loop(ids_ref, s, n_rows):
        def one(r, carry):
            scatter_row(ids_ref, s, r)
            return carry

        def group(j, carry):
            for q in range(DMA_UNROLL):
                scatter_row(ids_ref, s, j * DMA_UNROLL + q)
            return carry
        n_groups = n_rows // DMA_UNROLL
        lax.fori_loop(0, n_groups, group, 0)
        lax.fori_loop(n_groups * DMA_UNROLL, n_rows, one, 0)

    def tile_body(cur, prev_full):
        nxt = 1 - cur
        wait_gather(cur)

        @pl.when(i >= 2)
        def _():
            wait_scatter(cur, nv_ref[i - 2])

        if not prev_full:
            scatter_loop(slot_prev_ref, nxt, n_prev)

        @pl.when(new_expert)
        def _():
            half = PERM_BLOCK // 2
            for j in range(2 * D_EXPERT // PERM_BLOCK):
                blk = wgu_ref[0, :, j * PERM_BLOCK:(j + 1) * PERM_BLOCK].astype(BF16)
                split = jnp.dot(blk, perm_ref[...], preferred_element_type=F32)
                wg_s[:, j * half:(j + 1) * half] = split[:, :half].astype(BF16)
                wu_s[:, j * half:(j + 1) * half] = split[:, half:].astype(BF16)
            wd_s[...] = wd_ref[0].astype(BF16)

        n_grp = bm // DMA_GROUPS

        def dma_group(gi):
            for r in range(gi * n_grp, (gi + 1) * n_grp):
                gather_row(tok_next_ref, nxt, r)
                if prev_full:
                    scatter_row(slot_prev_ref, nxt, r)

        dma_group(0)
        x = _load_token_tiles(xbufs[cur], bm).astype(BF16)
        g = jnp.dot(x, wg_s[...], preferred_element_type=F32) + bg_ref[0]
        dma_group(1)
        u = jnp.dot(x, wu_s[...], preferred_element_type=F32) + bu_ref[0]
        dma_group(2)
        gate = jnp.minimum(g, SWIGLU_LIMIT)
        up = jnp.clip(u, -SWIGLU_LIMIT, SWIGLU_LIMIT)
        act = gate * _sigmoid(SWIGLU_ALPHA * gate) * (up + 1.0)
        dma_group(3)
        y = jnp.dot(act.astype(BF16), wd_s[...], preferred_element_type=F32) + bd_ref[0]
        _store_token_tiles(ybufs[cur], y, bm)

        @pl.when(i == n_used - 1)
        def _():
            scatter_loop(slot_ref, cur, nv_ref[i])
            wait_gather(nxt)
            wait_scatter(cur, nv_ref[i])

            @pl.when(i >= 1)
            def _():
                wait_scatter(nxt, nv_ref[i - 1])

    for parity in range(2):
        for prev_full in (False, True):
            full = (n_prev == bm) if prev_full else (n_prev != bm)
            cond = (i < n_used) & (lax.rem(i, 2) == parity) & full
            pl.when(cond)(functools.partial(tile_body, parity, prev_full))


def _split_permutation():
    half = PERM_BLOCK // 2
    src = jnp.arange(PERM_BLOCK)[:, None]
    dst = jnp.arange(PERM_BLOCK)[None, :]
    return (src == jnp.where(dst < half, 2 * dst, 2 * (dst - half) + 1)).astype(BF16)


def _experts(tile_expert, n_used, tile_valid, row_tok, row_slot, h2, wgu, wd, bg, bu, bd, bm):
    n_rows = row_tok.shape[0]
    n_tiles = n_rows // bm
    n_slots = h2.shape[0] // N_SEG * TOP_K
    cur = lambda i, te, nu, nv: (jnp.minimum(i, nu[0] - 1),)
    nxt = lambda i, te, nu, nv: (jnp.minimum(i + 1, nu[0] - 1),)
    prv = lambda i, te, nu, nv: (jnp.clip(i - 1, 0, nu[0] - 1),)
    w_map = lambda i, te, nu, nv: (te[i], 0, 0)
    grid_spec = pltpu.PrefetchScalarGridSpec(
        num_scalar_prefetch=3,
        grid=(n_tiles,),
        in_specs=[pl.BlockSpec((bm,), cur, memory_space=pltpu.SMEM),
                  pl.BlockSpec((bm,), nxt, memory_space=pltpu.SMEM),
                  pl.BlockSpec((bm,), cur, memory_space=pltpu.SMEM),
                  pl.BlockSpec((bm,), prv, memory_space=pltpu.SMEM),
                  pl.BlockSpec(memory_space=pl.ANY),
                  pl.BlockSpec((1, D_MODEL, 2 * D_EXPERT), w_map),
                  pl.BlockSpec((1, D_EXPERT, D_MODEL), w_map),
                  pl.BlockSpec((PERM_BLOCK, PERM_BLOCK), lambda i, te, nu, nv: (0, 0)),
                  pl.BlockSpec((1, 1, D_EXPERT), w_map),
                  pl.BlockSpec((1, 1, D_EXPERT), w_map),
                  pl.BlockSpec((1, 1, D_MODEL), w_map)],
        out_specs=pl.BlockSpec(memory_space=pl.ANY),
        scratch_shapes=[pltpu.VMEM((bm * N_SEG, LANES), F32), pltpu.VMEM((bm * N_SEG, LANES), F32),
                        pltpu.VMEM((bm * N_SEG, LANES), F32), pltpu.VMEM((bm * N_SEG, LANES), F32),
                        pltpu.VMEM((D_MODEL, D_EXPERT), BF16), pltpu.VMEM((D_MODEL, D_EXPERT), BF16),
                        pltpu.VMEM((D_EXPERT, D_MODEL), BF16),
                        pltpu.SemaphoreType.DMA((2,)), pltpu.SemaphoreType.DMA((2,))],
    )
    return pl.pallas_call(
        functools.partial(_expert_kernel, bm=bm),
        grid_spec=grid_spec,
        out_shape=jax.ShapeDtypeStruct((n_slots * N_SEG, LANES), F32),
        compiler_params=pltpu.CompilerParams(dimension_semantics=("arbitrary",),
                                             vmem_limit_bytes=VMEM_LIMIT,
                                             has_side_effects=True),
        name="experts",
    )(tile_expert, n_used, tile_valid, row_tok, row_tok, row_slot, row_slot, h2, wgu, wd, _split_permutation(),
      bg, bu, bd)


def _combine_kernel(y0_ref, y1_ref, y2_ref, y3_ref, gate_ref, x1_ref, g_ref, o_ref, *, tm):
    acc = x1_ref[...]
    for k, y_ref in enumerate((y0_ref, y1_ref, y2_ref, y3_ref)):
        acc = acc + gate_ref[:, k:k + 1] * _load_token_tiles(y_ref, tm)
    o_ref[...] = _rms(acc, g_ref[...])


def _combine(y_slots, gates, x1, g, tm):
    t = x1.shape[0]
    nb = t // tm

    def slot_spec(k):
        return pl.BlockSpec((tm * N_SEG, LANES), lambda i: (k * nb + i, 0))

    return pl.pallas_call(
        functools.partial(_combine_kernel, tm=tm),
        grid=(nb,),
        in_specs=[slot_spec(0), slot_spec(1), slot_spec(2), slot_spec(3),
                  pl.BlockSpec((tm, TOP_K), lambda i: (i, 0)),
                  pl.BlockSpec((tm, D_MODEL), lambda i: (i, 0)),
                  pl.BlockSpec((1, D_MODEL), lambda i: (0, 0))],
        out_specs=pl.BlockSpec((tm, D_MODEL), lambda i: (i, 0)),
        out_shape=jax.ShapeDtypeStruct((t, D_MODEL), F32),
        compiler_params=pltpu.CompilerParams(dimension_semantics=("arbitrary",),
                                             vmem_limit_bytes=VMEM_LIMIT),
        name="combine",
    )(y_slots, y_slots, y_slots, y_slots, gates, x1, g)


def _lane_row(vec, lane0):
    n = vec.shape[0]
    return jnp.zeros((SUBLANES, LANES), F32).at[0, lane0:lane0 + n].set(vec.astype(F32))


def _tile_size(n, pref):
    return pref if n % pref == 0 else n


def kernel(x, norm_mix, w_in, gdn_conv, gdn_a_log, gdn_dt_bias, gdn_norm, ml_conv, ml_b_i, ml_b_f,
           ml_norm, w_up_gdn, w_up_ml, w_out, norm_ffn, w_router, b_router, w_gate_up, b_gate_up,
           w_down, b_down, norm_final):
    assert norm_mix.shape[0] == 1, "single-layer stack"
    b, s, d = x.shape
    assert d == D_MODEL and s % CHUNK == 0
    t = b * s
    x2 = x.reshape(t, d)

    w = w_in[0]
    w_big = jnp.concatenate([_cols(w, n) for n in ("g_q", "g_k", "g_v", "g_z", "m_q", "m_k", "m_v",
                                                   "m_o", "gate_gdn", "gate_ml")], axis=1).astype(BF16)
    w_small = jnp.concatenate([_cols(w, n) for n in ("g_a", "g_b", "m_i", "m_f")], axis=1)
    w_small = jnp.pad(w_small, ((0, 0), (0, LANES - N_SMALL))).astype(BF16)

    tm = _tile_size(t, 512)
    big, small = _in_proj(x2, norm_mix[0][None, :], w_big, w_small, tm)
    big3 = big.reshape(b, s, N_BIG)
    small3 = small.reshape(b, s, LANES)

    cs = _tile_size(s, 512)
    gdn_prm = _lane_row(gdn_a_log[0], 0).at[1, 0:N_HEADS].set(gdn_dt_bias[0].astype(F32))
    oa = _gdn(big3, small3, gdn_conv[0].astype(F32), gdn_prm, gdn_norm[0][None, :].astype(F32), cs)
    ml_prm = _lane_row(ml_b_i[0], 2 * N_HEADS).at[0, 3 * N_HEADS:4 * N_HEADS].set(ml_b_f[0].astype(F32))
    ob = _mlstm(big3, small3, ml_conv[0].astype(F32), ml_prm, ml_norm[0][None, :].astype(F32), cs)

    w_r = jnp.pad(w_router[0], ((0, 0), (0, LANES - N_EXPERTS))).astype(BF16)
    b_r = jnp.full((1, LANES), PAD_LOGIT, F32).at[0, :N_EXPERTS].set(b_router[0].astype(F32))
    x1, h2, key, gates, counts = _merge(
        oa.reshape(t, D_QK), ob.reshape(t, D_QK), big, x2, w_up_gdn[0].astype(BF16),
        w_up_ml[0].astype(BF16), w_out[0].astype(BF16), norm_ffn[0][None, :], w_r, b_r, tm)

    bm = EXPERT_TILE
    n_assign = t * TOP_K
    n_tiles = -(-n_assign // bm) + N_EXPERTS
    counts = counts[0, :N_EXPERTS]
    padded = (counts + bm - 1) // bm * bm
    pend = jnp.cumsum(padded)
    pstart = pend - padded
    n_used = (pend[-1] // bm).astype(I32)
    tile_ids = jnp.minimum(jnp.arange(n_tiles, dtype=I32), n_used - 1)
    tile_expert = jnp.minimum(jnp.sum((pend[None, :] <= (tile_ids * bm)[:, None]).astype(I32), axis=1),
                              N_EXPERTS - 1)
    tile_valid = jnp.clip(counts[tile_expert] - (tile_ids * bm - pstart[tile_expert]), 0, bm).astype(I32)
    assert t <= RANK_SPAN and bm <= RANK_SPAN
    assign = jnp.arange(n_assign, dtype=I32)
    pad_e = jnp.arange(N_EXPERTS, dtype=I32)[:, None]
    pad_p = jnp.arange(bm, dtype=I32)[None, :]
    pad_key = jnp.where(pad_p < (padded - counts)[:, None],
                        pad_e * RANK_SPAN + counts[:, None] + pad_p,
                        N_EXPERTS * RANK_SPAN + pad_e * bm + pad_p)
    _, row_slot = lax.sort((jnp.concatenate([key.reshape(-1), pad_key.reshape(-1)]),
                            jnp.concatenate([(assign % TOP_K) * t + assign // TOP_K,
                                             jnp.full((N_EXPERTS * bm,), -1, I32)])),
                           num_keys=1)
    row_tok = jnp.where(row_slot < 0, jnp.arange(n_tiles * bm, dtype=I32), row_slot) % t

    bgu = b_gate_up[0]
    y_slots = _experts(tile_expert.astype(I32), n_used.reshape(1), tile_valid, row_tok, row_slot, h2,
                       w_gate_up[0], w_down[0],
                       bgu[:, None, 0::2].astype(F32), bgu[:, None, 1::2].astype(F32),
                       b_down[0][:, None, :].astype(F32), bm)
    out = _combine(y_slots, gates, x1, norm_final[None, :], tm)
    return out.reshape(b, s, d)
```

```python
import functools

import jax
import jax.numpy as jnp
from jax import lax
from jax.experimental import pallas as pl
from jax.experimental.pallas import tpu as pltpu

F32 = jnp.float32
BF16 = jnp.bfloat16
I32 = jnp.int32

D_MODEL = 1024
N_HEADS = 4
D_HEAD = 128
D_QK = N_HEADS * D_HEAD
CONV_WIDTH = 4
CHUNK = 64
GATE_SOFTCAP = 15.0
N_EXPERTS = 32
TOP_K = 4
D_EXPERT = 1024
SWIGLU_LIMIT = 7.0
SWIGLU_ALPHA = 1.702
NORM_EPS = 1e-6

LANES = 128
SUBLANES = 8
CONV_CARRY = SUBLANES
PAD_LOGIT = -1e30
N_SMALL = 16
N_BIG = 6 * D_MODEL
VMEM_LIMIT = 56 * 1024 * 1024
EXPERT_TILE = 512
DMA_UNROLL = 8
DMA_GROUPS = 4
RANK_SPAN = 1 << 18
PERM_BLOCK = 256
LOCAL_CHUNKS = 8

_OFF = {}
_o = 0
for _name, _w in (("g_q", D_QK), ("g_k", D_QK), ("g_v", D_QK), ("g_z", D_QK), ("g_a", N_HEADS),
                  ("g_b", N_HEADS), ("m_q", D_QK), ("m_k", D_QK), ("m_v", D_QK), ("m_o", D_QK),
                  ("m_i", N_HEADS), ("m_f", N_HEADS), ("gate_gdn", D_MODEL), ("gate_ml", D_MODEL)):
    _OFF[_name] = (_o, _w)
    _o += _w


def _cols(w, name):
    o, n = _OFF[name]
    return w[:, o:o + n]


_NN = (((1,), (0,)), ((), ()))
_NT = (((1,), (1,)), ((), ()))
_TN = (((0,), (0,)), ((), ()))


def _bdot(a, b, dims=_NN):
    return lax.dot_general(a.astype(BF16), b.astype(BF16), dims, preferred_element_type=F32)


def _split3(a):
    hi = a.astype(BF16)
    r = a - hi.astype(F32)
    mid = r.astype(BF16)
    return hi, mid, (r - mid.astype(F32)).astype(BF16)


def _select_dot(sel01, x, dims=_NN):
    s = sel01.astype(BF16)
    h, m, l = _split3(x)
    d = functools.partial(lax.dot_general, dimension_numbers=dims, preferred_element_type=F32)
    return d(s, h) + (d(s, m) + d(s, l))


def _sigmoid(x):
    return 1.0 / (1.0 + jnp.exp(-x))


def _silu(x):
    return x * _sigmoid(x)


def _softplus(x):
    return jnp.maximum(x, 0.0) + jnp.log(1.0 + jnp.exp(-jnp.abs(x)))


def _rms(x, g):
    return x * lax.rsqrt(jnp.mean(x * x, axis=-1, keepdims=True) + NORM_EPS) * g


def _in_proj_kernel(x_ref, g_ref, wbig_ref, wsmall_ref, big_ref, small_ref, *, n_chunk):
    h = _rms(x_ref[...], g_ref[...]).astype(BF16)
    for c in range(N_BIG // n_chunk):
        sl = slice(c * n_chunk, (c + 1) * n_chunk)
        big_ref[:, sl] = jnp.dot(h, wbig_ref[:, sl], preferred_element_type=F32)
    small_ref[...] = jnp.dot(h, wsmall_ref[...], preferred_element_type=F32)


def _in_proj(x2, g, w_big, w_small, tm):
    t = x2.shape[0]
    return pl.pallas_call(
        functools.partial(_in_proj_kernel, n_chunk=1024),
        grid=(t // tm,),
        in_specs=[pl.BlockSpec((tm, D_MODEL), lambda i: (i, 0)),
                  pl.BlockSpec((1, D_MODEL), lambda i: (0, 0)),
                  pl.BlockSpec((D_MODEL, N_BIG), lambda i: (0, 0)),
                  pl.BlockSpec((D_MODEL, LANES), lambda i: (0, 0))],
        out_specs=[pl.BlockSpec((tm, N_BIG), lambda i: (i, 0)),
                   pl.BlockSpec((tm, LANES), lambda i: (i, 0))],
        out_shape=[jax.ShapeDtypeStruct((t, N_BIG), F32),
                   jax.ShapeDtypeStruct((t, LANES), F32)],
        compiler_params=pltpu.CompilerParams(dimension_semantics=("arbitrary",),
                                             vmem_limit_bytes=VMEM_LIMIT),
        name="in_proj",
    )(x2, g, w_big, w_small)


def _chunk_masks():
    r = lax.broadcasted_iota(I32, (CHUNK, CHUNK), 0)
    c = lax.broadcasted_iota(I32, (CHUNK, CHUNK), 1)
    return r > c, r >= c


def _block_cumsum_matrix(cs):
    r = lax.broadcasted_iota(I32, (cs, cs), 0)
    c = lax.broadcasted_iota(I32, (cs, cs), 1)
    same = (r // CHUNK) == (c // CHUNK)
    return jnp.where(same & (r >= c), 1.0, 0.0).astype(F32)


def _lane_onehot(lane):
    return jnp.where(lax.broadcasted_iota(I32, (CHUNK, LANES), 1) == lane, 1.0, 0.0).astype(F32)


def _conv_silu_pieces(x_ref, w_ref, buf_ref, cs, width):
    assert CONV_WIDTH - 1 <= CONV_CARRY

    @pl.when(pl.program_id(1) == 0)
    def _():
        buf_ref[0:CONV_CARRY, :] = jnp.zeros((CONV_CARRY, buf_ref.shape[1]), F32)

    buf_ref[CONV_CARRY:CONV_CARRY + cs, :] = x_ref[0]
    for r0 in range(0, cs, CHUNK):
        for c0 in range(0, buf_ref.shape[1], width):
            cols = slice(c0, c0 + width)
            top = CONV_CARRY + r0
            acc = w_ref[CONV_WIDTH - 1:CONV_WIDTH, cols] * buf_ref[top:top + CHUNK, cols]
            for j in range(CONV_WIDTH - 1):
                s = top - (CONV_WIDTH - 1) + j
                acc = acc + w_ref[j:j + 1, cols] * buf_ref[s:s + CHUNK, cols]
            yield r0, c0, _silu(acc)
    buf_ref[0:CONV_CARRY, :] = buf_ref[cs:cs + CONV_CARRY, :]


def _gdn_kernel(qkv_ref, z_ref, sm_ref, conv_ref, prm_ref, gn_ref, o_ref,
                buf_ref, q_s, k_s, v_s, w_s, attn_s, gam_s, beta_s, state_s, *, cs):
    n_chunks = cs // CHUNK

    @pl.when(pl.program_id(1) == 0)
    def _():
        state_s[...] = jnp.zeros(state_s.shape, F32)

    for r0, c0, piece in _conv_silu_pieces(qkv_ref, conv_ref, buf_ref, cs, D_QK):
        rows = slice(r0, r0 + CHUNK)
        if c0 == 2 * D_QK:
            v_s[rows, :] = piece
            continue
        dst, scale = (q_s, D_HEAD ** -0.5) if c0 == 0 else (k_s, 1.0)
        for h in range(N_HEADS):
            sl = slice(h * D_HEAD, (h + 1) * D_HEAD)
            a = piece[:, sl]
            dst[rows, sl] = a * (lax.rsqrt(jnp.sum(a * a, axis=-1, keepdims=True) + NORM_EPS) * scale)

    sm = sm_ref[0]
    logdec = -jnp.exp(prm_ref[0:1, :]) * _softplus(sm + prm_ref[1:2, :])
    gam_s[...] = _select_dot(_block_cumsum_matrix(cs), logdec)
    beta_s[...] = _sigmoid(sm)

    strict, incl = _chunk_masks()
    eye = jnp.where(lax.broadcasted_iota(I32, (CHUNK, CHUNK), 0)
                    == lax.broadcasted_iota(I32, (CHUNK, CHUNK), 1), 1.0, 0.0).astype(F32)
    gnorm = gn_ref[...]

    heads = range(N_HEADS)
    hsl = [slice(h * D_HEAD, (h + 1) * D_HEAD) for h in heads]
    asl = [slice(h * CHUNK, (h + 1) * CHUNK) for h in heads]
    onehots = [_lane_onehot(h) for h in heads]

    def local_body(it, carry):
        rows = [pl.ds(pl.multiple_of((it * LOCAL_CHUNKS + ci) * CHUNK, CHUNK), CHUNK)
                for ci in range(LOCAL_CHUNKS)]
        probs = [(ci, h) for ci in range(LOCAL_CHUNKS) for h in heads]
        gam_all = [gam_s[r, :] for r in rows]
        beta_all = [beta_s[r, :] for r in rows]
        q = [q_s[rows[ci], hsl[h]] for ci, h in probs]
        k = [k_s[rows[ci], hsl[h]] for ci, h in probs]
        v = [v_s[rows[ci], hsl[h]] for ci, h in probs]
        gc = [gam_all[ci][:, h:h + 1] for ci, h in probs]
        beta = [beta_all[ci][:, N_HEADS + h:N_HEADS + h + 1] for ci, h in probs]
        g_row = [_select_dot(onehots[h], gam_all[ci], _NT) for ci, h in probs]
        kb = [a * b for a, b in zip(k, beta)]
        kk = [_bdot(a, b, _NT) for a, b in zip(kb, k)]
        qk = [_bdot(a, b, _NT) for a, b in zip(q, k)]
        decay = [jnp.where(incl, jnp.exp(jnp.minimum(a - b, 0.0)), 0.0) for a, b in zip(gc, g_row)]
        lower = [jnp.where(strict, a * b, 0.0) for a, b in zip(kk, decay)]
        inv = [eye - a for a in lower]
        pw = [_bdot(a, a) for a in lower]
        for lvl in range(5):
            upd = [_bdot(a, b) for a, b in zip(inv, pw)]
            if lvl < 4:
                pw = [_bdot(a, a) for a in pw]
            inv = [a + b for a, b in zip(inv, upd)]
        e_gc = [jnp.exp(a) for a in gc]
        rhs = [jnp.concatenate([a * b, c * d], axis=1) for a, b, c, d in zip(v, beta, kb, e_gc)]
        sol = [_bdot(a, b) for a, b in zip(inv, rhs)]
        for p, (ci, h) in enumerate(probs):
            v_s[rows[ci], hsl[h]] = sol[p][:, :D_HEAD]
            w_s[rows[ci], hsl[h]] = sol[p][:, D_HEAD:]
            attn_s[rows[ci], asl[h]] = qk[p] * decay[p]
            q_s[rows[ci], hsl[h]] = q[p] * e_gc[p]
            k_s[rows[ci], hsl[h]] = k[p] * jnp.exp(gc[p][CHUNK - 1:CHUNK, :] - gc[p])
        return carry

    lax.fori_loop(0, n_chunks // LOCAL_CHUNKS, local_body, 0)

    def state_body(c, carry):
        rows = pl.ds(pl.multiple_of(c * CHUNK, CHUNK), CHUNK)
        g_last = gam_s[pl.ds(c * CHUNK + CHUNK - 1, 1), :]
        st = [state_s[h] for h in heads]
        ws = [_bdot(w_s[rows, hsl[h]], st[h]) for h in heads]
        qs = [_bdot(q_s[rows, hsl[h]], st[h]) for h in heads]
        u = [v_s[rows, hsl[h]] - ws[h] for h in heads]
        au = [_bdot(attn_s[rows, asl[h]], u[h]) for h in heads]
        ku = [_bdot(k_s[rows, hsl[h]], u[h], _TN) for h in heads]
        for h in heads:
            state_s[h] = st[h] * jnp.exp(g_last[:, h:h + 1]) + ku[h]
            o = qs[h] + au[h]
            o = o * lax.rsqrt(jnp.mean(o * o, axis=-1, keepdims=True) + NORM_EPS) * gnorm
            o_ref[0, rows, hsl[h]] = (o * _silu(z_ref[0, rows, hsl[h]])).astype(o_ref.dtype)
        return carry

    lax.fori_loop(0, n_chunks, state_body, 0, unroll=2)


def _gdn(big3, small3, conv_w, prm, gnorm, cs):
    b, s, _ = big3.shape
    return pl.pallas_call(
        functools.partial(_gdn_kernel, cs=cs),
        grid=(b, s // cs),
        in_specs=[pl.BlockSpec((1, cs, 3 * D_QK), lambda i, j: (i, j, 0)),
                  pl.BlockSpec((1, cs, D_QK), lambda i, j: (i, j, 3)),
                  pl.BlockSpec((1, cs, LANES), lambda i, j: (i, j, 0)),
                  pl.BlockSpec((CONV_WIDTH, 3 * D_QK), lambda i, j: (0, 0)),
                  pl.BlockSpec((SUBLANES, LANES), lambda i, j: (0, 0)),
                  pl.BlockSpec((1, D_HEAD), lambda i, j: (0, 0))],
        out_specs=pl.BlockSpec((1, cs, D_QK), lambda i, j: (i, j, 0)),
        out_shape=jax.ShapeDtypeStruct((b, s, D_QK), BF16),
        scratch_shapes=[pltpu.VMEM((cs + CONV_CARRY, 3 * D_QK), F32),
                        pltpu.VMEM((cs, D_QK), F32), pltpu.VMEM((cs, D_QK), F32),
                        pltpu.VMEM((cs, D_QK), F32), pltpu.VMEM((cs, D_QK), F32),
                        pltpu.VMEM((cs, N_HEADS * CHUNK), F32),
                        pltpu.VMEM((cs, LANES), F32), pltpu.VMEM((cs, LANES), F32),
                        pltpu.VMEM((N_HEADS, D_HEAD, D_HEAD), F32)],
        compiler_params=pltpu.CompilerParams(dimension_semantics=("arbitrary", "arbitrary"),
                                             vmem_limit_bytes=VMEM_LIMIT),
        name="gdn",
    )(big3, big3, small3, conv_w, prm, gnorm)


def _mlstm_kernel(qk_ref, v_ref, og_ref, sm_ref, conv_ref, prm_ref, gn_ref, o_ref,
                  buf_ref, q_s, k_s, bcum_s, ipre_s, state_s, m_s, *, cs):
    n_chunks = cs // CHUNK

    @pl.when(pl.program_id(1) == 0)
    def _():
        state_s[...] = jnp.zeros(state_s.shape, F32)
        m_s[...] = jnp.zeros(m_s.shape, F32)

    for r0, c0, piece in _conv_silu_pieces(qk_ref, conv_ref, buf_ref, cs, D_QK):
        if c0 == 0:
            q_s[r0:r0 + CHUNK, :] = piece
        else:
            k_s[r0:r0 + CHUNK, :] = piece * (D_HEAD ** -0.5)

    pre = sm_ref[0] + prm_ref[0:1, :]
    capped = GATE_SOFTCAP * jnp.tanh(pre / GATE_SOFTCAP)
    logf = -_softplus(-capped)
    bcum_s[...] = _select_dot(_block_cumsum_matrix(cs), logf)
    ipre_s[...] = pltpu.roll(capped, N_HEADS, axis=1)

    _, incl = _chunk_masks()
    ones_aug = jnp.ones((CHUNK, D_HEAD), F32)

    heads = range(N_HEADS)
    hsl = [slice(h * D_HEAD, (h + 1) * D_HEAD) for h in heads]
    lanes = [3 * N_HEADS + h for h in heads]
    onehots = [_lane_onehot(lane) for lane in lanes]

    def chunk_body(c, carry):
        rows = pl.ds(pl.multiple_of(c * CHUNK, CHUNK), CHUNK)
        b_all = bcum_s[rows, :]
        comb_all = ipre_s[rows, :] - b_all
        q = [q_s[rows, hsl[h]] for h in heads]
        k = [k_s[rows, hsl[h]] for h in heads]
        v_aug = [jnp.concatenate([v_ref[0, rows, hsl[h]], ones_aug], axis=1) for h in heads]
        st = [state_s[h] for h in heads]
        m_st = [m_s[h] for h in heads]
        bc = [b_all[:, lane:lane + 1] for lane in lanes]
        comb_row = [_select_dot(onehots[h], comb_all, _NT) for h in heads]
        qk = [_bdot(q[h], k[h], _NT) for h in heads]
        inter = [_bdot(q[h], st[h]) for h in heads]
        b_last = [bc[h][CHUNK - 1:CHUNK, :] for h in heads]
        a_log = [b_last[h] + comb_all[:, lanes[h]:lanes[h] + 1] for h in heads]
        m_chunk = [jnp.max(a_log[h], axis=0, keepdims=True) for h in heads]
        wk = [k[h] * jnp.exp(a_log[h] - m_chunk[h]) for h in heads]
        d_state = [_bdot(wk[h], v_aug[h], _TN) for h in heads]
        d_log = [jnp.where(incl, bc[h] + comb_row[h], -jnp.inf) for h in heads]
        m_loc = [jnp.max(d_log[h], axis=-1, keepdims=True) for h in heads]
        p = [jnp.exp(d_log[h] - m_loc[h]) * qk[h] for h in heads]
        loc = [_bdot(p[h], v_aug[h]) for h in heads]
        for h in heads:
            m_inter = bc[h] + m_st[h]
            m_t = jnp.maximum(m_loc[h], m_inter)
            tot = jnp.exp(m_inter - m_t) * inter[h] + jnp.exp(m_loc[h] - m_t) * loc[h]
            hh = tot[:, :D_HEAD] / jnp.maximum(jnp.abs(tot[:, D_HEAD:]), jnp.exp(-m_t))
            m_new = jnp.maximum(b_last[h] + m_st[h], m_chunk[h])
            state_s[h] = (jnp.exp(b_last[h] + m_st[h] - m_new) * st[h]
                          + jnp.exp(m_chunk[h] - m_new) * d_state[h])
            m_s[h] = m_new
            hh = hh * lax.rsqrt(jnp.mean(hh * hh, axis=-1, keepdims=True) + NORM_EPS) * gn_ref[:, hsl[h]]
            o_ref[0, rows, hsl[h]] = (hh * _sigmoid(og_ref[0, rows, hsl[h]])).astype(o_ref.dtype)
        return carry

    lax.fori_loop(0, n_chunks, chunk_body, 0)


def _mlstm(big3, small3, conv_w, prm, gnorm, cs):
    b, s, _ = big3.shape
    return pl.pallas_call(
        functools.partial(_mlstm_kernel, cs=cs),
        grid=(b, s // cs),
        in_specs=[pl.BlockSpec((1, cs, 2 * D_QK), lambda i, j: (i, j, 2)),
                  pl.BlockSpec((1, cs, D_QK), lambda i, j: (i, j, 6)),
                  pl.BlockSpec((1, cs, D_QK), lambda i, j: (i, j, 7)),
                  pl.BlockSpec((1, cs, LANES), lambda i, j: (i, j, 0)),
                  pl.BlockSpec((CONV_WIDTH, 2 * D_QK), lambda i, j: (0, 0)),
                  pl.BlockSpec((SUBLANES, LANES), lambda i, j: (0, 0)),
                  pl.BlockSpec((1, D_QK), lambda i, j: (0, 0))],
        out_specs=pl.BlockSpec((1, cs, D_QK), lambda i, j: (i, j, 0)),
        out_shape=jax.ShapeDtypeStruct((b, s, D_QK), BF16),
        scratch_shapes=[pltpu.VMEM((cs + CONV_CARRY, 2 * D_QK), F32),
                        pltpu.VMEM((cs, D_QK), F32), pltpu.VMEM((cs, D_QK), F32),
                        pltpu.VMEM((cs, LANES), F32), pltpu.VMEM((cs, LANES), F32),
                        pltpu.VMEM((N_HEADS, D_HEAD, 2 * D_HEAD), F32),
                        pltpu.VMEM((N_HEADS, 1, 1), F32)],
        compiler_params=pltpu.CompilerParams(dimension_semantics=("arbitrary", "arbitrary"),
                                             vmem_limit_bytes=VMEM_LIMIT),
        name="mlstm",
    )(big3, big3, big3, small3, conv_w, prm, gnorm)


N_SEG = D_MODEL // LANES
assert N_SEG == SUBLANES


def _store_token_tiles(ref, val, n):
    for s in range(N_SEG):
        ref[pl.ds(s, n, stride=N_SEG), :] = val[:, s * LANES:(s + 1) * LANES]


def _load_token_tiles(ref, n):
    return jnp.concatenate([ref[pl.ds(s, n, stride=N_SEG), :] for s in range(N_SEG)], axis=1)


def _merge_kernel(oa_ref, ob_ref, gates_ref, x_ref, wa_ref, wb_ref, wo_ref, g_ref, wr_ref, br_ref,
                  x1_ref, h2_ref, key_ref, gate_ref, cnt_ref, carry_s, *, tm):
    @pl.when(pl.program_id(0) == 0)
    def _():
        carry_s[...] = jnp.zeros(carry_s.shape, F32)

    y_a = jnp.dot(oa_ref[...], wa_ref[...], preferred_element_type=F32)
    y_b = jnp.dot(ob_ref[...], wb_ref[...], preferred_element_type=F32)
    y = _sigmoid(gates_ref[:, :D_MODEL]) * y_a + _sigmoid(gates_ref[:, D_MODEL:]) * y_b
    x1 = x_ref[...] + jnp.dot(y.astype(BF16), wo_ref[...], preferred_element_type=F32)
    x1_ref[...] = x1
    h2 = _rms(x1, g_ref[...])
    _store_token_tiles(h2_ref, h2, tm)
    logits = jnp.dot(h2.astype(BF16), wr_ref[...], preferred_element_type=F32) + br_ref[...]

    lane = lax.broadcasted_iota(I32, (tm, LANES), 1).astype(F32)
    vals, sels = [], []
    idx_t = jnp.zeros((tm, LANES), F32)
    work = logits
    for k in range(TOP_K):
        m = jnp.max(work, axis=-1, keepdims=True)
        i = jnp.min(jnp.where(work == m, lane, float(LANES)), axis=-1, keepdims=True)
        sel = lane == i
        work = jnp.where(sel, -jnp.inf, work)
        idx_t = jnp.where(lane == float(k), i, idx_t)
        vals.append(m)
        sels.append(sel)
    es = [jnp.exp(v - vals[0]) for v in vals]
    denom = es[0] + es[1] + es[2] + es[3]
    gate_t = jnp.zeros((tm, LANES), F32)
    for k in range(TOP_K):
        gate_t = jnp.where(lane == float(k), es[k] / denom, gate_t)
    gate_ref[...] = gate_t[:, :TOP_K]

    onehot = jnp.zeros((tm, LANES), F32)
    for sel in sels:
        onehot = onehot + jnp.where(sel, 1.0, 0.0)
    r = lax.broadcasted_iota(I32, (tm, tm), 0)
    c = lax.broadcasted_iota(I32, (tm, tm), 1)
    before = jnp.where(r > c, 1.0, 0.0).astype(BF16)
    cum = jnp.dot(before, onehot.astype(BF16), preferred_element_type=F32) + carry_s[...]
    rank_t = jnp.zeros((tm, LANES), F32)
    for k, sel in enumerate(sels):
        rk = jnp.sum(jnp.where(sel, cum, 0.0), axis=-1, keepdims=True)
        rank_t = jnp.where(lane == float(k), rk, rank_t)
    key_ref[...] = (idx_t * float(RANK_SPAN) + rank_t)[:, :TOP_K].astype(I32)
    total = carry_s[...] + jnp.sum(onehot, axis=0, keepdims=True)
    carry_s[...] = total
    cnt_ref[...] = total.astype(I32)


def _merge(oa, ob, big, x2, wa, wb, wo, g, wr, br, tm):
    t = x2.shape[0]
    const = lambda i: (0, 0)
    return pl.pallas_call(
        functools.partial(_merge_kernel, tm=tm),
        grid=(t // tm,),
        in_specs=[pl.BlockSpec((tm, D_QK), lambda i: (i, 0)),
                  pl.BlockSpec((tm, D_QK), lambda i: (i, 0)),
                  pl.BlockSpec((tm, 2 * D_MODEL), lambda i: (i, 2)),
                  pl.BlockSpec((tm, D_MODEL), lambda i: (i, 0)),
                  pl.BlockSpec((D_QK, D_MODEL), const),
                  pl.BlockSpec((D_QK, D_MODEL), const),
                  pl.BlockSpec((D_MODEL, D_MODEL), const),
                  pl.BlockSpec((1, D_MODEL), const),
                  pl.BlockSpec((D_MODEL, LANES), const),
                  pl.BlockSpec((1, LANES), const)],
        out_specs=[pl.BlockSpec((tm, D_MODEL), lambda i: (i, 0)),
                   pl.BlockSpec((tm * N_SEG, LANES), lambda i: (i, 0)),
                   pl.BlockSpec((tm, TOP_K), lambda i: (i, 0)),
                   pl.BlockSpec((tm, TOP_K), lambda i: (i, 0)),
                   pl.BlockSpec((1, LANES), const)],
        out_shape=[jax.ShapeDtypeStruct((t, D_MODEL), F32),
                   jax.ShapeDtypeStruct((t * N_SEG, LANES), F32),
                   jax.ShapeDtypeStruct((t, TOP_K), I32),
                   jax.ShapeDtypeStruct((t, TOP_K), F32),
                   jax.ShapeDtypeStruct((1, LANES), I32)],
        scratch_shapes=[pltpu.VMEM((1, LANES), F32)],
        compiler_params=pltpu.CompilerParams(dimension_semantics=("arbitrary",),
                                             vmem_limit_bytes=VMEM_LIMIT),
        name="merge_router",
    )(oa, ob, big, x2, wa, wb, wo, g, wr, br)


def _expert_kernel(te_ref, nu_ref, nv_ref, tok_ref, tok_next_ref, slot_ref, slot_prev_ref, h2_hbm,
                   wgu_ref, wd_ref, perm_ref, bg_ref, bu_ref, bd_ref, y_hbm,
                   xbuf0, xbuf1, ybuf0, ybuf1, wg_s, wu_s, wd_s, in_sem, out_sem, *, bm):
    i = pl.program_id(0)
    n_used = nu_ref[0]
    new_expert = (i == 0) | (te_ref[i] != te_ref[jnp.maximum(i - 1, 0)])
    xbufs = (xbuf0, xbuf1)
    ybufs = (ybuf0, ybuf1)

    def token_tile(ref, r):
        return ref.at[pl.ds(pl.multiple_of(r * N_SEG, N_SEG), N_SEG)]

    def gather_row(ids_ref, s, r):
        pltpu.make_async_copy(token_tile(h2_hbm, ids_ref[r]), token_tile(xbufs[s], r),
                              in_sem.at[s]).start()

    def wait_gather(s):
        pltpu.make_async_copy(h2_hbm.at[pl.ds(0, bm * N_SEG)], xbufs[s], in_sem.at[s]).wait()

    def wait_scatter(s, n_rows):
        n = pl.multiple_of(n_rows * N_SEG, N_SEG)
        pltpu.make_async_copy(ybufs[s].at[pl.ds(0, n)], y_hbm.at[pl.ds(0, n)], out_sem.at[s]).wait()

    @pl.when(i == 0)
    def _():
        def body(r, carry):
            gather_row(tok_ref, 0, r)
            return carry
        lax.fori_loop(0, bm, body, 0, unroll=DMA_UNROLL)

    n_prev = jnp.where(i >= 1, nv_ref[jnp.maximum(i - 1, 0)], 0)

    def scatter_row(ids_ref, s, r):
        pltpu.make_async_copy(token_tile(ybufs[s], r), token_tile(y_hbm, ids_ref[r]),
                              out_sem.at[s]).start()

    def scatter_loop(ids_ref, s, n_rows):
        def one(r, carry):
            scatter_row(ids_ref, s, r)
            return carry

        def group(j, carry):
            for q in range(DMA_UNROLL):
                scatter_row(ids_ref, s, j * DMA_UNROLL + q)
            return carry
        n_groups = n_rows // DMA_UNROLL
        lax.fori_loop(0, n_groups, group, 0)
        lax.fori_loop(n_groups * DMA_UNROLL, n_rows, one, 0)

    def tile_body(cur, prev_full):
        nxt = 1 - cur
        wait_gather(cur)

        @pl.when(i >= 2)
        def _():
            wait_scatter(cur, nv_ref[i - 2])

        if not prev_full:
            scatter_loop(slot_prev_ref, nxt, n_prev)

        @pl.when(new_expert)
        def _():
            half = PERM_BLOCK // 2
            for j in range(2 * D_EXPERT // PERM_BLOCK):
                blk = wgu_ref[0, :, j * PERM_BLOCK:(j + 1) * PERM_BLOCK].astype(BF16)
                split = jnp.dot(blk, perm_ref[...], preferred_element_type=F32)
                wg_s[:, j * half:(j + 1) * half] = split[:, :half].astype(BF16)
                wu_s[:, j * half:(j + 1) * half] = split[:, half:].astype(BF16)
            wd_s[...] = wd_ref[0].astype(BF16)

        n_grp = bm // DMA_GROUPS

        def dma_group(gi):
            for r in range(gi * n_grp, (gi + 1) * n_grp):
                gather_row(tok_next_ref, nxt, r)
                if prev_full:
                    scatter_row(slot_prev_ref, nxt, r)

        dma_group(0)
        x = _load_token_tiles(xbufs[cur], bm).astype(BF16)
        g = jnp.dot(x, wg_s[...], preferred_element_type=F32) + bg_ref[0]
        dma_group(1)
        u = jnp.dot(x, wu_s[...], preferred_element_type=F32) + bu_ref[0]
        dma_group(2)
        gate = jnp.minimum(g, SWIGLU_LIMIT)
        up = jnp.clip(u, -SWIGLU_LIMIT, SWIGLU_LIMIT)
        act = gate * _sigmoid(SWIGLU_ALPHA * gate) * (up + 1.0)
        dma_group(3)
        y = jnp.dot(act.astype(BF16), wd_s[...], preferred_element_type=F32) + bd_ref[0]
        _store_token_tiles(ybufs[cur], y, bm)

        @pl.when(i == n_used - 1)
        def _():
            scatter_loop(slot_ref, cur, nv_ref[i])
            wait_gather(nxt)
            wait_scatter(cur, nv_ref[i])

            @pl.when(i >= 1)
            def _():
                wait_scatter(nxt, nv_ref[i - 1])

    for parity in range(2):
        for prev_full in (False, True):
            full = (n_prev == bm) if prev_full else (n_prev != bm)
            cond = (i < n_used) & (lax.rem(i, 2) == parity) & full
            pl.when(cond)(functools.partial(tile_body, parity, prev_full))


def _split_permutation():
    half = PERM_BLOCK // 2
    src = jnp.arange(PERM_BLOCK)[:, None]
    dst = jnp.arange(PERM_BLOCK)[None, :]
    return (src == jnp.where(dst < half, 2 * dst, 2 * (dst - half) + 1)).astype(BF16)


def _experts(tile_expert, n_used, tile_valid, row_tok, row_slot, h2, wgu, wd, bg, bu, bd, bm):
    n_rows = row_tok.shape[0]
    n_tiles = n_rows // bm
    n_slots = h2.shape[0] // N_SEG * TOP_K
    cur = lambda i, te, nu, nv: (jnp.minimum(i, nu[0] - 1),)
    nxt = lambda i, te, nu, nv: (jnp.minimum(i + 1, nu[0] - 1),)
    prv = lambda i, te, nu, nv: (jnp.clip(i - 1, 0, nu[0] - 1),)
    w_map = lambda i, te, nu, nv: (te[i], 0, 0)
    grid_spec = pltpu.PrefetchScalarGridSpec(
        num_scalar_prefetch=3,
        grid=(n_tiles,),
        in_specs=[pl.BlockSpec((bm,), cur, memory_space=pltpu.SMEM),
                  pl.BlockSpec((bm,), nxt, memory_space=pltpu.SMEM),
                  pl.BlockSpec((bm,), cur, memory_space=pltpu.SMEM),
                  pl.BlockSpec((bm,), prv, memory_space=pltpu.SMEM),
                  pl.BlockSpec(memory_space=pl.ANY),
                  pl.BlockSpec((1, D_MODEL, 2 * D_EXPERT), w_map),
                  pl.BlockSpec((1, D_EXPERT, D_MODEL), w_map),
                  pl.BlockSpec((PERM_BLOCK, PERM_BLOCK), lambda i, te, nu, nv: (0, 0)),
                  pl.BlockSpec((1, 1, D_EXPERT), w_map),
                  pl.BlockSpec((1, 1, D_EXPERT), w_map),
                  pl.BlockSpec((1, 1, D_MODEL), w_map)],
        out_specs=pl.BlockSpec(memory_space=pl.ANY),
        scratch_shapes=[pltpu.VMEM((bm * N_SEG, LANES), F32), pltpu.VMEM((bm * N_SEG, LANES), F32),
                        pltpu.VMEM((bm * N_SEG, LANES), F32), pltpu.VMEM((bm * N_SEG, LANES), F32),
                        pltpu.VMEM((D_MODEL, D_EXPERT), BF16), pltpu.VMEM((D_MODEL, D_EXPERT), BF16),
                        pltpu.VMEM((D_EXPERT, D_MODEL), BF16),
                        pltpu.SemaphoreType.DMA((2,)), pltpu.SemaphoreType.DMA((2,))],
    )
    return pl.pallas_call(
        functools.partial(_expert_kernel, bm=bm),
        grid_spec=grid_spec,
        out_shape=jax.ShapeDtypeStruct((n_slots * N_SEG, LANES), F32),
        compiler_params=pltpu.CompilerParams(dimension_semantics=("arbitrary",),
                                             vmem_limit_bytes=VMEM_LIMIT,
                                             has_side_effects=True),
        name="experts",
    )(tile_expert, n_used, tile_valid, row_tok, row_tok, row_slot, row_slot, h2, wgu, wd, _split_permutation(),
      bg, bu, bd)


def _combine_kernel(y0_ref, y1_ref, y2_ref, y3_ref, gate_ref, x1_ref, g_ref, o_ref, *, tm):
    acc = x1_ref[...]
    for k, y_ref in enumerate((y0_ref, y1_ref, y2_ref, y3_ref)):
        acc = acc + gate_ref[:, k:k + 1] * _load_token_tiles(y_ref, tm)
    o_ref[...] = _rms(acc, g_ref[...])


def _combine(y_slots, gates, x1, g, tm):
    t = x1.shape[0]
    nb = t // tm

    def slot_spec(k):
        return pl.BlockSpec((tm * N_SEG, LANES), lambda i: (k * nb + i, 0))

    return pl.pallas_call(
        functools.partial(_combine_kernel, tm=tm),
        grid=(nb,),
        in_specs=[slot_spec(0), slot_spec(1), slot_spec(2), slot_spec(3),
                  pl.BlockSpec((tm, TOP_K), lambda i: (i, 0)),
                  pl.BlockSpec((tm, D_MODEL), lambda i: (i, 0)),
                  pl.BlockSpec((1, D_MODEL), lambda i: (0, 0))],
        out_specs=pl.BlockSpec((tm, D_MODEL), lambda i: (i, 0)),
        out_shape=jax.ShapeDtypeStruct((t, D_MODEL), F32),
        compiler_params=pltpu.CompilerParams(dimension_semantics=("arbitrary",),
                                             vmem_limit_bytes=VMEM_LIMIT),
        name="combine",
    )(y_slots, y_slots, y_slots, y_slots, gates, x1, g)


def _lane_row(vec, lane0):
    n = vec.shape[0]
    return jnp.zeros((SUBLANES, LANES), F32).at[0, lane0:lane0 + n].set(vec.astype(F32))


def _tile_size(n, pref):
    return pref if n % pref == 0 else n


def kernel(x, norm_mix, w_in, gdn_conv, gdn_a_log, gdn_dt_bias, gdn_norm, ml_conv, ml_b_i, ml_b_f,
           ml_norm, w_up_gdn, w_up_ml, w_out, norm_ffn, w_router, b_router, w_gate_up, b_gate_up,
           w_down, b_down, norm_final):
    assert norm_mix.shape[0] == 1, "single-layer stack"
    b, s, d = x.shape
    assert d == D_MODEL and s % CHUNK == 0
    t = b * s
    x2 = x.reshape(t, d)

    w = w_in[0]
    w_big = jnp.concatenate([_cols(w, n) for n in ("g_q", "g_k", "g_v", "g_z", "m_q", "m_k", "m_v",
                                                   "m_o", "gate_gdn", "gate_ml")], axis=1).astype(BF16)
    w_small = jnp.concatenate([_cols(w, n) for n in ("g_a", "g_b", "m_i", "m_f")], axis=1)
    w_small = jnp.pad(w_small, ((0, 0), (0, LANES - N_SMALL))).astype(BF16)

    tm = _tile_size(t, 512)
    big, small = _in_proj(x2, norm_mix[0][None, :], w_big, w_small, tm)
    big3 = big.reshape(b, s, N_BIG)
    small3 = small.reshape(b, s, LANES)

    cs = _tile_size(s, 512)
    gdn_prm = _lane_row(gdn_a_log[0], 0).at[1, 0:N_HEADS].set(gdn_dt_bias[0].astype(F32))
    oa = _gdn(big3, small3, gdn_conv[0].astype(F32), gdn_prm, gdn_norm[0][None, :].astype(F32), cs)
    ml_prm = _lane_row(ml_b_i[0], 2 * N_HEADS).at[0, 3 * N_HEADS:4 * N_HEADS].set(ml_b_f[0].astype(F32))
    ob = _mlstm(big3, small3, ml_conv[0].astype(F32), ml_prm, ml_norm[0][None, :].astype(F32), cs)

    w_r = jnp.pad(w_router[0], ((0, 0), (0, LANES - N_EXPERTS))).astype(BF16)
    b_r = jnp.full((1, LANES), PAD_LOGIT, F32).at[0, :N_EXPERTS].set(b_router[0].astype(F32))
    x1, h2, key, gates, counts = _merge(
        oa.reshape(t, D_QK), ob.reshape(t, D_QK), big, x2, w_up_gdn[0].astype(BF16),
        w_up_ml[0].astype(BF16), w_out[0].astype(BF16), norm_ffn[0][None, :], w_r, b_r, tm)

    bm = EXPERT_TILE
    n_assign = t * TOP_K
    n_tiles = -(-n_assign // bm) + N_EXPERTS
    counts = counts[0, :N_EXPERTS]
    padded = (counts + bm - 1) // bm * bm
    pend = jnp.cumsum(padded)
    pstart = pend - padded
    n_used = (pend[-1] // bm).astype(I32)
    tile_ids = jnp.minimum(jnp.arange(n_tiles, dtype=I32), n_used - 1)
    tile_expert = jnp.minimum(jnp.sum((pend[None, :] <= (tile_ids * bm)[:, None]).astype(I32), axis=1),
                              N_EXPERTS - 1)
    tile_valid = jnp.clip(counts[tile_expert] - (tile_ids * bm - pstart[tile_expert]), 0, bm).astype(I32)
    assert t <= RANK_SPAN and bm <= RANK_SPAN
    assign = jnp.arange(n_assign, dtype=I32)
    pad_e = jnp.arange(N_EXPERTS, dtype=I32)[:, None]
    pad_p = jnp.arange(bm, dtype=I32)[None, :]
    pad_key = jnp.where(pad_p < (padded - counts)[:, None],
                        pad_e * RANK_SPAN + counts[:, None] + pad_p,
                        N_EXPERTS * RANK_SPAN + pad_e * bm + pad_p)
    _, row_slot = lax.sort((jnp.concatenate([key.reshape(-1), pad_key.reshape(-1)]),
                            jnp.concatenate([(assign % TOP_K) * t + assign // TOP_K,
                                             jnp.full((N_EXPERTS * bm,), -1, I32)])),
                           num_keys=1)
    row_tok = jnp.where(row_slot < 0, jnp.arange(n_tiles * bm, dtype=I32), row_slot) % t

    bgu = b_gate_up[0]
    y_slots = _experts(tile_expert.astype(I32), n_used.reshape(1), tile_valid, row_tok, row_slot, h2,
                       w_gate_up[0], w_down[0],
                       bgu[:, None, 0::2].astype(F32), bgu[:, None, 1::2].astype(F32),
                       b_down[0][:, None, :].astype(F32), bm)
    out = _combine(y_slots, gates, x1, norm_final[None, :], tm)
    return out.reshape(b, s, d)
```
